```python
import math
import jax, jax.numpy as jnp
from jax import lax
import numpy as np

D_MODEL = 4096
BATCH = 1
SEQ = 16384
DEPTH = 4

GRID_W = 64
CTX_LEN = 256
HEAD_DIM = 128
BLOCK = 128
WINDOW = 128
ROPE_THETA = 10000.0
EPS = 1e-6
ADALN_RANK = 512
N_MOD = 6

A_HEADS = D_MODEL // (2 * HEAD_DIM)
A_KV_HEADS = A_HEADS // 4
B_HEADS = D_MODEL // (4 * HEAD_DIM)
B_KV_HEADS = B_HEADS // 2
B_DV = 2 * HEAD_DIM
C_HEADS = D_MODEL // (2 * HEAD_DIM)
C_KV_HEADS = C_HEADS // 4
D_HEADS = D_MODEL // (2 * HEAD_DIM)
D_Q_RANK = 1536
D_KV_RANK = 512
D_NOPE = 128
D_ROPE = 64
D_V = 128
D_QK = D_NOPE + D_ROPE
N_EXPERTS = 32
TOP_K = 4
EXPERT_FF = 384
SWIGLU_ALPHA = 1.702
SWIGLU_LIMIT = 7.0

EV_Q_SIZES = (A_HEADS * HEAD_DIM, B_HEADS * 2 * HEAD_DIM)
EV_KV_SIZES = (A_KV_HEADS * HEAD_DIM, A_KV_HEADS * HEAD_DIM, B_KV_HEADS * 2 * HEAD_DIM, B_KV_HEADS * B_DV)
EV_Q = sum(EV_Q_SIZES)
EV_IN = EV_Q + sum(EV_KV_SIZES)
EV_OUT = A_HEADS * HEAD_DIM + B_HEADS * B_DV
OD_Q_SIZES = (C_HEADS * HEAD_DIM, D_Q_RANK)
OD_KV_SIZES = (C_KV_HEADS * HEAD_DIM, C_KV_HEADS * HEAD_DIM, D_KV_RANK, D_ROPE)
OD_Q = sum(OD_Q_SIZES)
OD_IN = OD_Q + sum(OD_KV_SIZES)
OD_OUT = C_HEADS * HEAD_DIM + D_HEADS * D_V

kernel_name = 'hybrid_dit_swa_diff_axial_mla_moe'


def rms_norm(x, gain):
    xf = x.astype(jnp.float32)
    y = xf * lax.rsqrt(jnp.mean(xf * xf, axis=-1, keepdims=True) + EPS)
    return (y * gain.astype(jnp.float32)).astype(x.dtype)


def split_cols(p, sizes):
    out, start = [], 0
    for s in sizes:
        out.append(p[..., start:start + s])
        start += s
    return out


def axial_rope_tables(rows, dim):
    quarter = dim // 4
    inv_freq = ROPE_THETA ** (-jnp.arange(quarter, dtype=jnp.float32) / quarter)
    row = jnp.repeat(jnp.arange(rows, dtype=jnp.float32), GRID_W)
    col = jnp.tile(jnp.arange(GRID_W, dtype=jnp.float32), rows)
    ang = jnp.stack([row[:, None] * inv_freq, col[:, None] * inv_freq], axis=1)
    ang = jnp.broadcast_to(ang[:, :, None, :], (rows * GRID_W, 2, 2, quarter)).reshape(rows * GRID_W, dim)
    return jnp.cos(ang), jnp.sin(ang)


def rotate_axial_half(x):
    d = x.shape[-1]
    xs = x.reshape(x.shape[:-1] + (2, 2, d // 4))
    return jnp.stack([-xs[..., 1, :], xs[..., 0, :]], axis=-2).reshape(x.shape)


def apply_rope(x, cos, sin):
    shape = (1, x.shape[1]) + (1,) * (x.ndim - 3) + (x.shape[-1],)
    return x * cos.reshape(shape).astype(x.dtype) + rotate_axial_half(x) * sin.reshape(shape).astype(x.dtype)


def rope_tail(x, cos, sin, n_plain):
    return jnp.concatenate([x[..., :n_plain], apply_rope(x[..., n_plain:], cos, sin)], axis=-1)


def query_blocks(q):
    b, s = q.shape[:2]
    return jnp.moveaxis(q.reshape((b, s // BLOCK, BLOCK) + q.shape[2:]), 1, 0)


def unblock(o):
    nb, b, blk = o.shape[:3]
    return jnp.moveaxis(o, 0, 1).reshape((b, nb * blk) + o.shape[3:])


def blocked_attention(q, k, v, scale, sink=None):
    hk, g = q.shape[2], q.shape[3]

    def one_block(qi):
        s = jnp.einsum('bqhgd,bthd->bhgqt', qi, k, preferred_element_type=jnp.float32) * scale
        if sink is not None:
            sl = jnp.broadcast_to(sink.reshape(hk, g, 1, 1).astype(jnp.float32), s.shape[:-1] + (1,))
            p = jax.nn.softmax(jnp.concatenate([s, sl], axis=-1), axis=-1)[..., :-1]
        else:
            p = jax.nn.softmax(s, axis=-1)
        return jnp.einsum('bhgqt,bthe->bqhge', p.astype(v.dtype), v)

    return unblock(lax.map(one_block, query_blocks(q)))


def window_sink_attention(q, k, v, k_ctx, v_ctx, sink, scale):
    n, hk, g = q.shape[1], q.shape[2], q.shape[3]
    pad = ((0, 0), (BLOCK, BLOCK), (0, 0), (0, 0))
    kp = jnp.pad(k, pad)
    vp = jnp.pad(v, pad)
    n_loc = 3 * BLOCK
    n_ctx = k_ctx.shape[1]
    offs = jnp.arange(n_loc)[None, :] - BLOCK - jnp.arange(BLOCK)[:, None]
    in_window = jnp.abs(offs) <= WINDOW

    def one_block(args):
        i, qi = args
        kw = lax.dynamic_slice_in_dim(kp, i * BLOCK, n_loc, axis=1)
        vw = lax.dynamic_slice_in_dim(vp, i * BLOCK, n_loc, axis=1)
        kpos = (i - 1) * BLOCK + jnp.arange(n_loc)
        valid = in_window & ((kpos >= 0) & (kpos < n))[None, :]
        s_loc = jnp.einsum('bqhgd,bjhd->bhgqj', qi, kw, preferred_element_type=jnp.float32) * scale
        s_loc = jnp.where(valid, s_loc, -jnp.inf)
        s_ctx = jnp.einsum('bqhgd,bchd->bhgqc', qi, k_ctx, preferred_element_type=jnp.float32) * scale
        sl = jnp.broadcast_to(sink.reshape(hk, g, 1, 1).astype(jnp.float32), s_ctx.shape[:-1] + (1,))
        p = jax.nn.softmax(jnp.concatenate([s_loc, s_ctx, sl], axis=-1), axis=-1).astype(v.dtype)
        return (jnp.einsum('bhgqj,bjhe->bqhge', p[..., :n_loc], vw)
                + jnp.einsum('bhgqc,bche->bqhge', p[..., n_loc:n_loc + n_ctx], v_ctx))

    return unblock(lax.map(one_block, (jnp.arange(n // BLOCK), query_blocks(q))))


def diff_attention(q, k, v, lam, scale):
    def one_block(qi):
        s = jnp.einsum('bqhgcd,bthcd->bhgcqt', qi, k, preferred_element_type=jnp.float32) * scale
        p = jax.nn.softmax(s, axis=-1)
        a = p[:, :, :, 0] - lam * p[:, :, :, 1]
        return jnp.einsum('bhgqt,bthe->bqhge', a.astype(v.dtype), v)

    return unblock(lax.map(one_block, query_blocks(q)))


def mixer_window_diff(hc, hl, cos, sin, w_in, w_out, a_qn, a_kn, a_sink, b_qn, b_kn, b_lam, b_subln, lam_init, need_ctx):
    bsz = hl.shape[0]
    scale = HEAD_DIM ** -0.5
    lf = b_lam.astype(jnp.float32)
    lam = jnp.exp(jnp.sum(lf[0] * lf[1])) - jnp.exp(jnp.sum(lf[2] * lf[3])) + lam_init

    def shape_q(pq):
        n = pq.shape[1]
        qa, qb = split_cols(pq, EV_Q_SIZES)
        qa = rms_norm(qa.reshape(bsz, n, A_KV_HEADS, A_HEADS // A_KV_HEADS, HEAD_DIM), a_qn)
        qb = rms_norm(qb.reshape(bsz, n, B_KV_HEADS, B_HEADS // B_KV_HEADS, 2, HEAD_DIM), b_qn)
        return qa, qb

    def shape_kv(pkv):
        n = pkv.shape[1]
        ka, va, kb, vb = split_cols(pkv, EV_KV_SIZES)
        ka = rms_norm(ka.reshape(bsz, n, A_KV_HEADS, HEAD_DIM), a_kn)
        va = va.reshape(bsz, n, A_KV_HEADS, HEAD_DIM)
        kb = rms_norm(kb.reshape(bsz, n, B_KV_HEADS, 2, HEAD_DIM), b_kn)
        vb = vb.reshape(bsz, n, B_KV_HEADS, B_DV)
        return ka, va, kb, vb

    def merge(ya, yb):
        n = ya.shape[1]
        yb = rms_norm(yb.reshape(bsz, n, B_HEADS, B_DV), b_subln) * (1.0 - lam_init)
        return jnp.concatenate([ya.reshape(bsz, n, -1), yb.reshape(bsz, n, -1)], axis=-1) @ w_out

    ka_c, va_c, kb_c, vb_c = shape_kv(hc @ w_in[:, EV_Q:])
    p_l = hl @ w_in
    qa_l, qb_l = shape_q(p_l[..., :EV_Q])
    ka_l, va_l, kb_l, vb_l = shape_kv(p_l[..., EV_Q:])
    qa_l, ka_l = apply_rope(qa_l, cos, sin), apply_rope(ka_l, cos, sin)
    qb_l, kb_l = apply_rope(qb_l, cos, sin), apply_rope(kb_l, cos, sin)
    ya_l = window_sink_attention(qa_l, ka_l, va_l, ka_c, va_c, a_sink, scale)
    yb_l = diff_attention(qb_l, jnp.concatenate([kb_c, kb_l], axis=1), jnp.concatenate([vb_c, vb_l], axis=1), lam, scale)
    y_l = merge(ya_l, yb_l)
    if not need_ctx:
        return None, y_l
    qa_c, qb_c = shape_q(hc @ w_in[:, :EV_Q])
    y_c = merge(blocked_attention(qa_c, ka_c, va_c, scale, sink=a_sink), diff_attention(qb_c, kb_c, vb_c, lam, scale))
    return y_c, y_l


def mixer_axial_mla(hc, hl, cos, sin, cos_r, sin_r, w_in, w_out, c_qn, c_kn, d_qa_norm, d_kva_norm,
                    d_wq_up, d_wkv_up, d_qn, d_kn, need_ctx):
    bsz = hl.shape[0]

    def shape_q(pq, rope):
        n = pq.shape[1]
        qc, qd_a = split_cols(pq, OD_Q_SIZES)
        qc = rms_norm(qc.reshape(bsz, n, C_KV_HEADS, C_HEADS // C_KV_HEADS, HEAD_DIM), c_qn)
        qd = rms_norm((rms_norm(qd_a, d_qa_norm) @ d_wq_up).reshape(bsz, n, D_HEADS, 1, D_QK), d_qn)
        if rope:
            qc = apply_rope(qc, cos, sin)
            qd = rope_tail(qd, cos_r, sin_r, D_NOPE)
        return qc, qd

    def shape_kv(pkv, rope):
        n = pkv.shape[1]
        kc, vc, kv_a, kr = split_cols(pkv, OD_KV_SIZES)
        kc = rms_norm(kc.reshape(bsz, n, C_KV_HEADS, HEAD_DIM), c_kn)
        vc = vc.reshape(bsz, n, C_KV_HEADS, HEAD_DIM)
        kv = (rms_norm(kv_a, d_kva_norm) @ d_wkv_up).reshape(bsz, n, D_HEADS, D_NOPE + D_V)
        kr = jnp.broadcast_to(kr[:, :, None, :], (bsz, n, D_HEADS, D_ROPE))
        kd = rms_norm(jnp.concatenate([kv[..., :D_NOPE], kr], axis=-1), d_kn)
        vd = kv[..., D_NOPE:]
        if rope:
            kc = apply_rope(kc, cos, sin)
            kd = rope_tail(kd, cos_r, sin_r, D_NOPE)
        return kc, vc, kd, vd

    def merge(yc, yd):
        n = yc.shape[1]
        return jnp.concatenate([yc.reshape(bsz, n, -1), yd.reshape(bsz, n, -1)], axis=-1) @ w_out

    kc_c, vc_c, kd_c, vd_c = shape_kv(hc @ w_in[:, OD_Q:], False)
    p_l = hl @ w_in
    qc_l, qd_l = shape_q(p_l[..., :OD_Q], True)
    kc_l, vc_l, kd_l, vd_l = shape_kv(p_l[..., OD_Q:], True)
    cat = lambda a, b: jnp.concatenate([a, b], axis=1)
    y_l = merge(blocked_attention(qc_l, cat(kc_c, kc_l), cat(vc_c, vc_l), HEAD_DIM ** -0.5),
                blocked_attention(qd_l, cat(kd_c, kd_l), cat(vd_c, vd_l), D_QK ** -0.5))
    if not need_ctx:
        return None, y_l
    qc_c, qd_c = shape_q(hc @ w_in[:, :OD_Q], False)
    y_c = merge(blocked_attention(qc_c, kc_c, vc_c, HEAD_DIM ** -0.5),
                blocked_attention(qd_c, kd_c, vd_c, D_QK ** -0.5))
    return y_c, y_l


def moe_ffn(h, router_w, router_b, w_gu, b_gu, w_down, b_down):
    shp = h.shape
    t = h.reshape(-1, shp[-1])
    logits = jnp.dot(t, router_w, preferred_element_type=jnp.float32) + router_b.astype(jnp.float32)
    top_val, top_idx = lax.top_k(logits, TOP_K)
    top_w = jax.nn.softmax(top_val, axis=-1)
    gates = jnp.einsum('nk,nke->ne', top_w, jax.nn.one_hot(top_idx, N_EXPERTS, dtype=jnp.float32)).astype(t.dtype)
    out = jnp.zeros_like(t)
    for e in range(N_EXPERTS):
        gu = t @ w_gu[e] + b_gu[e]
        glu = jnp.minimum(gu[:, :EXPERT_FF], SWIGLU_LIMIT)
        lin = jnp.clip(gu[:, EXPERT_FF:], -SWIGLU_LIMIT, SWIGLU_LIMIT)
        act = glu * jax.nn.sigmoid(SWIGLU_ALPHA * glu) * (lin + 1.0)
        out = out + gates[:, e:e + 1] * (act @ w_down[e] + b_down[e])
    return out.reshape(shp)


def adaln(cvec, down, up, bias):
    return (jax.nn.silu(cvec) @ down) @ up + bias


def modulate(t, gain, shift, scale):
    return rms_norm(t, gain) * (1.0 + scale) + shift


def setup_inputs(seed: int = 0) -> dict:
    key = jax.random.key(seed)
    ks = iter(jax.random.split(key, 40))
    f32 = jnp.float32
    D = D_MODEL
    n_ev = (DEPTH + 1) // 2
    n_od = DEPTH // 2

    def nrm(shape, scale):
        return jax.random.normal(next(ks), shape, f32) * scale

    def gain(shape):
        return 1.0 + 0.02 * jax.random.normal(next(ks), shape, f32)

    return {
        'x': nrm((BATCH, SEQ, D), 1.0),
        'c': nrm((BATCH, D), 1.0),
        'ctx': nrm((BATCH, CTX_LEN, D), 1.0),
        'c_ctx': nrm((D,), 1.0),
        'adaln_down': nrm((DEPTH, D, ADALN_RANK), D ** -0.5),
        'adaln_up': nrm((DEPTH, ADALN_RANK, N_MOD * D), 0.5 * ADALN_RANK ** -0.5),
        'adaln_b': nrm((DEPTH, N_MOD * D), 0.01),
        'norm_mix': gain((DEPTH, D)),
        'norm_ffn': gain((DEPTH, D)),
        'ev_w_in': nrm((n_ev, D, EV_IN), D ** -0.5),
        'ev_w_out': nrm((n_ev, EV_OUT, D), EV_OUT ** -0.5),
        'ev_a_qn': gain((n_ev, HEAD_DIM)),
        'ev_a_kn': gain((n_ev, HEAD_DIM)),
        'ev_a_sink': nrm((n_ev, A_HEADS), 0.5),
        'ev_b_qn': gain((n_ev, HEAD_DIM)),
        'ev_b_kn': gain((n_ev, HEAD_DIM)),
        'ev_b_lam': nrm((n_ev, 4, HEAD_DIM), 0.1),
        'ev_b_subln': gain((n_ev, B_DV)),
        'od_w_in': nrm((n_od, D, OD_IN), D ** -0.5),
        'od_w_out': nrm((n_od, OD_OUT, D), OD_OUT ** -0.5),
        'od_c_qn': gain((n_od, HEAD_DIM)),
        'od_c_kn': gain((n_od, HEAD_DIM)),
        'od_d_qa_norm': gain((n_od, D_Q_RANK)),
        'od_d_kva_norm': gain((n_od, D_KV_RANK)),
        'od_d_wq_up': nrm((n_od, D_Q_RANK, D_HEADS * D_QK), D_Q_RANK ** -0.5),
        'od_d_wkv_up': nrm((n_od, D_KV_RANK, D_HEADS * (D_NOPE + D_V)), D_KV_RANK ** -0.5),
        'od_d_qn': gain((n_od, D_QK)),
        'od_d_kn': gain((n_od, D_QK)),
        'router_w': nrm((DEPTH, D, N_EXPERTS), D ** -0.5),
        'router_b': nrm((DEPTH, N_EXPERTS), 0.01),
        'moe_w_gu': nrm((DEPTH, N_EXPERTS, D, 2 * EXPERT_FF), D ** -0.5),
        'moe_b_gu': nrm((DEPTH, N_EXPERTS, 2 * EXPERT_FF), 0.01),
        'moe_w_down': nrm((DEPTH, N_EXPERTS, EXPERT_FF, D), EXPERT_FF ** -0.5),
        'moe_b_down': nrm((DEPTH, N_EXPERTS, D), 0.01),
    }


def reference(x, c, ctx, c_ctx, adaln_down, adaln_up, adaln_b, norm_mix, norm_ffn,
              ev_w_in, ev_w_out, ev_a_qn, ev_a_kn, ev_a_sink, ev_b_qn, ev_b_kn, ev_b_lam, ev_b_subln,
              od_w_in, od_w_out, od_c_qn, od_c_kn, od_d_qa_norm, od_d_kva_norm, od_d_wq_up, od_d_wkv_up,
              od_d_qn, od_d_kn, router_w, router_b, moe_w_gu, moe_b_gu, moe_w_down, moe_b_down):
    rows = x.shape[1] // GRID_W
    cos, sin = axial_rope_tables(rows, HEAD_DIM)
    cos_r, sin_r = axial_rope_tables(rows, D_ROPE)
    xc = ctx
    for l in range(DEPTH):
        need_ctx = l < DEPTH - 1
        m_lat = jnp.split(adaln(c, adaln_down[l], adaln_up[l], adaln_b[l])[:, None, :], N_MOD, axis=-1)
        m_ctx = jnp.split(adaln(c_ctx, adaln_down[l], adaln_up[l], adaln_b[l])[None, None, :], N_MOD, axis=-1)
        hl = modulate(x, norm_mix[l], m_lat[0], m_lat[1])
        hc = modulate(xc, norm_mix[l], m_ctx[0], m_ctx[1])
        if l % 2 == 0:
            j = l // 2
            lam_init = 0.8 - 0.6 * math.exp(-0.3 * l)
            y_c, y_l = mixer_window_diff(hc, hl, cos, sin, ev_w_in[j], ev_w_out[j], ev_a_qn[j], ev_a_kn[j],
                                         ev_a_sink[j], ev_b_qn[j], ev_b_kn[j], ev_b_lam[j], ev_b_subln[j],
                                         lam_init, need_ctx)
        else:
            j = l // 2
            y_c, y_l = mixer_axial_mla(hc, hl, cos, sin, cos_r, sin_r, od_w_in[j], od_w_out[j], od_c_qn[j],
                                       od_c_kn[j], od_d_qa_norm[j], od_d_kva_norm[j], od_d_wq_up[j],
                                       od_d_wkv_up[j], od_d_qn[j], od_d_kn[j], need_ctx)
        x = x + m_lat[2] * y_l
        hl = modulate(x, norm_ffn[l], m_lat[3], m_lat[4])
        if need_ctx:
            xc = xc + m_ctx[2] * y_c
            hc = modulate(xc, norm_ffn[l], m_ctx[3], m_ctx[4])
            n_ctx = hc.shape[1]
            y = moe_ffn(jnp.concatenate([hc, hl], axis=1), router_w[l], router_b[l], moe_w_gu[l], moe_b_gu[l],
                        moe_w_down[l], moe_b_down[l])
            xc = xc + m_ctx[5] * y[:, :n_ctx]
            x = x + m_lat[5] * y[:, n_ctx:]
        else:
            x = x + m_lat[5] * moe_ffn(hl, router_w[l], router_b[l], moe_w_gu[l], moe_b_gu[l],
                                       moe_w_down[l], moe_b_down[l])
    return x
```

```python
import functools
import math

import jax
import jax.numpy as jnp
from jax import lax
from jax.experimental import pallas as pl
from jax.experimental.pallas import tpu as pltpu

F32 = jnp.float32
BF16 = jnp.bfloat16

GRID_W = 64
HEAD_DIM = 128
ROPE_THETA = 10000.0
EPS = 1e-6
N_MOD = 6
D_NOPE = 128
D_ROPE = 64
D_QK = D_NOPE + D_ROPE
D_V = 128
D_PAD = 256
TOP_K = 4
SWIGLU_ALPHA = 1.702
SWIGLU_LIMIT = 7.0
LOG2E = 1.4426950408889634
NEG_BIG = -1e30

LANES = 128
ROW_TILE = 256
VMEM_LIMIT = 56 * 1024 * 1024


def _cparams(*sem):
    return pltpu.CompilerParams(dimension_semantics=sem, vmem_limit_bytes=VMEM_LIMIT)


def _adaln_kernel(c_ref, down_ref, up_ref, b_ref, o_ref, t_ref):
    @pl.when(pl.program_id(1) == 0)
    def _():
        c = c_ref[...]
        a = c * (1.0 / (1.0 + jnp.exp(-c)))
        t_ref[...] = jnp.dot(a, down_ref[0], preferred_element_type=F32, precision=lax.Precision.HIGHEST)

    o_ref[0] = jnp.dot(t_ref[...], up_ref[0], preferred_element_type=F32,
                       precision=lax.Precision.HIGHEST) + b_ref[0]


def adaln_all(cvecs, down, up, bias):
    depth, d, rank = down.shape
    n = up.shape[2]
    tn = 2048
    assert n % tn == 0
    return pl.pallas_call(
        _adaln_kernel,
        grid=(depth, n // tn),
        in_specs=[
            pl.BlockSpec((8, d), lambda l, j: (0, 0)),
            pl.BlockSpec((1, d, rank), lambda l, j: (l, 0, 0)),
            pl.BlockSpec((1, rank, tn), lambda l, j: (l, 0, j)),
            pl.BlockSpec((1, 1, tn), lambda l, j: (l, 0, j)),
        ],
        out_specs=pl.BlockSpec((1, 8, tn), lambda l, j: (l, 0, j)),
        out_shape=jax.ShapeDtypeStruct((depth, 8, n), F32),
        scratch_shapes=[pltpu.VMEM((8, rank), F32)],
        compiler_params=_cparams("arbitrary", "arbitrary"),
        name="adaln",
    )(cvecs, down, up, bias.reshape(depth, 1, n))


def _topk_gates(logits, n_exp):
    lane = lax.broadcasted_iota(jnp.int32, logits.shape, 1).astype(F32)
    work = jnp.where(lane < n_exp, logits, -jnp.inf)
    gates = jnp.zeros_like(logits)
    denom = None
    v0 = None
    for k in range(TOP_K):
        m = jnp.max(work, axis=-1, keepdims=True)
        idx = jnp.min(jnp.where(work == m, lane, float(LANES)), axis=-1, keepdims=True)
        sel = lane == idx
        if k == 0:
            v0 = m
            e = jnp.ones_like(m)
            denom = e
        else:
            e = jnp.exp(m - v0)
            denom = denom + e
        gates = jnp.where(sel, e, gates)
        work = jnp.where(sel, -jnp.inf, work)
    return gates / denom


def _modulate_kernel(*refs, has_res, n_exp, ctx_rows):
    it = iter(refs)
    x_ref = next(it)
    y_ref = next(it) if has_res else None
    gate_ref = next(it) if has_res else None
    mult_ref = next(it)
    shift_ref = next(it)
    rw_ref = next(it) if n_exp else None
    rb_ref = next(it) if n_exp else None
    xo_ref = next(it) if has_res else None
    h_ref = next(it)
    g_ref = next(it) if n_exp else None

    x = x_ref[...]
    if has_res:
        x = x + gate_ref[0] * y_ref[...].astype(F32)
        xo_ref[...] = x
    ms = jnp.mean(x * x, axis=-1, keepdims=True)
    h = x * lax.rsqrt(ms + EPS) * mult_ref[0] + shift_ref[0]
    h_ref[...] = h.astype(BF16)
    if n_exp:
        logits = jnp.dot(h, rw_ref[...], preferred_element_type=F32,
                         precision=lax.Precision.HIGHEST) + rb_ref[...]
        g_ref[...] = _topk_gates(logits, n_exp)


def modulate(x, mult, shift, res=None, router=None):
    st, d = x.shape
    tm = ROW_TILE
    stream = lambda i: (jnp.minimum(i, 1), 0, 0)
    row = lambda i: (i, 0)
    vec_spec = pl.BlockSpec((1, 1, d), stream)
    args, in_specs = [x], [pl.BlockSpec((tm, d), row)]
    if res is not None:
        args += [res[0], res[1]]
        in_specs += [pl.BlockSpec((tm, d), row), vec_spec]
    args += [mult, shift]
    in_specs += [vec_spec, vec_spec]
    n_exp = 0
    if router is not None:
        args += [router[0], router[1]]
        n_exp = router[2]
        in_specs += [pl.BlockSpec((d, LANES), lambda i: (0, 0)), pl.BlockSpec((1, LANES), lambda i: (0, 0))]
    out_shape, out_specs = [], []
    if res is not None:
        out_shape.append(jax.ShapeDtypeStruct((st, d), F32))
        out_specs.append(pl.BlockSpec((tm, d), row))
    out_shape.append(jax.ShapeDtypeStruct((st, d), BF16))
    out_specs.append(pl.BlockSpec((tm, d), row))
    if router is not None:
        out_shape.append(jax.ShapeDtypeStruct((st, LANES), F32))
        out_specs.append(pl.BlockSpec((tm, LANES), row))
    return pl.pallas_call(
        functools.partial(_modulate_kernel, has_res=res is not None, n_exp=n_exp, ctx_rows=ROW_TILE),
        grid=(st // tm,),
        in_specs=in_specs,
        out_specs=out_specs,
        out_shape=out_shape,
        compiler_params=_cparams("arbitrary"),
        name="modulate",
    )(*args)


def _final_residual_kernel(x_ref, y_ref, gate_ref, o_ref):
    o_ref[...] = x_ref[...] + gate_ref[0] * y_ref[...].astype(F32)


def final_residual(x, y, gate):
    st, d = x.shape
    tm = ROW_TILE
    n = st // tm - 1
    return pl.pallas_call(
        _final_residual_kernel,
        grid=(n,),
        in_specs=[
            pl.BlockSpec((tm, d), lambda i: (i + 1, 0)),
            pl.BlockSpec((tm, d), lambda i: (i + 1, 0)),
            pl.BlockSpec((1, 1, d), lambda i: (1, 0, 0)),
        ],
        out_specs=pl.BlockSpec((tm, d), lambda i: (i, 0)),
        out_shape=jax.ShapeDtypeStruct((n * tm, d), F32),
        compiler_params=_cparams("arbitrary"),
        name="final_residual",
    )(x, y, gate)


def _mm_kernel(*refs, n_pairs):
    o_ref = refs[-1]
    acc = None
    for p in range(n_pairs):
        t = jnp.dot(refs[p][...], refs[n_pairs + p][...], preferred_element_type=F32)
        acc = t if acc is None else acc + t
    o_ref[...] = acc.astype(o_ref.dtype)


def _row_block(m):
    for tm in (1280, 1024, 512, 256):
        if m % tm == 0:
            return tm
    raise ValueError(m)


def matmul(a_list, b_list, tn, out_dtype=BF16):
    m = a_list[0].shape[0]
    n = b_list[0].shape[1]
    tm = _row_block(m)
    while n % tn:
        tn -= LANES
    in_specs = [pl.BlockSpec((tm, a.shape[1]), lambda i, j: (i, 0)) for a in a_list]
    in_specs += [pl.BlockSpec((b.shape[0], tn), lambda i, j: (0, j)) for b in b_list]
    return pl.pallas_call(
        functools.partial(_mm_kernel, n_pairs=len(a_list)),
        grid=(m // tm, n // tn),
        in_specs=in_specs,
        out_specs=pl.BlockSpec((tm, tn), lambda i, j: (i, j)),
        out_shape=jax.ShapeDtypeStruct((m, n), out_dtype),
        compiler_params=_cparams("arbitrary", "arbitrary"),
        name="matmul",
    )(*a_list, *b_list)


def _rope(y, cos, sin_lo, sin_hi, quarter):
    left = pltpu.roll(y, LANES - quarter, 1)
    right = pltpu.roll(y, quarter, 1)
    return y * cos + left * sin_lo + right * sin_hi


def _head_norm(x, gain, n_valid):
    ss = jnp.sum(x * x, axis=-1, keepdims=True)
    return x * lax.rsqrt(ss * (1.0 / n_valid) + EPS) * gain


def _prep_even_kernel(p_ref, cos_ref, slo_ref, shi_ref, gains_ref, o_ref, *, segs, n_copy_from):
    cos, slo, shi = cos_ref[...], slo_ref[...], shi_ref[...]
    for (c0, c1, gi, scale) in segs:
        gain = gains_ref[gi:gi + 1, :]
        for c in range(c0, c1):
            x = p_ref[:, c * LANES:(c + 1) * LANES].astype(F32)
            y = _rope(_head_norm(x, gain, HEAD_DIM), cos, slo, shi, HEAD_DIM // 4)
            if scale != 1.0:
                y = y * scale
            o_ref[:, c * LANES:(c + 1) * LANES] = y.astype(BF16)
    for (c0, c1) in n_copy_from:
        o_ref[:, c0 * LANES:c1 * LANES] = p_ref[:, c0 * LANES:c1 * LANES]


def prep_heads(p, rope128, gains, segs, copies):
    st, n = p.shape
    tm = ROW_TILE
    tab = pl.BlockSpec((tm, LANES), lambda i: (i, 0))
    return pl.pallas_call(
        functools.partial(_prep_even_kernel, segs=segs, n_copy_from=copies),
        grid=(st // tm,),
        in_specs=[pl.BlockSpec((tm, n), lambda i: (i, 0)), tab, tab, tab,
                  pl.BlockSpec(gains.shape, lambda i: (0, 0))],
        out_specs=pl.BlockSpec((tm, n), lambda i: (i, 0)),
        out_shape=jax.ShapeDtypeStruct((st, n), BF16),
        compiler_params=_cparams("arbitrary"),
        name="prep_heads",
    )(p, *rope128, gains)


def _prep_odd1_kernel(p_ref, cos_ref, slo_ref, shi_ref, gains_ref, gqa_ref, gkva_ref,
                      qk_ref, qa_ref, kva_ref, *, n_q, n_k, c_k, c_qa, n_qa, c_kva, n_kva, q_scale):
    cos, slo, shi = cos_ref[...], slo_ref[...], shi_ref[...]
    for c in range(n_q):
        x = p_ref[:, c * LANES:(c + 1) * LANES].astype(F32)
        y = _rope(_head_norm(x, gains_ref[0:1, :], HEAD_DIM), cos, slo, shi, HEAD_DIM // 4) * q_scale
        qk_ref[:, c * LANES:(c + 1) * LANES] = y.astype(BF16)
    for c in range(n_k):
        x = p_ref[:, (c_k + c) * LANES:(c_k + c + 1) * LANES].astype(F32)
        y = _rope(_head_norm(x, gains_ref[1:2, :], HEAD_DIM), cos, slo, shi, HEAD_DIM // 4)
        qk_ref[:, (n_q + c) * LANES:(n_q + c + 1) * LANES] = y.astype(BF16)
    xa = p_ref[:, c_qa * LANES:(c_qa + n_qa) * LANES].astype(F32)
    qa_ref[...] = _head_norm(xa, gqa_ref[...], n_qa * LANES).astype(BF16)
    xk = p_ref[:, c_kva * LANES:(c_kva + n_kva) * LANES].astype(F32)
    kva_ref[...] = _head_norm(xk, gkva_ref[...], n_kva * LANES).astype(BF16)


def _prep_odd2_kernel(qd_ref, kv_ref, kr_ref, cos_ref, slo_ref, shi_ref, gq_ref, gk_ref,
                      qo_ref, ko_ref, *, n_heads, q_scale):
    cos, slo, shi = cos_ref[...], slo_ref[...], shi_ref[...]
    gq_n, gq_r = gq_ref[:, :LANES], gq_ref[:, LANES:]
    gk_n, gk_r = gk_ref[:, :LANES], gk_ref[:, LANES:]
    kr = kr_ref[...].astype(F32)
    kr_ss = jnp.sum(kr * kr, axis=-1, keepdims=True)
    for h in range(n_heads):
        b = h * D_PAD
        qn = qd_ref[:, b:b + LANES].astype(F32)
        qr = qd_ref[:, b + LANES:b + D_PAD].astype(F32)
        ss = jnp.sum(qn * qn, axis=-1, keepdims=True) + jnp.sum(qr * qr, axis=-1, keepdims=True)
        r = lax.rsqrt(ss * (1.0 / D_QK) + EPS) * q_scale
        qo_ref[:, b:b + LANES] = (qn * r * gq_n).astype(BF16)
        qo_ref[:, b + LANES:b + D_PAD] = _rope(qr * r * gq_r, cos, slo, shi, D_ROPE // 4).astype(BF16)
        kn = kv_ref[:, b:b + LANES].astype(F32)
        ss = jnp.sum(kn * kn, axis=-1, keepdims=True) + kr_ss
        r = lax.rsqrt(ss * (1.0 / D_QK) + EPS)
        ko_ref[:, b:b + LANES] = (kn * r * gk_n).astype(BF16)
        ko_ref[:, b + LANES:b + D_PAD] = _rope(kr * r * gk_r, cos, slo, shi, D_ROPE // 4).astype(BF16)


def _stack_heads(q_ref, g, d):
    return jnp.concatenate([q_ref[:, i * d:(i + 1) * d] for i in range(g)], axis=0)


def _qk(q, k):
    return lax.dot_general(q, k, (((1,), (1,)), ((), ())), preferred_element_type=F32)


def _window_attn_kernel(q_ref, kp_ref, km_ref, kn_ref, kc_ref, vp_ref, vm_ref, vn_ref, vc_ref, sink_ref,
                        o_ref, *, g, st):
    i = pl.program_id(1)
    tq = q_ref.shape[0]
    half = tq // 2
    qs = _stack_heads(q_ref, g, HEAD_DIM)
    k_loc = jnp.concatenate([kp_ref[...], km_ref[...], kn_ref[...]], axis=0)
    v_loc = jnp.concatenate([vp_ref[...], vm_ref[...], vn_ref[...]], axis=0)
    s_loc = _qk(qs, k_loc)
    s_ctx = _qk(qs, kc_ref[...])
    a = lax.broadcasted_iota(jnp.int32, s_loc.shape, 0) & (tq - 1)
    j = lax.broadcasted_iota(jnp.int32, s_loc.shape, 1)
    diff = j - half - a
    krow = i * tq - half + j
    valid = (jnp.abs(diff) <= half) & (krow >= tq) & (krow < st) & (i >= 1)
    s_loc = jnp.where(valid, s_loc, NEG_BIG)
    sink = jnp.concatenate([jnp.broadcast_to(sink_ref[0, h:h + 1, 0:1], (tq, 1)) for h in range(g)], axis=0)
    m = jnp.maximum(jnp.maximum(jnp.max(s_loc, axis=-1, keepdims=True),
                                jnp.max(s_ctx, axis=-1, keepdims=True)), sink)
    p_loc = jnp.exp2(s_loc - m)
    p_ctx = jnp.exp2(s_ctx - m)
    l = jnp.sum(p_loc, axis=-1, keepdims=True) + jnp.sum(p_ctx, axis=-1, keepdims=True) + jnp.exp2(sink - m)
    o = (jnp.dot(p_loc.astype(BF16), v_loc, preferred_element_type=F32)
         + jnp.dot(p_ctx.astype(BF16), vc_ref[...], preferred_element_type=F32)) / l
    for h in range(g):
        o_ref[:, h * HEAD_DIM:(h + 1) * HEAD_DIM] = o[h * tq:(h + 1) * tq].astype(BF16)


def window_attention(pn, sink, n_kv, g, c_q, c_k, c_v):
    st = pn.shape[0]
    tq = ROW_TILE
    half = tq // 2
    nh = st // half
    q_spec = pl.BlockSpec((tq, g * HEAD_DIM), lambda h, i: (i, c_q // g + h))

    def kv_specs(c0):
        return [
            pl.BlockSpec((half, HEAD_DIM), lambda h, i: (jnp.maximum(2 * i - 1, 0), c0 + h)),
            pl.BlockSpec((tq, HEAD_DIM), lambda h, i: (i, c0 + h)),
            pl.BlockSpec((half, HEAD_DIM), lambda h, i: (jnp.minimum(2 * i + 2, nh - 1), c0 + h)),
            pl.BlockSpec((tq, HEAD_DIM), lambda h, i: (0, c0 + h)),
        ]

    return pl.pallas_call(
        functools.partial(_window_attn_kernel, g=g, st=st),
        grid=(n_kv, st // tq),
        in_specs=[q_spec] + kv_specs(c_k) + kv_specs(c_v)
        + [pl.BlockSpec((1, g, LANES), lambda h, i: (h, 0, 0))],
        out_specs=pl.BlockSpec((tq, g * HEAD_DIM), lambda h, i: (i, h)),
        out_shape=jax.ShapeDtypeStruct((st, n_kv * g * HEAD_DIM), BF16),
        compiler_params=_cparams("arbitrary", "arbitrary"),
        name="window_attention",
    )(pn, pn, pn, pn, pn, pn, pn, pn, pn, sink)


def _flash_kernel(q_ref, k_ref, v_ref, aux_ref, o_ref, acc_ref, *, g, comps, dq, dv, tk, diff):
    qi = pl.program_id(1)
    tq = q_ref.shape[0]
    n_keys = k_ref.shape[0]
    n_lat = (n_keys - tq) // tk
    heads = [(gi, c) for c in range(comps) for gi in range(g)]
    qs = [jnp.concatenate([q_ref[:, (gi * comps + c) * dq:(gi * comps + c + 1) * dq] for gi in range(g)], axis=0)
          for c in range(comps)]
    rows = len(heads) * tq

    def scores(start, size):
        return jnp.concatenate(
            [_qk(qs[c], k_ref[pl.ds(start, size), c * dq:(c + 1) * dq]) for c in range(comps)], axis=0)

    def step(start, size, m, l, first):
        s = scores(start, size)
        m_cur = jnp.max(s, axis=-1, keepdims=True)
        m_new = m_cur if first else jnp.maximum(m, m_cur)
        p = jnp.exp2(s - m_new)
        pv = jnp.dot(p.astype(BF16), v_ref[pl.ds(start, size), :], preferred_element_type=F32)
        if first:
            l_new = jnp.sum(p, axis=-1, keepdims=True)
            acc_ref[...] = pv
        else:
            alpha = jnp.exp2(m - m_new)
            l_new = alpha * l + jnp.sum(p, axis=-1, keepdims=True)
            acc_ref[...] = alpha * acc_ref[...] + pv
        return m_new, l_new

    m, l = step(0, tq, None, None, True)

    def body(j, carry):
        start = pl.multiple_of(tq + j * tk, tk)
        return step(start, tk, carry[0], carry[1], False)

    m, l = lax.fori_loop(0, jnp.where(qi == 0, 0, n_lat), body, (m, l))
    o = acc_ref[...] / l
    if diff:
        lam = aux_ref[0:1, 0:1]
        for gi in range(g):
            y = o[gi * tq:(gi + 1) * tq] - lam * o[(g + gi) * tq:(g + gi + 1) * tq]
            ss = jnp.mean(y * y, axis=-1, keepdims=True)
            y = y * lax.rsqrt(ss + EPS) * aux_ref[1:2, :]
            o_ref[:, gi * dv:(gi + 1) * dv] = y.astype(BF16)
    else:
        for gi in range(g):
            o_ref[:, gi * dv:(gi + 1) * dv] = o[gi * tq:(gi + 1) * tq].astype(BF16)


def flash_attention(q_arr, k_arr, v_arr, aux, *, n_kv, g, comps, dq, dv, c_q, c_k, c_v, tk, diff=False):
    st = q_arr.shape[0]
    tq = ROW_TILE
    assert (st - tq) % tk == 0
    rows = g * comps * tq
    return pl.pallas_call(
        functools.partial(_flash_kernel, g=g, comps=comps, dq=dq, dv=dv, tk=tk, diff=diff),
        grid=(n_kv, st // tq),
        in_specs=[
            pl.BlockSpec((tq, g * comps * dq), lambda h, i: (i, c_q + h)),
            pl.BlockSpec((st, comps * dq), lambda h, i: (0, c_k + h)),
            pl.BlockSpec((st, dv), lambda h, i: (0, c_v(h))),
            pl.BlockSpec(aux.shape, lambda h, i: (0, 0)),
        ],
        out_specs=pl.BlockSpec((tq, g * dv), lambda h, i: (i, h)),
        out_shape=jax.ShapeDtypeStruct((st, n_kv * g * dv), BF16),
        scratch_shapes=[pltpu.VMEM((rows, dv), F32)],
        compiler_params=_cparams("arbitrary", "arbitrary"),
        name="flash_attention",
    )(q_arr, k_arr, v_arr, aux)


def _moe_kernel(h_ref, gates_ref, wgu_ref, bgu_ref, wd_ref, bd_ref, o_ref, acc_ref, *, ff):
    e = pl.program_id(1)
    gates = gates_ref[...]

    @pl.when(e == 0)
    def _():
        acc_ref[...] = jnp.zeros_like(acc_ref)

    lane = lax.broadcasted_iota(jnp.int32, gates.shape, 1)
    ge = jnp.sum(jnp.where(lane == e, gates, 0.0), axis=-1, keepdims=True)
    gu = jnp.dot(h_ref[...], wgu_ref[0], preferred_element_type=F32) + bgu_ref[0]
    glu = jnp.minimum(gu[:, :ff], SWIGLU_LIMIT)
    lin = jnp.clip(gu[:, ff:], -SWIGLU_LIMIT, SWIGLU_LIMIT)
    act = glu * (1.0 / (1.0 + jnp.exp(-SWIGLU_ALPHA * glu))) * (lin + 1.0)
    act = (act * ge).astype(BF16)
    d = acc_ref.shape[1]
    cn = 1024 if d % 1024 == 0 else d
    for n0 in range(0, d, cn):
        acc_ref[:, n0:n0 + cn] += (jnp.dot(act, wd_ref[0, :, n0:n0 + cn], preferred_element_type=F32)
                                   + ge * bd_ref[0, :, n0:n0 + cn])

    @pl.when(e == pl.num_programs(1) - 1)
    def _():
        o_ref[...] = acc_ref[...].astype(o_ref.dtype)


def moe_dense(h, gates, w_gu, b_gu, w_down, b_down):
    st, d = h.shape
    n_exp, _, ff2 = w_gu.shape
    ff = ff2 // 2
    tm = 416 if st % 416 == 0 else ROW_TILE
    return pl.pallas_call(
        functools.partial(_moe_kernel, ff=ff),
        grid=(st // tm, n_exp),
        in_specs=[
            pl.BlockSpec((tm, d), lambda i, e: (i, 0)),
            pl.BlockSpec((tm, LANES), lambda i, e: (i, 0)),
            pl.BlockSpec((1, d, ff2), lambda i, e: (e, 0, 0)),
            pl.BlockSpec((1, 1, ff2), lambda i, e: (e, 0, 0)),
            pl.BlockSpec((1, ff, d), lambda i, e: (e, 0, 0)),
            pl.BlockSpec((1, 1, d), lambda i, e: (e, 0, 0)),
        ],
        out_specs=pl.BlockSpec((tm, d), lambda i, e: (i, 0)),
        out_shape=jax.ShapeDtypeStruct((st, d), BF16),
        scratch_shapes=[pltpu.VMEM((tm, d), F32)],
        compiler_params=_cparams("arbitrary", "arbitrary"),
        name="moe_dense",
    )(h, gates, w_gu, b_gu.reshape(n_exp, 1, ff2), w_down, b_down.reshape(n_exp, 1, d))


def _rope_tables(n_ctx, n_lat, dim):
    quarter = dim // 4
    inv_freq = ROPE_THETA ** (-jnp.arange(quarter, dtype=F32) / quarter)
    t = jnp.arange(n_lat)
    row = (t // GRID_W).astype(F32)
    col = (t % GRID_W).astype(F32)
    ang = jnp.stack([row[:, None] * inv_freq, col[:, None] * inv_freq], axis=1)
    ang = jnp.broadcast_to(ang[:, :, None, :], (n_lat, 2, 2, quarter)).reshape(n_lat, dim)
    cos, sin = jnp.cos(ang), jnp.sin(ang)
    first_half = (jnp.arange(dim) % (2 * quarter)) < quarter
    sin_lo = jnp.where(first_half, -sin, 0.0)
    sin_hi = jnp.where(first_half, 0.0, sin)

    def full(tab, fill):
        tab = jnp.pad(tab, ((0, 0), (0, LANES - dim)), constant_values=fill)
        return jnp.concatenate([jnp.full((n_ctx, LANES), fill, F32), tab], axis=0)

    return full(cos, 1.0), full(sin_lo, 0.0), full(sin_hi, 0.0)


def _pad_cols(w, n):
    return jnp.pad(w, ((0, 0), (0, n - w.shape[1])))


def _streams(v_ctx, v_lat):
    return jnp.stack([v_ctx, v_lat], axis=0)[:, None, :]


def _even_mixer(h, rope128, w_in, w_out, a_qn, a_kn, a_sink, b_qn, b_kn, b_lam, b_subln, lam_init, d):
    a_heads = d // (2 * HEAD_DIM)
    a_kv = a_heads // 4
    b_heads = d // (4 * HEAD_DIM)
    b_kv = b_heads // 2
    ga, gb = a_heads // a_kv, b_heads // b_kv
    scale = HEAD_DIM ** -0.5 * LOG2E
    n_qa, n_qb, n_ka, n_va, n_kb, n_vb = a_heads, 2 * b_heads, a_kv, a_kv, 2 * b_kv, 2 * b_kv
    c_qb = n_qa
    c_ka = c_qb + n_qb
    c_va = c_ka + n_ka
    c_kb = c_va + n_va
    c_vb = c_kb + n_kb
    p = matmul([h], [w_in.astype(BF16)], tn=512)
    gains = jnp.stack([a_qn, b_qn, a_kn, b_kn], axis=0).astype(F32)
    pn = prep_heads(p, rope128, gains,
                    segs=((0, c_qb, 0, scale), (c_qb, c_ka, 1, scale), (c_ka, c_va, 2, 1.0), (c_kb, c_vb, 3, 1.0)),
                    copies=((c_va, c_kb), (c_vb, c_vb + n_vb)))
    sink = jnp.broadcast_to((a_sink.astype(F32) * LOG2E).reshape(a_kv, ga, 1), (a_kv, ga, LANES))
    ya = window_attention(pn, sink, a_kv, ga, 0, c_ka, c_va)
    lf = b_lam.astype(F32)
    lam = jnp.exp(jnp.sum(lf[0] * lf[1])) - jnp.exp(jnp.sum(lf[2] * lf[3])) + lam_init
    dvb = 2 * HEAD_DIM
    aux = jnp.stack([jnp.full((dvb,), lam, F32), b_subln.astype(F32) * (1.0 - lam_init)], axis=0)
    yb = flash_attention(pn, pn, pn, aux, n_kv=b_kv, g=gb, comps=2, dq=HEAD_DIM, dv=dvb,
                         c_q=c_qb * HEAD_DIM // (gb * 2 * HEAD_DIM), c_k=c_kb // 2, c_v=lambda hh: c_vb // 2 + hh,
                         tk=512, diff=True)
    w_out = w_out.astype(BF16)
    na = a_heads * HEAD_DIM
    return matmul([ya, yb], [w_out[:na], w_out[na:]], tn=512)


def _odd_mixer(h, rope128, rope64, w_in, w_out, c_qn, c_kn, d_qa_norm, d_kva_norm, d_wq_up, d_wkv_up, d_qn, d_kn, d):
    st = h.shape[0]
    c_heads = d // (2 * HEAD_DIM)
    c_kv = c_heads // 4
    gc = c_heads // c_kv
    d_heads = d // (2 * HEAD_DIM)
    q_rank = d_wq_up.shape[0]
    kv_rank = d_wkv_up.shape[0]
    n_qc, n_qa, n_kc, n_vc, n_kva = c_heads, q_rank // LANES, c_kv, c_kv, kv_rank // LANES
    c_qa = n_qc
    c_kc = c_qa + n_qa
    c_vc = c_kc + n_kc
    c_kva = c_vc + n_vc
    c_kr = c_kva + n_kva
    n_in = (c_kr + 1) * LANES
    tn = 768
    n_pad = -(-n_in // tn) * tn
    p = matmul([h], [_pad_cols(w_in, n_pad).astype(BF16)], tn=tn)
    tm = ROW_TILE
    tab = pl.BlockSpec((tm, LANES), lambda i: (i, 0))
    gains = jnp.stack([c_qn, c_kn], axis=0).astype(F32)
    qk, qa, kva = pl.pallas_call(
        functools.partial(_prep_odd1_kernel, n_q=n_qc, n_k=n_kc, c_k=c_kc, c_qa=c_qa, n_qa=n_qa, c_kva=c_kva,
                          n_kva=n_kva, q_scale=HEAD_DIM ** -0.5 * LOG2E),
        grid=(st // tm,),
        in_specs=[pl.BlockSpec((tm, n_pad), lambda i: (i, 0)), tab, tab, tab,
                  pl.BlockSpec((2, LANES), lambda i: (0, 0)),
                  pl.BlockSpec((1, q_rank), lambda i: (0, 0)),
                  pl.BlockSpec((1, kv_rank), lambda i: (0, 0))],
        out_specs=[pl.BlockSpec((tm, (n_qc + n_kc) * LANES), lambda i: (i, 0)),
                   pl.BlockSpec((tm, q_rank), lambda i: (i, 0)),
                   pl.BlockSpec((tm, kv_rank), lambda i: (i, 0))],
        out_shape=[jax.ShapeDtypeStruct((st, (n_qc + n_kc) * LANES), BF16),
                   jax.ShapeDtypeStruct((st, q_rank), BF16),
                   jax.ShapeDtypeStruct((st, kv_rank), BF16)],
        compiler_params=_cparams("arbitrary"),
        name="prep_odd1",
    )(p, *rope128, gains, d_qa_norm.astype(F32)[None, :], d_kva_norm.astype(F32)[None, :])
    wq = jnp.pad(d_wq_up.reshape(q_rank, d_heads, D_QK), ((0, 0), (0, 0), (0, D_PAD - D_QK)))
    qd_raw = matmul([qa], [wq.reshape(q_rank, d_heads * D_PAD).astype(BF16)], tn=512)
    kv = matmul([kva], [d_wkv_up.astype(BF16)], tn=512)
    pad_gain = lambda gvec: jnp.pad(gvec.astype(F32), (0, D_PAD - D_QK))[None, :]
    qd, kd = pl.pallas_call(
        functools.partial(_prep_odd2_kernel, n_heads=d_heads, q_scale=D_QK ** -0.5 * LOG2E),
        grid=(st // tm,),
        in_specs=[pl.BlockSpec((tm, d_heads * D_PAD), lambda i: (i, 0)),
                  pl.BlockSpec((tm, d_heads * D_PAD), lambda i: (i, 0)),
                  pl.BlockSpec((tm, LANES), lambda i: (i, c_kr)),
                  tab, tab, tab,
                  pl.BlockSpec((1, D_PAD), lambda i: (0, 0)),
                  pl.BlockSpec((1, D_PAD), lambda i: (0, 0))],
        out_specs=[pl.BlockSpec((tm, d_heads * D_PAD), lambda i: (i, 0)),
                   pl.BlockSpec((tm, d_heads * D_PAD), lambda i: (i, 0))],
        out_shape=[jax.ShapeDtypeStruct((st, d_heads * D_PAD), BF16),
                   jax.ShapeDtypeStruct((st, d_heads * D_PAD), BF16)],
        compiler_params=_cparams("arbitrary"),
        name="prep_odd2",
    )(qd_raw, kv, p, *rope64, pad_gain(d_qn), pad_gain(d_kn))
    aux = jnp.zeros((8, LANES), F32)
    yc = flash_attention(qk, qk, p, aux, n_kv=c_kv, g=gc, comps=1, dq=HEAD_DIM, dv=HEAD_DIM,
                         c_q=0, c_k=n_qc, c_v=lambda hh: c_vc + hh, tk=512)
    yd = flash_attention(qd, kd, kv, aux, n_kv=d_heads, g=1, comps=1, dq=D_PAD, dv=D_V,
                         c_q=0, c_k=0, c_v=lambda hh: 2 * hh + 1, tk=512)
    w_out = w_out.astype(BF16)
    nc = c_heads * HEAD_DIM
    return matmul([yc, yd], [w_out[:nc], w_out[nc:]], tn=512)


def kernel(x, c, ctx, c_ctx, adaln_down, adaln_up, adaln_b, norm_mix, norm_ffn, ev_w_in, ev_w_out, ev_a_qn, ev_a_kn,
           ev_a_sink, ev_b_qn, ev_b_kn, ev_b_lam, ev_b_subln, od_w_in, od_w_out, od_c_qn, od_c_kn, od_d_qa_norm,
           od_d_kva_norm, od_d_wq_up, od_d_wkv_up, od_d_qn, od_d_kn, router_w, router_b, moe_w_gu, moe_b_gu,
           moe_w_down, moe_b_down):
    bsz, seq, d = x.shape
    n_ctx = ctx.shape[1]
    depth = adaln_down.shape[0]
    n_exp = router_w.shape[2]
    assert bsz == 1 and n_ctx == ROW_TILE and seq % ROW_TILE == 0 and n_exp <= LANES

    rope128 = _rope_tables(n_ctx, seq, HEAD_DIM)
    rope64 = _rope_tables(n_ctx, seq, D_ROPE)
    cvecs = jnp.zeros((8, d), F32).at[0].set(c[0]).at[1].set(c_ctx)
    mods = adaln_all(cvecs, adaln_down, adaln_up, adaln_b)

    xs = jnp.concatenate([ctx[0], x[0]], axis=0)
    y_prev, gate_prev = None, None
    for l in range(depth):
        m_lat = mods[l, 0].reshape(N_MOD, d)
        m_ctx = mods[l, 1].reshape(N_MOD, d)
        mult = _streams(norm_mix[l] * (1.0 + m_ctx[1]), norm_mix[l] * (1.0 + m_lat[1]))
        shift = _streams(m_ctx[0], m_lat[0])
        if y_prev is None:
            (h,) = modulate(xs, mult, shift)
        else:
            xs, h = modulate(xs, mult, shift, res=(y_prev, gate_prev))
        j = l // 2
        if l % 2 == 0:
            lam_init = 0.8 - 0.6 * math.exp(-0.3 * l)
            y = _even_mixer(h, rope128, ev_w_in[j], ev_w_out[j], ev_a_qn[j], ev_a_kn[j], ev_a_sink[j], ev_b_qn[j],
                            ev_b_kn[j], ev_b_lam[j], ev_b_subln[j], lam_init, d)
        else:
            y = _odd_mixer(h, rope128, rope64, od_w_in[j], od_w_out[j], od_c_qn[j], od_c_kn[j], od_d_qa_norm[j],
                           od_d_kva_norm[j], od_d_wq_up[j], od_d_wkv_up[j], od_d_qn[j], od_d_kn[j], d)
        mult = _streams(norm_ffn[l] * (1.0 + m_ctx[4]), norm_ffn[l] * (1.0 + m_lat[4]))
        shift = _streams(m_ctx[3], m_lat[3])
        rw = _pad_cols(router_w[l].astype(F32), LANES)
        rb = jnp.pad(router_b[l].astype(F32), (0, LANES - n_exp))[None, :]
        xs, h2, gates = modulate(xs, mult, shift, res=(y, _streams(m_ctx[2], m_lat[2])), router=(rw, rb, n_exp))
        y_prev = moe_dense(h2, gates, moe_w_gu[l].astype(BF16), moe_b_gu[l].astype(F32),
                           moe_w_down[l].astype(BF16), moe_b_down[l].astype(F32))
        gate_prev = _streams(m_ctx[5], m_lat[5])
    out = final_residual(xs, y_prev, gate_prev)
    return out[None]
```

```python
import functools
import math

import jax
import jax.numpy as jnp
from jax import lax
from jax.experimental import pallas as pl
from jax.experimental.pallas import tpu as pltpu

F32 = jnp.float32
BF16 = jnp.bfloat16

GRID_W = 64
HEAD_DIM = 128
ROPE_THETA = 10000.0
EPS = 1e-6
N_MOD = 6
D_NOPE = 128
D_ROPE = 64
D_QK = D_NOPE + D_ROPE
D_V = 128
D_PAD = 256
TOP_K = 4
SWIGLU_ALPHA = 1.702
SWIGLU_LIMIT = 7.0
LOG2E = 1.4426950408889634
NEG_BIG = -1e30
SCORE_BOUND = 60.0
BF16_NORM_SLACK = 1.02

LANES = 128
ROW_TILE = 256
VMEM_LIMIT = 56 * 1024 * 1024


def _cparams(*sem):
    return pltpu.CompilerParams(dimension_semantics=sem, vmem_limit_bytes=VMEM_LIMIT)


def _adaln_kernel(c_ref, down_ref, up_ref, b_ref, o_ref, t_ref):
    @pl.when(pl.program_id(1) == 0)
    def _():
        c = c_ref[...]
        a = c * (1.0 / (1.0 + jnp.exp(-c)))
        t_ref[...] = jnp.dot(a, down_ref[0], preferred_element_type=F32, precision=lax.Precision.HIGHEST)

    o_ref[0] = jnp.dot(t_ref[...], up_ref[0], preferred_element_type=F32,
                       precision=lax.Precision.HIGHEST) + b_ref[0]


def adaln_all(cvecs, down, up, bias):
    depth, d, rank = down.shape
    n = up.shape[2]
    tn = 2048
    assert n % tn == 0
    return pl.pallas_call(
        _adaln_kernel,
        grid=(depth, n // tn),
        in_specs=[
            pl.BlockSpec((8, d), lambda l, j: (0, 0)),
            pl.BlockSpec((1, d, rank), lambda l, j: (l, 0, 0)),
            pl.BlockSpec((1, rank, tn), lambda l, j: (l, 0, j)),
            pl.BlockSpec((1, 1, tn), lambda l, j: (l, 0, j)),
        ],
        out_specs=pl.BlockSpec((1, 8, tn), lambda l, j: (l, 0, j)),
        out_shape=jax.ShapeDtypeStruct((depth, 8, n), F32),
        scratch_shapes=[pltpu.VMEM((8, rank), F32)],
        compiler_params=_cparams("arbitrary", "arbitrary"),
        name="adaln",
    )(cvecs, down, up, bias.reshape(depth, 1, n))


def _topk_gates(logits, n_exp):
    lane = lax.broadcasted_iota(jnp.int32, logits.shape, 1).astype(F32)
    work = jnp.where(lane < n_exp, logits, -jnp.inf)
    gates = jnp.zeros_like(logits)
    denom = None
    v0 = None
    for k in range(TOP_K):
        m = jnp.max(work, axis=-1, keepdims=True)
        idx = jnp.min(jnp.where(work == m, lane, float(LANES)), axis=-1, keepdims=True)
        sel = lane == idx
        if k == 0:
            v0 = m
            e = jnp.ones_like(m)
            denom = e
        else:
            e = jnp.exp(m - v0)
            denom = denom + e
        gates = jnp.where(sel, e, gates)
        work = jnp.where(sel, -jnp.inf, work)
    return gates / denom


def _modulate_kernel(*refs, has_res, n_exp, ctx_rows):
    it = iter(refs)
    x_ref = next(it)
    y_ref = next(it) if has_res else None
    gate_ref = next(it) if has_res else None
    mult_ref = next(it)
    shift_ref = next(it)
    rw_ref = next(it) if n_exp else None
    rb_ref = next(it) if n_exp else None
    xo_ref = next(it) if has_res else None
    h_ref = next(it)
    g_ref = next(it) if n_exp else None

    x = x_ref[...]
    if has_res:
        x = x + gate_ref[0] * y_ref[...].astype(F32)
        xo_ref[...] = x
    ms = jnp.mean(x * x, axis=-1, keepdims=True)
    h = x * lax.rsqrt(ms + EPS) * mult_ref[0] + shift_ref[0]
    h_ref[...] = h.astype(BF16)
    if n_exp:
        logits = jnp.dot(h, rw_ref[...], preferred_element_type=F32,
                         precision=lax.Precision.HIGHEST) + rb_ref[...]
        g_ref[...] = _topk_gates(logits, n_exp)


def modulate(x, mult, shift, res=None, router=None):
    st, d = x.shape
    tm = ROW_TILE
    stream = lambda i: (jnp.minimum(i, 1), 0, 0)
    row = lambda i: (i, 0)
    vec_spec = pl.BlockSpec((1, 1, d), stream)
    args, in_specs = [x], [pl.BlockSpec((tm, d), row)]
    if res is not None:
        args += [res[0], res[1]]
        in_specs += [pl.BlockSpec((tm, d), row), vec_spec]
    args += [mult, shift]
    in_specs += [vec_spec, vec_spec]
    n_exp = 0
    if router is not None:
        args += [router[0], router[1]]
        n_exp = router[2]
        in_specs += [pl.BlockSpec((d, LANES), lambda i: (0, 0)), pl.BlockSpec((1, LANES), lambda i: (0, 0))]
    out_shape, out_specs = [], []
    if res is not None:
        out_shape.append(jax.ShapeDtypeStruct((st, d), F32))
        out_specs.append(pl.BlockSpec((tm, d), row))
    out_shape.append(jax.ShapeDtypeStruct((st, d), BF16))
    out_specs.append(pl.BlockSpec((tm, d), row))
    if router is not None:
        out_shape.append(jax.ShapeDtypeStruct((st, LANES), F32))
        out_specs.append(pl.BlockSpec((tm, LANES), row))
    return pl.pallas_call(
        functools.partial(_modulate_kernel, has_res=res is not None, n_exp=n_exp, ctx_rows=ROW_TILE),
        grid=(st // tm,),
        in_specs=in_specs,
        out_specs=out_specs,
        out_shape=out_shape,
        compiler_params=_cparams("arbitrary"),
        name="modulate",
    )(*args)


def _final_residual_kernel(x_ref, y_ref, gate_ref, o_ref):
    o_ref[...] = x_ref[...] + gate_ref[0] * y_ref[...].astype(F32)


def final_residual(x, y, gate):
    st, d = x.shape
    tm = ROW_TILE
    n = st // tm - 1
    return pl.pallas_call(
        _final_residual_kernel,
        grid=(n,),
        in_specs=[
            pl.BlockSpec((tm, d), lambda i: (i + 1, 0)),
            pl.BlockSpec((tm, d), lambda i: (i + 1, 0)),
            pl.BlockSpec((1, 1, d), lambda i: (1, 0, 0)),
        ],
        out_specs=pl.BlockSpec((tm, d), lambda i: (i, 0)),
        out_shape=jax.ShapeDtypeStruct((n * tm, d), F32),
        compiler_params=_cparams("arbitrary"),
        name="final_residual",
    )(x, y, gate)


def _mm_kernel(*refs, n_pairs):
    o_ref = refs[-1]
    acc = None
    for p in range(n_pairs):
        t = jnp.dot(refs[p][...], refs[n_pairs + p][...], preferred_element_type=F32)
        acc = t if acc is None else acc + t
    o_ref[...] = acc.astype(o_ref.dtype)


def _row_block(m):
    for tm in (1280, 1024, 512, 256):
        if m % tm == 0:
            return tm
    raise ValueError(m)


def matmul(a_list, b_list, tn, out_dtype=BF16):
    m = a_list[0].shape[0]
    n = b_list[0].shape[1]
    tm = _row_block(m)
    while n % tn:
        tn -= LANES
    in_specs = [pl.BlockSpec((tm, a.shape[1]), lambda i, j: (i, 0)) for a in a_list]
    in_specs += [pl.BlockSpec((b.shape[0], tn), lambda i, j: (0, j)) for b in b_list]
    return pl.pallas_call(
        functools.partial(_mm_kernel, n_pairs=len(a_list)),
        grid=(m // tm, n // tn),
        in_specs=in_specs,
        out_specs=pl.BlockSpec((tm, tn), lambda i, j: (i, j)),
        out_shape=jax.ShapeDtypeStruct((m, n), out_dtype),
        compiler_params=_cparams("arbitrary", "arbitrary"),
        name="matmul",
    )(*a_list, *b_list)


def _rope(y, cos, sin_lo, sin_hi, quarter):
    left = pltpu.roll(y, LANES - quarter, 1)
    right = pltpu.roll(y, quarter, 1)
    return y * cos + left * sin_lo + right * sin_hi


def _head_norm(x, gain, n_valid):
    ss = jnp.sum(x * x, axis=-1, keepdims=True)
    return x * lax.rsqrt(ss * (1.0 / n_valid) + EPS) * gain


def _prep_even_kernel(p_ref, cos_ref, slo_ref, shi_ref, gains_ref, o_ref, *, segs, n_copy_from):
    cos, slo, shi = cos_ref[...], slo_ref[...], shi_ref[...]
    for (c0, c1, gi, scale) in segs:
        gain = gains_ref[gi:gi + 1, :]
        for c in range(c0, c1):
            x = p_ref[:, c * LANES:(c + 1) * LANES].astype(F32)
            y = _rope(_head_norm(x, gain, HEAD_DIM), cos, slo, shi, HEAD_DIM // 4)
            if scale != 1.0:
                y = y * scale
            o_ref[:, c * LANES:(c + 1) * LANES] = y.astype(BF16)
    for (c0, c1) in n_copy_from:
        o_ref[:, c0 * LANES:c1 * LANES] = p_ref[:, c0 * LANES:c1 * LANES]


def prep_heads(p, rope128, gains, segs, copies):
    st, n = p.shape
    tm = ROW_TILE
    tab = pl.BlockSpec((tm, LANES), lambda i: (i, 0))
    return pl.pallas_call(
        functools.partial(_prep_even_kernel, segs=segs, n_copy_from=copies),
        grid=(st // tm,),
        in_specs=[pl.BlockSpec((tm, n), lambda i: (i, 0)), tab, tab, tab,
                  pl.BlockSpec(gains.shape, lambda i: (0, 0))],
        out_specs=pl.BlockSpec((tm, n), lambda i: (i, 0)),
        out_shape=jax.ShapeDtypeStruct((st, n), BF16),
        compiler_params=_cparams("arbitrary"),
        name="prep_heads",
    )(p, *rope128, gains)


def _prep_odd1_kernel(p_ref, cos_ref, slo_ref, shi_ref, gains_ref, gqa_ref, gkva_ref,
                      qk_ref, qa_ref, kva_ref, *, n_q, n_k, c_k, c_qa, n_qa, c_kva, n_kva, q_scale):
    cos, slo, shi = cos_ref[...], slo_ref[...], shi_ref[...]
    for c in range(n_q):
        x = p_ref[:, c * LANES:(c + 1) * LANES].astype(F32)
        y = _rope(_head_norm(x, gains_ref[0:1, :], HEAD_DIM), cos, slo, shi, HEAD_DIM // 4) * q_scale
        qk_ref[:, c * LANES:(c + 1) * LANES] = y.astype(BF16)
    for c in range(n_k):
        x = p_ref[:, (c_k + c) * LANES:(c_k + c + 1) * LANES].astype(F32)
        y = _rope(_head_norm(x, gains_ref[1:2, :], HEAD_DIM), cos, slo, shi, HEAD_DIM // 4)
        qk_ref[:, (n_q + c) * LANES:(n_q + c + 1) * LANES] = y.astype(BF16)
    xa = p_ref[:, c_qa * LANES:(c_qa + n_qa) * LANES].astype(F32)
    qa_ref[...] = _head_norm(xa, gqa_ref[...], n_qa * LANES).astype(BF16)
    xk = p_ref[:, c_kva * LANES:(c_kva + n_kva) * LANES].astype(F32)
    kva_ref[...] = _head_norm(xk, gkva_ref[...], n_kva * LANES).astype(BF16)


def _prep_odd2_kernel(qd_ref, kv_ref, kr_ref, cos_ref, slo_ref, shi_ref, gq_ref, gk_ref,
                      qo_ref, ko_ref, *, n_heads, q_scale):
    cos, slo, shi = cos_ref[...], slo_ref[...], shi_ref[...]
    gq_n, gq_r = gq_ref[:, :LANES], gq_ref[:, LANES:]
    gk_n, gk_r = gk_ref[:, :LANES], gk_ref[:, LANES:]
    kr = kr_ref[...].astype(F32)
    kr_ss = jnp.sum(kr * kr, axis=-1, keepdims=True)
    for h in range(n_heads):
        b = h * D_PAD
        qn = qd_ref[:, b:b + LANES].astype(F32)
        qr = qd_ref[:, b + LANES:b + D_PAD].astype(F32)
        ss = jnp.sum(qn * qn, axis=-1, keepdims=True) + jnp.sum(qr * qr, axis=-1, keepdims=True)
        r = lax.rsqrt(ss * (1.0 / D_QK) + EPS) * q_scale
        qo_ref[:, b:b + LANES] = (qn * r * gq_n).astype(BF16)
        qo_ref[:, b + LANES:b + D_PAD] = _rope(qr * r * gq_r, cos, slo, shi, D_ROPE // 4).astype(BF16)
        kn = kv_ref[:, b:b + LANES].astype(F32)
        ss = jnp.sum(kn * kn, axis=-1, keepdims=True) + kr_ss
        r = lax.rsqrt(ss * (1.0 / D_QK) + EPS)
        ko_ref[:, b:b + LANES] = (kn * r * gk_n).astype(BF16)
        ko_ref[:, b + LANES:b + D_PAD] = _rope(kr * r * gk_r, cos, slo, shi, D_ROPE // 4).astype(BF16)


def _stack_heads(q_ref, g, d):
    return jnp.concatenate([q_ref[:, i * d:(i + 1) * d] for i in range(g)], axis=0)


def _qk(q, k):
    return lax.dot_general(q, k, (((1,), (1,)), ((), ())), preferred_element_type=F32)


def _window_attn_kernel(q_ref, kp_ref, km_ref, kn_ref, kc_ref, vp_ref, vm_ref, vn_ref, vc_ref, sink_ref,
                        o_ref, *, g, st):
    i = pl.program_id(1)
    tq = q_ref.shape[0]
    half = tq // 2
    qs = _stack_heads(q_ref, g, HEAD_DIM)
    k_loc = jnp.concatenate([kp_ref[...], km_ref[...], kn_ref[...]], axis=0)
    v_loc = jnp.concatenate([vp_ref[...], vm_ref[...], vn_ref[...]], axis=0)
    s_loc = _qk(qs, k_loc)
    s_ctx = _qk(qs, kc_ref[...])
    a = lax.broadcasted_iota(jnp.int32, s_loc.shape, 0) & (tq - 1)
    j = lax.broadcasted_iota(jnp.int32, s_loc.shape, 1)
    diff = j - half - a
    krow = i * tq - half + j
    valid = (jnp.abs(diff) <= half) & (krow >= tq) & (krow < st) & (i >= 1)
    s_loc = jnp.where(valid, s_loc, NEG_BIG)
    sink = jnp.concatenate([jnp.broadcast_to(sink_ref[0, h:h + 1, 0:1], (tq, 1)) for h in range(g)], axis=0)
    m = jnp.maximum(jnp.maximum(jnp.max(s_loc, axis=-1, keepdims=True),
                                jnp.max(s_ctx, axis=-1, keepdims=True)), sink)
    p_loc = jnp.exp2(s_loc - m)
    p_ctx = jnp.exp2(s_ctx - m)
    l = jnp.sum(p_loc, axis=-1, keepdims=True) + jnp.sum(p_ctx, axis=-1, keepdims=True) + jnp.exp2(sink - m)
    o = (jnp.dot(p_loc.astype(BF16), v_loc, preferred_element_type=F32)
         + jnp.dot(p_ctx.astype(BF16), vc_ref[...], preferred_element_type=F32)) / l
    for h in range(g):
        o_ref[:, h * HEAD_DIM:(h + 1) * HEAD_DIM] = o[h * tq:(h + 1) * tq].astype(BF16)


def window_attention(pn, sink, n_kv, g, c_q, c_k, c_v):
    st = pn.shape[0]
    tq = ROW_TILE
    half = tq // 2
    nh = st // half
    q_spec = pl.BlockSpec((tq, g * HEAD_DIM), lambda h, i: (i, c_q // g + h))

    def kv_specs(c0):
        return [
            pl.BlockSpec((half, HEAD_DIM), lambda h, i: (jnp.maximum(2 * i - 1, 0), c0 + h)),
            pl.BlockSpec((tq, HEAD_DIM), lambda h, i: (i, c0 + h)),
            pl.BlockSpec((half, HEAD_DIM), lambda h, i: (jnp.minimum(2 * i + 2, nh - 1), c0 + h)),
            pl.BlockSpec((tq, HEAD_DIM), lambda h, i: (0, c0 + h)),
        ]

    return pl.pallas_call(
        functools.partial(_window_attn_kernel, g=g, st=st),
        grid=(n_kv, st // tq),
        in_specs=[q_spec] + kv_specs(c_k) + kv_specs(c_v)
        + [pl.BlockSpec((1, g, LANES), lambda h, i: (h, 0, 0))],
        out_specs=pl.BlockSpec((tq, g * HEAD_DIM), lambda h, i: (i, h)),
        out_shape=jax.ShapeDtypeStruct((st, n_kv * g * HEAD_DIM), BF16),
        compiler_params=_cparams("arbitrary", "arbitrary"),
        name="window_attention",
    )(pn, pn, pn, pn, pn, pn, pn, pn, pn, sink)


def _flash_kernel(flag_ref, q_ref, k_ref, v_ref, aux_ref, o_ref, acc_ref, l_ref, *, g, comps, dq, dv, tk, diff):
    qi = pl.program_id(1)
    tq = q_ref.shape[0]
    n_keys = k_ref.shape[0]
    n_lat = (n_keys - tq) // tk
    n_iter = jnp.where(qi == 0, 0, n_lat)
    qs = [jnp.concatenate([q_ref[:, (gi * comps + c) * dq:(gi * comps + c + 1) * dq] for gi in range(g)], axis=0)
          for c in range(comps)]

    def scores(start, size):
        return jnp.concatenate(
            [_qk(qs[c], k_ref[pl.ds(start, size), c * dq:(c + 1) * dq]) for c in range(comps)], axis=0)

    def chunk_start(j):
        return pl.multiple_of(tq + j * tk, math.gcd(tq, tk))

    @pl.when(flag_ref[0] == 1)
    def _bounded():
        def step(start, size, first):
            p = jnp.exp2(scores(start, size))
            psum = p[:, :LANES]
            for b in range(1, size // LANES):
                psum = psum + p[:, b * LANES:(b + 1) * LANES]
            pv = jnp.dot(p.astype(BF16), v_ref[pl.ds(start, size), :], preferred_element_type=F32)
            if first:
                acc_ref[...] = pv
                l_ref[...] = psum
            else:
                acc_ref[...] += pv
                l_ref[...] += psum

        step(0, tq, True)

        def body(j, carry):
            step(chunk_start(j), tk, False)
            return carry

        lax.fori_loop(0, n_iter, body, 0)
        l_ref[...] = jnp.broadcast_to(jnp.sum(l_ref[...], axis=-1, keepdims=True), l_ref.shape)

    @pl.when(flag_ref[0] != 1)
    def _online():
        def step(start, size, m, l, first):
            s = scores(start, size)
            m_cur = jnp.max(s, axis=-1, keepdims=True)
            m_new = m_cur if first else jnp.maximum(m, m_cur)
            p = jnp.exp2(s - m_new)
            pv = jnp.dot(p.astype(BF16), v_ref[pl.ds(start, size), :], preferred_element_type=F32)
            if first:
                l_new = jnp.sum(p, axis=-1, keepdims=True)
                acc_ref[...] = pv
            else:
                alpha = jnp.exp2(m - m_new)
                l_new = alpha * l + jnp.sum(p, axis=-1, keepdims=True)
                acc_ref[...] = alpha * acc_ref[...] + pv
            return m_new, l_new

        m, l = step(0, tq, None, None, True)
        m, l = lax.fori_loop(0, n_iter, lambda j, c: step(chunk_start(j), tk, c[0], c[1], False), (m, l))
        l_ref[...] = jnp.broadcast_to(l, l_ref.shape)

    o = acc_ref[...] / l_ref[:, 0:1]
    if diff:
        lam = aux_ref[0:1, 0:1]
        for gi in range(g):
            y = o[gi * tq:(gi + 1) * tq] - lam * o[(g + gi) * tq:(g + gi + 1) * tq]
            ss = jnp.mean(y * y, axis=-1, keepdims=True)
            y = y * lax.rsqrt(ss + EPS) * aux_ref[1:2, :]
            o_ref[:, gi * dv:(gi + 1) * dv] = y.astype(BF16)
    else:
        for gi in range(g):
            o_ref[:, gi * dv:(gi + 1) * dv] = o[gi * tq:(gi + 1) * tq].astype(BF16)


def _score_bound_flag(q_gain, k_gain, n_norm, scale):
    bound = n_norm * jnp.max(jnp.abs(q_gain)) * jnp.max(jnp.abs(k_gain)) * scale * BF16_NORM_SLACK
    return (bound <= SCORE_BOUND).astype(jnp.int32).reshape(1)


def flash_attention(q_arr, k_arr, v_arr, aux, flag, *, n_kv, g, comps, dq, dv, c_q, c_k, c_v, tk, diff=False):
    st = q_arr.shape[0]
    tq = ROW_TILE
    while (st - tq) % tk:
        tk //= 2
    rows = g * comps * tq
    return pl.pallas_call(
        functools.partial(_flash_kernel, g=g, comps=comps, dq=dq, dv=dv, tk=tk, diff=diff),
        grid=(n_kv, st // tq),
        in_specs=[
            pl.BlockSpec(memory_space=pltpu.SMEM),
            pl.BlockSpec((tq, g * comps * dq), lambda h, i: (i, c_q + h)),
            pl.BlockSpec((st, comps * dq), lambda h, i: (0, c_k + h)),
            pl.BlockSpec((st, dv), lambda h, i: (0, c_v(h))),
            pl.BlockSpec(aux.shape, lambda h, i: (0, 0)),
        ],
        out_specs=pl.BlockSpec((tq, g * dv), lambda h, i: (i, h)),
        out_shape=jax.ShapeDtypeStruct((st, n_kv * g * dv), BF16),
        scratch_shapes=[pltpu.VMEM((rows, dv), F32), pltpu.VMEM((rows, LANES), F32)],
        compiler_params=_cparams("arbitrary", "arbitrary"),
        name="flash_attention",
    )(flag, q_arr, k_arr, v_arr, aux)


def _moe_kernel(h_ref, gates_ref, wgu_ref, bgu_ref, wd_ref, bd_ref, o_ref, acc_ref, *, ff):
    e = pl.program_id(1)
    gates = gates_ref[...]

    @pl.when(e == 0)
    def _():
        acc_ref[...] = jnp.zeros_like(acc_ref)

    lane = lax.broadcasted_iota(jnp.int32, gates.shape, 1)
    ge = jnp.sum(jnp.where(lane == e, gates, 0.0), axis=-1, keepdims=True)
    gu = jnp.dot(h_ref[...], wgu_ref[0], preferred_element_type=F32) + bgu_ref[0]
    glu = jnp.minimum(gu[:, :ff], SWIGLU_LIMIT)
    lin = jnp.clip(gu[:, ff:], -SWIGLU_LIMIT, SWIGLU_LIMIT)
    act = glu * (1.0 / (1.0 + jnp.exp(-SWIGLU_ALPHA * glu))) * (lin + 1.0)
    act = (act * ge).astype(BF16)
    d = acc_ref.shape[1]
    cn = 1024 if d % 1024 == 0 else d
    for n0 in range(0, d, cn):
        acc_ref[:, n0:n0 + cn] += (jnp.dot(act, wd_ref[0, :, n0:n0 + cn], preferred_element_type=F32)
                                   + ge * bd_ref[0, :, n0:n0 + cn])

    @pl.when(e == pl.num_programs(1) - 1)
    def _():
        o_ref[...] = acc_ref[...].astype(o_ref.dtype)


def moe_dense(h, gates, w_gu, b_gu, w_down, b_down):
    st, d = h.shape
    n_exp, _, ff2 = w_gu.shape
    ff = ff2 // 2
    tm = 416 if st % 416 == 0 else ROW_TILE
    return pl.pallas_call(
        functools.partial(_moe_kernel, ff=ff),
        grid=(st // tm, n_exp),
        in_specs=[
            pl.BlockSpec((tm, d), lambda i, e: (i, 0)),
            pl.BlockSpec((tm, LANES), lambda i, e: (i, 0)),
            pl.BlockSpec((1, d, ff2), lambda i, e: (e, 0, 0)),
            pl.BlockSpec((1, 1, ff2), lambda i, e: (e, 0, 0)),
            pl.BlockSpec((1, ff, d), lambda i, e: (e, 0, 0)),
            pl.BlockSpec((1, 1, d), lambda i, e: (e, 0, 0)),
        ],
        out_specs=pl.BlockSpec((tm, d), lambda i, e: (i, 0)),
        out_shape=jax.ShapeDtypeStruct((st, d), BF16),
        scratch_shapes=[pltpu.VMEM((tm, d), F32)],
        compiler_params=_cparams("arbitrary", "arbitrary"),
        name="moe_dense",
    )(h, gates, w_gu, b_gu.reshape(n_exp, 1, ff2), w_down, b_down.reshape(n_exp, 1, d))


def _rope_tables(n_ctx, n_lat, dim):
    quarter = dim // 4
    inv_freq = ROPE_THETA ** (-jnp.arange(quarter, dtype=F32) / quarter)
    t = jnp.arange(n_lat)
    row = (t // GRID_W).astype(F32)
    col = (t % GRID_W).astype(F32)
    ang = jnp.stack([row[:, None] * inv_freq, col[:, None] * inv_freq], axis=1)
    ang = jnp.broadcast_to(ang[:, :, None, :], (n_lat, 2, 2, quarter)).reshape(n_lat, dim)
    cos, sin = jnp.cos(ang), jnp.sin(ang)
    first_half = (jnp.arange(dim) % (2 * quarter)) < quarter
    sin_lo = jnp.where(first_half, -sin, 0.0)
    sin_hi = jnp.where(first_half, 0.0, sin)

    def full(tab, fill):
        tab = jnp.pad(tab, ((0, 0), (0, LANES - dim)), constant_values=fill)
        return jnp.concatenate([jnp.full((n_ctx, LANES), fill, F32), tab], axis=0)

    return full(cos, 1.0), full(sin_lo, 0.0), full(sin_hi, 0.0)


def _pad_cols(w, n):
    return jnp.pad(w, ((0, 0), (0, n - w.shape[1])))


def _streams(v_ctx, v_lat):
    return jnp.stack([v_ctx, v_lat], axis=0)[:, None, :]


def _even_mixer(h, rope128, w_in, w_out, a_qn, a_kn, a_sink, b_qn, b_kn, b_lam, b_subln, lam_init, d):
    a_heads = d // (2 * HEAD_DIM)
    a_kv = a_heads // 4
    b_heads = d // (4 * HEAD_DIM)
    b_kv = b_heads // 2
    ga, gb = a_heads // a_kv, b_heads // b_kv
    scale = HEAD_DIM ** -0.5 * LOG2E
    n_qa, n_qb, n_ka, n_va, n_kb, n_vb = a_heads, 2 * b_heads, a_kv, a_kv, 2 * b_kv, 2 * b_kv
    c_qb = n_qa
    c_ka = c_qb + n_qb
    c_va = c_ka + n_ka
    c_kb = c_va + n_va
    c_vb = c_kb + n_kb
    p = matmul([h], [w_in.astype(BF16)], tn=512)
    gains = jnp.stack([a_qn, b_qn, a_kn, b_kn], axis=0).astype(F32)
    pn = prep_heads(p, rope128, gains,
                    segs=((0, c_qb, 0, scale), (c_qb, c_ka, 1, scale), (c_ka, c_va, 2, 1.0), (c_kb, c_vb, 3, 1.0)),
                    copies=((c_va, c_kb), (c_vb, c_vb + n_vb)))
    sink = jnp.broadcast_to((a_sink.astype(F32) * LOG2E).reshape(a_kv, ga, 1), (a_kv, ga, LANES))
    ya = window_attention(pn, sink, a_kv, ga, 0, c_ka, c_va)
    lf = b_lam.astype(F32)
    lam = jnp.exp(jnp.sum(lf[0] * lf[1])) - jnp.exp(jnp.sum(lf[2] * lf[3])) + lam_init
    dvb = 2 * HEAD_DIM
    aux = jnp.stack([jnp.full((dvb,), lam, F32), b_subln.astype(F32) * (1.0 - lam_init)], axis=0)
    flag = _score_bound_flag(b_qn, b_kn, HEAD_DIM, scale)
    yb = flash_attention(pn, pn, pn, aux, flag, n_kv=b_kv, g=gb, comps=2, dq=HEAD_DIM, dv=dvb,
                         c_q=c_qb * HEAD_DIM // (gb * 2 * HEAD_DIM), c_k=c_kb // 2, c_v=lambda hh: c_vb // 2 + hh,
                         tk=1024, diff=True)
    w_out = w_out.astype(BF16)
    na = a_heads * HEAD_DIM
    return matmul([ya, yb], [w_out[:na], w_out[na:]], tn=512)


def _odd_mixer(h, rope128, rope64, w_in, w_out, c_qn, c_kn, d_qa_norm, d_kva_norm, d_wq_up, d_wkv_up, d_qn, d_kn, d):
    st = h.shape[0]
    c_heads = d // (2 * HEAD_DIM)
    c_kv = c_heads // 4
    gc = c_heads // c_kv
    d_heads = d // (2 * HEAD_DIM)
    q_rank = d_wq_up.shape[0]
    kv_rank = d_wkv_up.shape[0]
    n_qc, n_qa, n_kc, n_vc, n_kva = c_heads, q_rank // LANES, c_kv, c_kv, kv_rank // LANES
    c_qa = n_qc
    c_kc = c_qa + n_qa
    c_vc = c_kc + n_kc
    c_kva = c_vc + n_vc
    c_kr = c_kva + n_kva
    n_in = (c_kr + 1) * LANES
    tn = 768
    n_pad = -(-n_in // tn) * tn
    p = matmul([h], [_pad_cols(w_in, n_pad).astype(BF16)], tn=tn)
    tm = ROW_TILE
    tab = pl.BlockSpec((tm, LANES), lambda i: (i, 0))
    gains = jnp.stack([c_qn, c_kn], axis=0).astype(F32)
    qk, qa, kva = pl.pallas_call(
        functools.partial(_prep_odd1_kernel, n_q=n_qc, n_k=n_kc, c_k=c_kc, c_qa=c_qa, n_qa=n_qa, c_kva=c_kva,
                          n_kva=n_kva, q_scale=HEAD_DIM ** -0.5 * LOG2E),
        grid=(st // tm,),
        in_specs=[pl.BlockSpec((tm, n_pad), lambda i: (i, 0)), tab, tab, tab,
                  pl.BlockSpec((2, LANES), lambda i: (0, 0)),
                  pl.BlockSpec((1, q_rank), lambda i: (0, 0)),
                  pl.BlockSpec((1, kv_rank), lambda i: (0, 0))],
        out_specs=[pl.BlockSpec((tm, (n_qc + n_kc) * LANES), lambda i: (i, 0)),
                   pl.BlockSpec((tm, q_rank), lambda i: (i, 0)),
                   pl.BlockSpec((tm, kv_rank), lambda i: (i, 0))],
        out_shape=[jax.ShapeDtypeStruct((st, (n_qc + n_kc) * LANES), BF16),
                   jax.ShapeDtypeStruct((st, q_rank), BF16),
                   jax.ShapeDtypeStruct((st, kv_rank), BF16)],
        compiler_params=_cparams("arbitrary"),
        name="prep_odd1",
    )(p, *rope128, gains, d_qa_norm.astype(F32)[None, :], d_kva_norm.astype(F32)[None, :])
    wq = jnp.pad(d_wq_up.reshape(q_rank, d_heads, D_QK), ((0, 0), (0, 0), (0, D_PAD - D_QK)))
    qd_raw = matmul([qa], [wq.reshape(q_rank, d_heads * D_PAD).astype(BF16)], tn=512)
    kv = matmul([kva], [d_wkv_up.astype(BF16)], tn=512)
    pad_gain = lambda gvec: jnp.pad(gvec.astype(F32), (0, D_PAD - D_QK))[None, :]
    qd, kd = pl.pallas_call(
        functools.partial(_prep_odd2_kernel, n_heads=d_heads, q_scale=D_QK ** -0.5 * LOG2E),
        grid=(st // tm,),
        in_specs=[pl.BlockSpec((tm, d_heads * D_PAD), lambda i: (i, 0)),
                  pl.BlockSpec((tm, d_heads * D_PAD), lambda i: (i, 0)),
                  pl.BlockSpec((tm, LANES), lambda i: (i, c_kr)),
                  tab, tab, tab,
                  pl.BlockSpec((1, D_PAD), lambda i: (0, 0)),
                  pl.BlockSpec((1, D_PAD), lambda i: (0, 0))],
        out_specs=[pl.BlockSpec((tm, d_heads * D_PAD), lambda i: (i, 0)),
                   pl.BlockSpec((tm, d_heads * D_PAD), lambda i: (i, 0))],
        out_shape=[jax.ShapeDtypeStruct((st, d_heads * D_PAD), BF16),
                   jax.ShapeDtypeStruct((st, d_heads * D_PAD), BF16)],
        compiler_params=_cparams("arbitrary"),
        name="prep_odd2",
    )(qd_raw, kv, p, *rope64, pad_gain(d_qn), pad_gain(d_kn))
    aux = jnp.zeros((8, LANES), F32)
    flag_c = _score_bound_flag(c_qn, c_kn, HEAD_DIM, HEAD_DIM ** -0.5 * LOG2E)
    flag_d = _score_bound_flag(d_qn, d_kn, D_QK, D_QK ** -0.5 * LOG2E)
    yc = flash_attention(qk, qk, p, aux, flag_c, n_kv=c_kv, g=gc, comps=1, dq=HEAD_DIM, dv=HEAD_DIM,
                         c_q=0, c_k=n_qc, c_v=lambda hh: c_vc + hh, tk=1024)
    yd = flash_attention(qd, kd, kv, aux, flag_d, n_kv=d_heads, g=1, comps=1, dq=D_PAD, dv=D_V,
                         c_q=0, c_k=0, c_v=lambda hh: 2 * hh + 1, tk=2048)
    w_out = w_out.astype(BF16)
    nc = c_heads * HEAD_DIM
    return matmul([yc, yd], [w_out[:nc], w_out[nc:]], tn=512)


def kernel(x, c, ctx, c_ctx, adaln_down, adaln_up, adaln_b, norm_mix, norm_ffn, ev_w_in, ev_w_out, ev_a_qn, ev_a_kn,
           ev_a_sink, ev_b_qn, ev_b_kn, ev_b_lam, ev_b_subln, od_w_in, od_w_out, od_c_qn, od_c_kn, od_d_qa_norm,
           od_d_kva_norm, od_d_wq_up, od_d_wkv_up, od_d_qn, od_d_kn, router_w, router_b, moe_w_gu, moe_b_gu,
           moe_w_down, moe_b_down):
    bsz, seq, d = x.shape
    n_ctx = ctx.shape[1]
    depth = adaln_down.shape[0]
    n_exp = router_w.shape[2]
    assert bsz == 1 and n_ctx == ROW_TILE and seq % ROW_TILE == 0 and n_exp <= LANES

    rope128 = _rope_tables(n_ctx, seq, HEAD_DIM)
    rope64 = _rope_tables(n_ctx, seq, D_ROPE)
    cvecs = jnp.zeros((8, d), F32).at[0].set(c[0]).at[1].set(c_ctx)
    mods = adaln_all(cvecs, adaln_down, adaln_up, adaln_b)

    xs = jnp.concatenate([ctx[0], x[0]], axis=0)
    y_prev, gate_prev = None, None
    for l in range(depth):
        m_lat = mods[l, 0].reshape(N_MOD, d)
        m_ctx = mods[l, 1].reshape(N_MOD, d)
        mult = _streams(norm_mix[l] * (1.0 + m_ctx[1]), norm_mix[l] * (1.0 + m_lat[1]))
        shift = _streams(m_ctx[0], m_lat[0])
        if y_prev is None:
            (h,) = modulate(xs, mult, shift)
        else:
            xs, h = modulate(xs, mult, shift, res=(y_prev, gate_prev))
        j = l // 2
        if l % 2 == 0:
            lam_init = 0.8 - 0.6 * math.exp(-0.3 * l)
            y = _even_mixer(h, rope128, ev_w_in[j], ev_w_out[j], ev_a_qn[j], ev_a_kn[j], ev_a_sink[j], ev_b_qn[j],
                            ev_b_kn[j], ev_b_lam[j], ev_b_subln[j], lam_init, d)
        else:
            y = _odd_mixer(h, rope128, rope64, od_w_in[j], od_w_out[j], od_c_qn[j], od_c_kn[j], od_d_qa_norm[j],
                           od_d_kva_norm[j], od_d_wq_up[j], od_d_wkv_up[j], od_d_qn[j], od_d_kn[j], d)
        mult = _streams(norm_ffn[l] * (1.0 + m_ctx[4]), norm_ffn[l] * (1.0 + m_lat[4]))
        shift = _streams(m_ctx[3], m_lat[3])
        rw = _pad_cols(router_w[l].astype(F32), LANES)
        rb = jnp.pad(router_b[l].astype(F32), (0, LANES - n_exp))[None, :]
        xs, h2, gates = modulate(xs, mult, shift, res=(y, _streams(m_ctx[2], m_lat[2])), router=(rw, rb, n_exp))
        y_prev = moe_dense(h2, gates, moe_w_gu[l].astype(BF16), moe_b_gu[l].astype(F32),
                           moe_w_down[l].astype(BF16), moe_b_down[l].astype(F32))
        gate_prev = _streams(m_ctx[5], m_lat[5])
    out = final_residual(xs, y_prev, gate_prev)
    return out[None]
```

```python
import functools
import math

import jax
import jax.numpy as jnp
from jax import lax
from jax.experimental import pallas as pl
from jax.experimental.pallas import tpu as pltpu

F32 = jnp.float32
BF16 = jnp.bfloat16

GRID_W = 64
HEAD_DIM = 128
ROPE_THETA = 10000.0
EPS = 1e-6
N_MOD = 6
D_NOPE = 128
D_ROPE = 64
D_QK = D_NOPE + D_ROPE
D_V = 128
D_PAD = 256
TOP_K = 4
SWIGLU_ALPHA = 1.702
SWIGLU_LIMIT = 7.0
LOG2E = 1.4426950408889634
NEG_BIG = -1e30
SCORE_BOUND = 60.0
BF16_NORM_SLACK = 1.02

LANES = 128
ROW_TILE = 256
VMEM_LIMIT = 56 * 1024 * 1024


def _cparams(*sem):
    return pltpu.CompilerParams(dimension_semantics=sem, vmem_limit_bytes=VMEM_LIMIT)


def _adaln_kernel(c_ref, down_ref, up_ref, b_ref, o_ref, t_ref):
    @pl.when(pl.program_id(1) == 0)
    def _():
        c = c_ref[...]
        a = c * (1.0 / (1.0 + jnp.exp(-c)))
        t_ref[...] = jnp.dot(a, down_ref[0], preferred_element_type=F32, precision=lax.Precision.HIGHEST)

    o_ref[0] = jnp.dot(t_ref[...], up_ref[0], preferred_element_type=F32,
                       precision=lax.Precision.HIGHEST) + b_ref[0]


def adaln_all(cvecs, down, up, bias):
    depth, d, rank = down.shape
    n = up.shape[2]
    tn = 2048
    assert n % tn == 0
    return pl.pallas_call(
        _adaln_kernel,
        grid=(depth, n // tn),
        in_specs=[
            pl.BlockSpec((8, d), lambda l, j: (0, 0)),
            pl.BlockSpec((1, d, rank), lambda l, j: (l, 0, 0)),
            pl.BlockSpec((1, rank, tn), lambda l, j: (l, 0, j)),
            pl.BlockSpec((1, 1, tn), lambda l, j: (l, 0, j)),
        ],
        out_specs=pl.BlockSpec((1, 8, tn), lambda l, j: (l, 0, j)),
        out_shape=jax.ShapeDtypeStruct((depth, 8, n), F32),
        scratch_shapes=[pltpu.VMEM((8, rank), F32)],
        compiler_params=_cparams("arbitrary", "arbitrary"),
        name="adaln",
    )(cvecs, down, up, bias.reshape(depth, 1, n))


def _topk_route(logits, n_exp):
    lane = lax.broadcasted_iota(jnp.int32, logits.shape, 1).astype(F32)
    work = jnp.where(lane < n_exp, logits, -jnp.inf)
    ids = jnp.zeros_like(logits)
    wts = jnp.zeros_like(logits)
    denom = None
    v0 = None
    for k in range(TOP_K):
        m = jnp.max(work, axis=-1, keepdims=True)
        idx = jnp.min(jnp.where(work == m, lane, float(LANES)), axis=-1, keepdims=True)
        if k == 0:
            v0 = m
            e = jnp.ones_like(m)
            denom = e
        else:
            e = jnp.exp(m - v0)
            denom = denom + e
        ids = jnp.where(lane == k, idx, ids)
        wts = jnp.where(lane == TOP_K + k, e, wts)
        work = jnp.where(lane == idx, -jnp.inf, work)
    return ids + wts / denom


def _modulate_kernel(*refs, has_res, n_exp):
    it = iter(refs)
    x_ref = next(it)
    y_ref = next(it) if has_res else None
    gate_ref = next(it) if has_res else None
    mult_ref = next(it)
    shift_ref = next(it)
    rw_ref = next(it) if n_exp else None
    rb_ref = next(it) if n_exp else None
    xo_ref = next(it) if has_res else None
    h_ref = next(it)
    g_ref = next(it) if n_exp else None

    x = x_ref[...]
    if has_res:
        x = x + gate_ref[0] * y_ref[...].astype(F32)
        xo_ref[...] = x
    ms = jnp.mean(x * x, axis=-1, keepdims=True)
    h = x * lax.rsqrt(ms + EPS) * mult_ref[0] + shift_ref[0]
    h_ref[...] = h.astype(h_ref.dtype)
    if n_exp:
        logits = jnp.dot(h, rw_ref[...], preferred_element_type=F32,
                         precision=lax.Precision.HIGHEST) + rb_ref[...]
        g_ref[...] = _topk_route(logits, n_exp)


def modulate(x, mult, shift, res=None, router=None):
    st, d = x.shape
    tm = ROW_TILE
    stream = lambda i: (jnp.minimum(i, 1), 0, 0)
    row = lambda i: (i, 0)
    vec_spec = pl.BlockSpec((1, 1, d), stream)
    args, in_specs = [x], [pl.BlockSpec((tm, d), row)]
    if res is not None:
        args += [res[0], res[1]]
        in_specs += [pl.BlockSpec((tm, d), row), vec_spec]
    args += [mult, shift]
    in_specs += [vec_spec, vec_spec]
    n_exp = 0
    if router is not None:
        args += [router[0], router[1]]
        n_exp = router[2]
        in_specs += [pl.BlockSpec((d, LANES), lambda i: (0, 0)), pl.BlockSpec((1, LANES), lambda i: (0, 0))]
    out_shape, out_specs = [], []
    if res is not None:
        out_shape.append(jax.ShapeDtypeStruct((st, d), F32))
        out_specs.append(pl.BlockSpec((tm, d), row))
    out_shape.append(jax.ShapeDtypeStruct((st, d), BF16 if router is None else F32))
    out_specs.append(pl.BlockSpec((tm, d), row))
    if router is not None:
        out_shape.append(jax.ShapeDtypeStruct((st, LANES), F32))
        out_specs.append(pl.BlockSpec((tm, LANES), row))
    return pl.pallas_call(
        functools.partial(_modulate_kernel, has_res=res is not None, n_exp=n_exp),
        grid=(st // tm,),
        in_specs=in_specs,
        out_specs=out_specs,
        out_shape=out_shape,
        compiler_params=_cparams("arbitrary"),
        name="modulate",
    )(*args)


def _final_residual_kernel(x_ref, y_ref, gate_ref, o_ref):
    o_ref[...] = x_ref[...] + gate_ref[0] * y_ref[...].astype(F32)


def final_residual(x, y, gate):
    st, d = x.shape
    tm = ROW_TILE
    n = st // tm - 1
    return pl.pallas_call(
        _final_residual_kernel,
        grid=(n,),
        in_specs=[
            pl.BlockSpec((tm, d), lambda i: (i + 1, 0)),
            pl.BlockSpec((tm, d), lambda i: (i + 1, 0)),
            pl.BlockSpec((1, 1, d), lambda i: (1, 0, 0)),
        ],
        out_specs=pl.BlockSpec((tm, d), lambda i: (i, 0)),
        out_shape=jax.ShapeDtypeStruct((n * tm, d), F32),
        compiler_params=_cparams("arbitrary"),
        name="final_residual",
    )(x, y, gate)


def _mm_kernel(*refs, n_pairs):
    o_ref = refs[-1]
    acc = None
    for p in range(n_pairs):
        t = jnp.dot(refs[p][...], refs[n_pairs + p][...], preferred_element_type=F32)
        acc = t if acc is None else acc + t
    o_ref[...] = acc.astype(o_ref.dtype)


def _row_block(m):
    for tm in (1280, 1024, 512, 256):
        if m % tm == 0:
            return tm
    raise ValueError(m)


def matmul(a_list, b_list, tn, out_dtype=BF16):
    m = a_list[0].shape[0]
    n = b_list[0].shape[1]
    tm = _row_block(m)
    while n % tn:
        tn -= LANES
    in_specs = [pl.BlockSpec((tm, a.shape[1]), lambda i, j: (i, 0)) for a in a_list]
    in_specs += [pl.BlockSpec((b.shape[0], tn), lambda i, j: (0, j)) for b in b_list]
    return pl.pallas_call(
        functools.partial(_mm_kernel, n_pairs=len(a_list)),
        grid=(m // tm, n // tn),
        in_specs=in_specs,
        out_specs=pl.BlockSpec((tm, tn), lambda i, j: (i, j)),
        out_shape=jax.ShapeDtypeStruct((m, n), out_dtype),
        compiler_params=_cparams("arbitrary", "arbitrary"),
        name="matmul",
    )(*a_list, *b_list)


def _rope(y, cos, sin_lo, sin_hi, quarter):
    left = pltpu.roll(y, LANES - quarter, 1)
    right = pltpu.roll(y, quarter, 1)
    return y * cos + left * sin_lo + right * sin_hi


def _head_norm(x, gain, n_valid):
    ss = jnp.sum(x * x, axis=-1, keepdims=True)
    return x * lax.rsqrt(ss * (1.0 / n_valid) + EPS) * gain


def _prep_even_kernel(p_ref, cos_ref, slo_ref, shi_ref, gains_ref, o_ref, *, segs, n_copy_from):
    cos, slo, shi = cos_ref[...], slo_ref[...], shi_ref[...]
    for (c0, c1, gi, scale) in segs:
        gain = gains_ref[gi:gi + 1, :]
        for c in range(c0, c1):
            x = p_ref[:, c * LANES:(c + 1) * LANES].astype(F32)
            y = _rope(_head_norm(x, gain, HEAD_DIM), cos, slo, shi, HEAD_DIM // 4)
            if scale != 1.0:
                y = y * scale
            o_ref[:, c * LANES:(c + 1) * LANES] = y.astype(BF16)
    for (c0, c1) in n_copy_from:
        o_ref[:, c0 * LANES:c1 * LANES] = p_ref[:, c0 * LANES:c1 * LANES]


def prep_heads(p, rope128, gains, segs, copies):
    st, n = p.shape
    tm = ROW_TILE
    tab = pl.BlockSpec((tm, LANES), lambda i: (i, 0))
    return pl.pallas_call(
        functools.partial(_prep_even_kernel, segs=segs, n_copy_from=copies),
        grid=(st // tm,),
        in_specs=[pl.BlockSpec((tm, n), lambda i: (i, 0)), tab, tab, tab,
                  pl.BlockSpec(gains.shape, lambda i: (0, 0))],
        out_specs=pl.BlockSpec((tm, n), lambda i: (i, 0)),
        out_shape=jax.ShapeDtypeStruct((st, n), BF16),
        compiler_params=_cparams("arbitrary"),
        name="prep_heads",
    )(p, *rope128, gains)


def _prep_odd1_kernel(p_ref, cos_ref, slo_ref, shi_ref, gains_ref, gqa_ref, gkva_ref,
                      qk_ref, qa_ref, kva_ref, *, n_q, n_k, c_k, c_qa, n_qa, c_kva, n_kva, q_scale):
    cos, slo, shi = cos_ref[...], slo_ref[...], shi_ref[...]
    for c in range(n_q):
        x = p_ref[:, c * LANES:(c + 1) * LANES].astype(F32)
        y = _rope(_head_norm(x, gains_ref[0:1, :], HEAD_DIM), cos, slo, shi, HEAD_DIM // 4) * q_scale
        qk_ref[:, c * LANES:(c + 1) * LANES] = y.astype(BF16)
    for c in range(n_k):
        x = p_ref[:, (c_k + c) * LANES:(c_k + c + 1) * LANES].astype(F32)
        y = _rope(_head_norm(x, gains_ref[1:2, :], HEAD_DIM), cos, slo, shi, HEAD_DIM // 4)
        qk_ref[:, (n_q + c) * LANES:(n_q + c + 1) * LANES] = y.astype(BF16)
    xa = p_ref[:, c_qa * LANES:(c_qa + n_qa) * LANES].astype(F32)
    qa_ref[...] = _head_norm(xa, gqa_ref[...], n_qa * LANES).astype(BF16)
    xk = p_ref[:, c_kva * LANES:(c_kva + n_kva) * LANES].astype(F32)
    kva_ref[...] = _head_norm(xk, gkva_ref[...], n_kva * LANES).astype(BF16)


def _prep_odd2_kernel(qd_ref, kv_ref, kr_ref, cos_ref, slo_ref, shi_ref, gq_ref, gk_ref,
                      qo_ref, ko_ref, *, n_heads, q_scale):
    cos, slo, shi = cos_ref[...], slo_ref[...], shi_ref[...]
    gq_n, gq_r = gq_ref[:, :LANES], gq_ref[:, LANES:]
    gk_n, gk_r = gk_ref[:, :LANES], gk_ref[:, LANES:]
    kr = kr_ref[...].astype(F32)
    kr_ss = jnp.sum(kr * kr, axis=-1, keepdims=True)
    for h in range(n_heads):
        b = h * D_PAD
        qn = qd_ref[:, b:b + LANES].astype(F32)
        qr = qd_ref[:, b + LANES:b + D_PAD].astype(F32)
        ss = jnp.sum(qn * qn, axis=-1, keepdims=True) + jnp.sum(qr * qr, axis=-1, keepdims=True)
        r = lax.rsqrt(ss * (1.0 / D_QK) + EPS) * q_scale
        qo_ref[:, b:b + LANES] = (qn * r * gq_n).astype(BF16)
        qo_ref[:, b + LANES:b + D_PAD] = _rope(qr * r * gq_r, cos, slo, shi, D_ROPE // 4).astype(BF16)
        kn = kv_ref[:, b:b + LANES].astype(F32)
        ss = jnp.sum(kn * kn, axis=-1, keepdims=True) + kr_ss
        r = lax.rsqrt(ss * (1.0 / D_QK) + EPS)
        ko_ref[:, b:b + LANES] = (kn * r * gk_n).astype(BF16)
        ko_ref[:, b + LANES:b + D_PAD] = _rope(kr * r * gk_r, cos, slo, shi, D_ROPE // 4).astype(BF16)


def _stack_heads(q_ref, g, d):
    return jnp.concatenate([q_ref[:, i * d:(i + 1) * d] for i in range(g)], axis=0)


def _qk(q, k):
    return lax.dot_general(q, k, (((1,), (1,)), ((), ())), preferred_element_type=F32)


def _window_attn_kernel(q_ref, kp_ref, km_ref, kn_ref, kc_ref, vp_ref, vm_ref, vn_ref, vc_ref, sink_ref,
                        o_ref, *, g, st):
    i = pl.program_id(1)
    tq = q_ref.shape[0]
    half = tq // 2
    qs = _stack_heads(q_ref, g, HEAD_DIM)
    k_loc = jnp.concatenate([kp_ref[...], km_ref[...], kn_ref[...]], axis=0)
    v_loc = jnp.concatenate([vp_ref[...], vm_ref[...], vn_ref[...]], axis=0)
    s_loc = _qk(qs, k_loc)
    s_ctx = _qk(qs, kc_ref[...])
    a = lax.broadcasted_iota(jnp.int32, s_loc.shape, 0) & (tq - 1)
    j = lax.broadcasted_iota(jnp.int32, s_loc.shape, 1)
    diff = j - half - a
    krow = i * tq - half + j
    valid = (jnp.abs(diff) <= half) & (krow >= tq) & (krow < st) & (i >= 1)
    s_loc = jnp.where(valid, s_loc, NEG_BIG)
    sink = jnp.concatenate([jnp.broadcast_to(sink_ref[0, h:h + 1, 0:1], (tq, 1)) for h in range(g)], axis=0)
    m = jnp.maximum(jnp.maximum(jnp.max(s_loc, axis=-1, keepdims=True),
                                jnp.max(s_ctx, axis=-1, keepdims=True)), sink)
    p_loc = jnp.exp2(s_loc - m)
    p_ctx = jnp.exp2(s_ctx - m)
    l = jnp.sum(p_loc, axis=-1, keepdims=True) + jnp.sum(p_ctx, axis=-1, keepdims=True) + jnp.exp2(sink - m)
    o = (jnp.dot(p_loc.astype(BF16), v_loc, preferred_element_type=F32)
         + jnp.dot(p_ctx.astype(BF16), vc_ref[...], preferred_element_type=F32)) / l
    for h in range(g):
        o_ref[:, h * HEAD_DIM:(h + 1) * HEAD_DIM] = o[h * tq:(h + 1) * tq].astype(BF16)


def window_attention(pn, sink, n_kv, g, c_q, c_k, c_v):
    st = pn.shape[0]
    tq = ROW_TILE
    half = tq // 2
    nh = st // half
    q_spec = pl.BlockSpec((tq, g * HEAD_DIM), lambda h, i: (i, c_q // g + h))

    def kv_specs(c0):
        return [
            pl.BlockSpec((half, HEAD_DIM), lambda h, i: (jnp.maximum(2 * i - 1, 0), c0 + h)),
            pl.BlockSpec((tq, HEAD_DIM), lambda h, i: (i, c0 + h)),
            pl.BlockSpec((half, HEAD_DIM), lambda h, i: (jnp.minimum(2 * i + 2, nh - 1), c0 + h)),
            pl.BlockSpec((tq, HEAD_DIM), lambda h, i: (0, c0 + h)),
        ]

    return pl.pallas_call(
        functools.partial(_window_attn_kernel, g=g, st=st),
        grid=(n_kv, st // tq),
        in_specs=[q_spec] + kv_specs(c_k) + kv_specs(c_v)
        + [pl.BlockSpec((1, g, LANES), lambda h, i: (h, 0, 0))],
        out_specs=pl.BlockSpec((tq, g * HEAD_DIM), lambda h, i: (i, h)),
        out_shape=jax.ShapeDtypeStruct((st, n_kv * g * HEAD_DIM), BF16),
        compiler_params=_cparams("arbitrary", "arbitrary"),
        name="window_attention",
    )(pn, pn, pn, pn, pn, pn, pn, pn, pn, sink)


def _flash_kernel(flag_ref, q_ref, k_ref, v_ref, aux_ref, o_ref, acc_ref, l_ref, *, g, comps, dq, dv, tk, diff):
    qi = pl.program_id(1)
    tq = q_ref.shape[0]
    n_keys = k_ref.shape[0]
    n_lat = (n_keys - tq) // tk
    n_iter = jnp.where(qi == 0, 0, n_lat)
    qs = [jnp.concatenate([q_ref[:, (gi * comps + c) * dq:(gi * comps + c + 1) * dq] for gi in range(g)], axis=0)
          for c in range(comps)]

    def scores(start, size):
        return jnp.concatenate(
            [_qk(qs[c], k_ref[pl.ds(start, size), c * dq:(c + 1) * dq]) for c in range(comps)], axis=0)

    def chunk_start(j):
        return pl.multiple_of(tq + j * tk, math.gcd(tq, tk))

    @pl.when(flag_ref[0] == 1)
    def _bounded():
        def step(start, size, first):
            p = jnp.exp2(scores(start, size))
            psum = p[:, :LANES]
            for b in range(1, size // LANES):
                psum = psum + p[:, b * LANES:(b + 1) * LANES]
            pv = jnp.dot(p.astype(BF16), v_ref[pl.ds(start, size), :], preferred_element_type=F32)
            if first:
                acc_ref[...] = pv
                l_ref[...] = psum
            else:
                acc_ref[...] += pv
                l_ref[...] += psum

        step(0, tq, True)

        def body(j, carry):
            step(chunk_start(j), tk, False)
            return carry

        lax.fori_loop(0, n_iter, body, 0)
        l_ref[...] = jnp.broadcast_to(jnp.sum(l_ref[...], axis=-1, keepdims=True), l_ref.shape)

    @pl.when(flag_ref[0] != 1)
    def _online():
        def step(start, size, m, l, first):
            s = scores(start, size)
            m_cur = jnp.max(s, axis=-1, keepdims=True)
            m_new = m_cur if first else jnp.maximum(m, m_cur)
            p = jnp.exp2(s - m_new)
            pv = jnp.dot(p.astype(BF16), v_ref[pl.ds(start, size), :], preferred_element_type=F32)
            if first:
                l_new = jnp.sum(p, axis=-1, keepdims=True)
                acc_ref[...] = pv
            else:
                alpha = jnp.exp2(m - m_new)
                l_new = alpha * l + jnp.sum(p, axis=-1, keepdims=True)
                acc_ref[...] = alpha * acc_ref[...] + pv
            return m_new, l_new

        m, l = step(0, tq, None, None, True)
        m, l = lax.fori_loop(0, n_iter, lambda j, c: step(chunk_start(j), tk, c[0], c[1], False), (m, l))
        l_ref[...] = jnp.broadcast_to(l, l_ref.shape)

    o = acc_ref[...] / l_ref[:, 0:1]
    if diff:
        lam = aux_ref[0:1, 0:1]
        for gi in range(g):
            y = o[gi * tq:(gi + 1) * tq] - lam * o[(g + gi) * tq:(g + gi + 1) * tq]
            ss = jnp.mean(y * y, axis=-1, keepdims=True)
            y = y * lax.rsqrt(ss + EPS) * aux_ref[1:2, :]
            o_ref[:, gi * dv:(gi + 1) * dv] = y.astype(BF16)
    else:
        for gi in range(g):
            o_ref[:, gi * dv:(gi + 1) * dv] = o[gi * tq:(gi + 1) * tq].astype(BF16)


def _score_bound_flag(q_gain, k_gain, n_norm, scale):
    bound = n_norm * jnp.max(jnp.abs(q_gain)) * jnp.max(jnp.abs(k_gain)) * scale * BF16_NORM_SLACK
    return (bound <= SCORE_BOUND).astype(jnp.int32).reshape(1)


def flash_attention(q_arr, k_arr, v_arr, aux, flag, *, n_kv, g, comps, dq, dv, c_q, c_k, c_v, tk, diff=False):
    st = q_arr.shape[0]
    tq = ROW_TILE
    while (st - tq) % tk:
        tk //= 2
    rows = g * comps * tq
    return pl.pallas_call(
        functools.partial(_flash_kernel, g=g, comps=comps, dq=dq, dv=dv, tk=tk, diff=diff),
        grid=(n_kv, st // tq),
        in_specs=[
            pl.BlockSpec(memory_space=pltpu.SMEM),
            pl.BlockSpec((tq, g * comps * dq), lambda h, i: (i, c_q + h)),
            pl.BlockSpec((st, comps * dq), lambda h, i: (0, c_k + h)),
            pl.BlockSpec((st, dv), lambda h, i: (0, c_v(h))),
            pl.BlockSpec(aux.shape, lambda h, i: (0, 0)),
        ],
        out_specs=pl.BlockSpec((tq, g * dv), lambda h, i: (i, h)),
        out_shape=jax.ShapeDtypeStruct((st, n_kv * g * dv), BF16),
        scratch_shapes=[pltpu.VMEM((rows, dv), F32), pltpu.VMEM((rows, LANES), F32)],
        compiler_params=_cparams("arbitrary", "arbitrary"),
        name="flash_attention",
    )(flag, q_arr, k_arr, v_arr, aux)


MOE_TILE = 256


def _gather_kernel(idx_ref, src_ref, o_ref, sem):
    tr = o_ref.shape[0]
    base = pl.program_id(0) * tr

    def row_copy(r, src_row):
        return pltpu.make_async_copy(src_ref.at[pl.ds(src_row, 1)], o_ref.at[pl.ds(r, 1)], sem)

    def issue(r, carry):
        row_copy(r, idx_ref[base + r]).start()
        return carry

    def drain(r, carry):
        row_copy(r, 0).wait()
        return carry

    lax.fori_loop(0, tr, issue, 0)
    lax.fori_loop(0, tr, drain, 0)


def row_gather(src, idx):
    m = idx.shape[0]
    d = src.shape[1]
    tr = MOE_TILE
    return pl.pallas_call(
        _gather_kernel,
        grid_spec=pltpu.PrefetchScalarGridSpec(
            num_scalar_prefetch=1,
            grid=(m // tr,),
            in_specs=[pl.BlockSpec(memory_space=pl.ANY)],
            out_specs=pl.BlockSpec((tr, d), lambda i, idx_ref: (i, 0)),
            scratch_shapes=[pltpu.SemaphoreType.DMA(())],
        ),
        out_shape=jax.ShapeDtypeStruct((m, d), src.dtype),
        compiler_params=_cparams("arbitrary"),
        name="row_gather",
    )(idx, src)


def _grouped_moe_kernel(te_ref, nt_ref, x_ref, wgu_ref, bgu_ref, wd_ref, bd_ref, o_ref, *, ff):
    t = pl.program_id(0)

    @pl.when(t < nt_ref[0])
    def _():
        gu = jnp.dot(x_ref[...].astype(BF16), wgu_ref[0], preferred_element_type=F32) + bgu_ref[0]
        glu = jnp.minimum(gu[:, :ff], SWIGLU_LIMIT)
        lin = jnp.clip(gu[:, ff:], -SWIGLU_LIMIT, SWIGLU_LIMIT)
        act = glu * (1.0 / (1.0 + jnp.exp(-SWIGLU_ALPHA * glu))) * (lin + 1.0)
        o_ref[...] = jnp.dot(act.astype(BF16), wd_ref[0], preferred_element_type=F32) + bd_ref[0]

    @pl.when(t >= nt_ref[0])
    def _():
        o_ref[...] = jnp.zeros_like(o_ref)


def grouped_moe(xs, tile_expert, n_tiles_used, w_gu, b_gu, w_down, b_down):
    p, d = xs.shape
    n_exp, _, ff2 = w_gu.shape
    ff = ff2 // 2
    tg = MOE_TILE
    return pl.pallas_call(
        functools.partial(_grouped_moe_kernel, ff=ff),
        grid_spec=pltpu.PrefetchScalarGridSpec(
            num_scalar_prefetch=2,
            grid=(p // tg,),
            in_specs=[
                pl.BlockSpec((tg, d), lambda t, te, nt: (t, 0)),
                pl.BlockSpec((1, d, ff2), lambda t, te, nt: (te[t], 0, 0)),
                pl.BlockSpec((1, 1, ff2), lambda t, te, nt: (te[t], 0, 0)),
                pl.BlockSpec((1, ff, d), lambda t, te, nt: (te[t], 0, 0)),
                pl.BlockSpec((1, 1, d), lambda t, te, nt: (te[t], 0, 0)),
            ],
            out_specs=pl.BlockSpec((tg, d), lambda t, te, nt: (t, 0)),
        ),
        out_shape=jax.ShapeDtypeStruct((p, d), F32),
        compiler_params=_cparams("arbitrary"),
        name="grouped_moe",
    )(tile_expert, n_tiles_used, xs, w_gu, b_gu.reshape(n_exp, 1, ff2), w_down, b_down.reshape(n_exp, 1, d))


def _combine_kernel(route_ref, *refs):
    o_ref = refs[-1]
    route = route_ref[...]
    acc = None
    for k in range(TOP_K):
        term = route[:, TOP_K + k:TOP_K + k + 1] * refs[k][...]
        acc = term if acc is None else acc + term
    o_ref[...] = acc.astype(o_ref.dtype)


def moe_combine(route, ys4):
    st = route.shape[0]
    d = ys4.shape[1]
    tm = ROW_TILE
    nt = st // tm
    return pl.pallas_call(
        _combine_kernel,
        grid=(nt,),
        in_specs=[pl.BlockSpec((tm, LANES), lambda i: (i, 0))]
        + [pl.BlockSpec((tm, d), functools.partial(lambda i, k: (k * nt + i, 0), k=k)) for k in range(TOP_K)],
        out_specs=pl.BlockSpec((tm, d), lambda i: (i, 0)),
        out_shape=jax.ShapeDtypeStruct((st, d), BF16),
        compiler_params=_cparams("arbitrary"),
        name="moe_combine",
    )(route, *([ys4] * TOP_K))


def moe_sparse(h, route, w_gu, b_gu, w_down, b_down):
    st = h.shape[0]
    n_exp = w_gu.shape[0]
    tg = MOE_TILE
    ids = route[:, :TOP_K].astype(jnp.int32)
    onehot = (ids[:, :, None] == jnp.arange(n_exp)[None, None, :]).any(axis=1)
    counts = jnp.sum(onehot, axis=0, dtype=jnp.int32)
    tiles = (counts + tg - 1) // tg
    tile_end = jnp.cumsum(tiles)
    row_off = (tile_end - tiles) * tg
    rank = jnp.cumsum(onehot.astype(jnp.int32), axis=0) - 1
    pos = row_off[ids] + jnp.take_along_axis(rank, ids, axis=1)
    n_tiles = (TOP_K * st + n_exp * (tg - 1)) // tg
    tok = jnp.zeros((n_tiles * tg,), jnp.int32).at[pos.reshape(-1)].set(
        jnp.repeat(jnp.arange(st, dtype=jnp.int32), TOP_K), unique_indices=True)
    tile_expert = jnp.minimum(jnp.searchsorted(tile_end, jnp.arange(n_tiles), side="right"),
                              n_exp - 1).astype(jnp.int32)
    xs = row_gather(h, tok)
    ys = grouped_moe(xs, tile_expert, tile_end[-1:].astype(jnp.int32), w_gu, b_gu, w_down, b_down)
    ys4 = row_gather(ys, pos.T.reshape(-1).astype(jnp.int32))
    return moe_combine(route, ys4)


def _rope_tables(n_ctx, n_lat, dim):
    quarter = dim // 4
    inv_freq = ROPE_THETA ** (-jnp.arange(quarter, dtype=F32) / quarter)
    t = jnp.arange(n_lat)
    row = (t // GRID_W).astype(F32)
    col = (t % GRID_W).astype(F32)
    ang = jnp.stack([row[:, None] * inv_freq, col[:, None] * inv_freq], axis=1)
    ang = jnp.broadcast_to(ang[:, :, None, :], (n_lat, 2, 2, quarter)).reshape(n_lat, dim)
    cos, sin = jnp.cos(ang), jnp.sin(ang)
    first_half = (jnp.arange(dim) % (2 * quarter)) < quarter
    sin_lo = jnp.where(first_half, -sin, 0.0)
    sin_hi = jnp.where(first_half, 0.0, sin)

    def full(tab, fill):
        tab = jnp.pad(tab, ((0, 0), (0, LANES - dim)), constant_values=fill)
        return jnp.concatenate([jnp.full((n_ctx, LANES), fill, F32), tab], axis=0)

    return full(cos, 1.0), full(sin_lo, 0.0), full(sin_hi, 0.0)


def _pad_cols(w, n):
    return jnp.pad(w, ((0, 0), (0, n - w.shape[1])))


def _streams(v_ctx, v_lat):
    return jnp.stack([v_ctx, v_lat], axis=0)[:, None, :]


def _even_mixer(h, rope128, w_in, w_out, a_qn, a_kn, a_sink, b_qn, b_kn, b_lam, b_subln, lam_init, d):
    a_heads = d // (2 * HEAD_DIM)
    a_kv = a_heads // 4
    b_heads = d // (4 * HEAD_DIM)
    b_kv = b_heads // 2
    ga, gb = a_heads // a_kv, b_heads // b_kv
    scale = HEAD_DIM ** -0.5 * LOG2E
    n_qa, n_qb, n_ka, n_va, n_kb, n_vb = a_heads, 2 * b_heads, a_kv, a_kv, 2 * b_kv, 2 * b_kv
    c_qb = n_qa
    c_ka = c_qb + n_qb
    c_va = c_ka + n_ka
    c_kb = c_va + n_va
    c_vb = c_kb + n_kb
    p = matmul([h], [w_in.astype(BF16)], tn=512)
    gains = jnp.stack([a_qn, b_qn, a_kn, b_kn], axis=0).astype(F32)
    pn = prep_heads(p, rope128, gains,
                    segs=((0, c_qb, 0, scale), (c_qb, c_ka, 1, scale), (c_ka, c_va, 2, 1.0), (c_kb, c_vb, 3, 1.0)),
                    copies=((c_va, c_kb), (c_vb, c_vb + n_vb)))
    sink = jnp.broadcast_to((a_sink.astype(F32) * LOG2E).reshape(a_kv, ga, 1), (a_kv, ga, LANES))
    ya = window_attention(pn, sink, a_kv, ga, 0, c_ka, c_va)
    lf = b_lam.astype(F32)
    lam = jnp.exp(jnp.sum(lf[0] * lf[1])) - jnp.exp(jnp.sum(lf[2] * lf[3])) + lam_init
    dvb = 2 * HEAD_DIM
    aux = jnp.stack([jnp.full((dvb,), lam, F32), b_subln.astype(F32) * (1.0 - lam_init)], axis=0)
    flag = _score_bound_flag(b_qn, b_kn, HEAD_DIM, scale)
    yb = flash_attention(pn, pn, pn, aux, flag, n_kv=b_kv, g=gb, comps=2, dq=HEAD_DIM, dv=dvb,
                         c_q=c_qb * HEAD_DIM // (gb * 2 * HEAD_DIM), c_k=c_kb // 2, c_v=lambda hh: c_vb // 2 + hh,
                         tk=1024, diff=True)
    w_out = w_out.astype(BF16)
    na = a_heads * HEAD_DIM
    return matmul([ya, yb], [w_out[:na], w_out[na:]], tn=512)


def _odd_mixer(h, rope128, rope64, w_in, w_out, c_qn, c_kn, d_qa_norm, d_kva_norm, d_wq_up, d_wkv_up, d_qn, d_kn, d):
    st = h.shape[0]
    c_heads = d // (2 * HEAD_DIM)
    c_kv = c_heads // 4
    gc = c_heads // c_kv
    d_heads = d // (2 * HEAD_DIM)
    q_rank = d_wq_up.shape[0]
    kv_rank = d_wkv_up.shape[0]
    n_qc, n_qa, n_kc, n_vc, n_kva = c_heads, q_rank // LANES, c_kv, c_kv, kv_rank // LANES
    c_qa = n_qc
    c_kc = c_qa + n_qa
    c_vc = c_kc + n_kc
    c_kva = c_vc + n_vc
    c_kr = c_kva + n_kva
    n_in = (c_kr + 1) * LANES
    tn = 768
    n_pad = -(-n_in // tn) * tn
    p = matmul([h], [_pad_cols(w_in, n_pad).astype(BF16)], tn=tn)
    tm = ROW_TILE
    tab = pl.BlockSpec((tm, LANES), lambda i: (i, 0))
    gains = jnp.stack([c_qn, c_kn], axis=0).astype(F32)
    qk, qa, kva = pl.pallas_call(
        functools.partial(_prep_odd1_kernel, n_q=n_qc, n_k=n_kc, c_k=c_kc, c_qa=c_qa, n_qa=n_qa, c_kva=c_kva,
                          n_kva=n_kva, q_scale=HEAD_DIM ** -0.5 * LOG2E),
        grid=(st // tm,),
        in_specs=[pl.BlockSpec((tm, n_pad), lambda i: (i, 0)), tab, tab, tab,
                  pl.BlockSpec((2, LANES), lambda i: (0, 0)),
                  pl.BlockSpec((1, q_rank), lambda i: (0, 0)),
                  pl.BlockSpec((1, kv_rank), lambda i: (0, 0))],
        out_specs=[pl.BlockSpec((tm, (n_qc + n_kc) * LANES), lambda i: (i, 0)),
                   pl.BlockSpec((tm, q_rank), lambda i: (i, 0)),
                   pl.BlockSpec((tm, kv_rank), lambda i: (i, 0))],
        out_shape=[jax.ShapeDtypeStruct((st, (n_qc + n_kc) * LANES), BF16),
                   jax.ShapeDtypeStruct((st, q_rank), BF16),
                   jax.ShapeDtypeStruct((st, kv_rank), BF16)],
        compiler_params=_cparams("arbitrary"),
        name="prep_odd1",
    )(p, *rope128, gains, d_qa_norm.astype(F32)[None, :], d_kva_norm.astype(F32)[None, :])
    wq = jnp.pad(d_wq_up.reshape(q_rank, d_heads, D_QK), ((0, 0), (0, 0), (0, D_PAD - D_QK)))
    qd_raw = matmul([qa], [wq.reshape(q_rank, d_heads * D_PAD).astype(BF16)], tn=512)
    kv = matmul([kva], [d_wkv_up.astype(BF16)], tn=512)
    pad_gain = lambda gvec: jnp.pad(gvec.astype(F32), (0, D_PAD - D_QK))[None, :]
    qd, kd = pl.pallas_call(
        functools.partial(_prep_odd2_kernel, n_heads=d_heads, q_scale=D_QK ** -0.5 * LOG2E),
        grid=(st // tm,),
        in_specs=[pl.BlockSpec((tm, d_heads * D_PAD), lambda i: (i, 0)),
                  pl.BlockSpec((tm, d_heads * D_PAD), lambda i: (i, 0)),
                  pl.BlockSpec((tm, LANES), lambda i: (i, c_kr)),
                  tab, tab, tab,
                  pl.BlockSpec((1, D_PAD), lambda i: (0, 0)),
                  pl.BlockSpec((1, D_PAD), lambda i: (0, 0))],
        out_specs=[pl.BlockSpec((tm, d_heads * D_PAD), lambda i: (i, 0)),
                   pl.BlockSpec((tm, d_heads * D_PAD), lambda i: (i, 0))],
        out_shape=[jax.ShapeDtypeStruct((st, d_heads * D_PAD), BF16),
                   jax.ShapeDtypeStruct((st, d_heads * D_PAD), BF16)],
        compiler_params=_cparams("arbitrary"),
        name="prep_odd2",
    )(qd_raw, kv, p, *rope64, pad_gain(d_qn), pad_gain(d_kn))
    aux = jnp.zeros((8, LANES), F32)
    flag_c = _score_bound_flag(c_qn, c_kn, HEAD_DIM, HEAD_DIM ** -0.5 * LOG2E)
    flag_d = _score_bound_flag(d_qn, d_kn, D_QK, D_QK ** -0.5 * LOG2E)
    yc = flash_attention(qk, qk, p, aux, flag_c, n_kv=c_kv, g=gc, comps=1, dq=HEAD_DIM, dv=HEAD_DIM,
                         c_q=0, c_k=n_qc, c_v=lambda hh: c_vc + hh, tk=1024)
    yd = flash_attention(qd, kd, kv, aux, flag_d, n_kv=d_heads, g=1, comps=1, dq=D_PAD, dv=D_V,
                         c_q=0, c_k=0, c_v=lambda hh: 2 * hh + 1, tk=2048)
    w_out = w_out.astype(BF16)
    nc = c_heads * HEAD_DIM
    return matmul([yc, yd], [w_out[:nc], w_out[nc:]], tn=512)


def kernel(x, c, ctx, c_ctx, adaln_down, adaln_up, adaln_b, norm_mix, norm_ffn, ev_w_in, ev_w_out, ev_a_qn, ev_a_kn,
           ev_a_sink, ev_b_qn, ev_b_kn, ev_b_lam, ev_b_subln, od_w_in, od_w_out, od_c_qn, od_c_kn, od_d_qa_norm,
           od_d_kva_norm, od_d_wq_up, od_d_wkv_up, od_d_qn, od_d_kn, router_w, router_b, moe_w_gu, moe_b_gu,
           moe_w_down, moe_b_down):
    bsz, seq, d = x.shape
    n_ctx = ctx.shape[1]
    depth = adaln_down.shape[0]
    n_exp = router_w.shape[2]
    assert bsz == 1 and n_ctx == ROW_TILE and seq % ROW_TILE == 0 and n_exp <= LANES

    rope128 = _rope_tables(n_ctx, seq, HEAD_DIM)
    rope64 = _rope_tables(n_ctx, seq, D_ROPE)
    cvecs = jnp.zeros((8, d), F32).at[0].set(c[0]).at[1].set(c_ctx)
    mods = adaln_all(cvecs, adaln_down, adaln_up, adaln_b)

    xs = jnp.concatenate([ctx[0], x[0]], axis=0)
    y_prev, gate_prev = None, None
    for l in range(depth):
        m_lat = mods[l, 0].reshape(N_MOD, d)
        m_ctx = mods[l, 1].reshape(N_MOD, d)
        mult = _streams(norm_mix[l] * (1.0 + m_ctx[1]), norm_mix[l] * (1.0 + m_lat[1]))
        shift = _streams(m_ctx[0], m_lat[0])
        if y_prev is None:
            (h,) = modulate(xs, mult, shift)
        else:
            xs, h = modulate(xs, mult, shift, res=(y_prev, gate_prev))
        j = l // 2
        if l % 2 == 0:
            lam_init = 0.8 - 0.6 * math.exp(-0.3 * l)
            y = _even_mixer(h, rope128, ev_w_in[j], ev_w_out[j], ev_a_qn[j], ev_a_kn[j], ev_a_sink[j], ev_b_qn[j],
                            ev_b_kn[j], ev_b_lam[j], ev_b_subln[j], lam_init, d)
        else:
            y = _odd_mixer(h, rope128, rope64, od_w_in[j], od_w_out[j], od_c_qn[j], od_c_kn[j], od_d_qa_norm[j],
                           od_d_kva_norm[j], od_d_wq_up[j], od_d_wkv_up[j], od_d_qn[j], od_d_kn[j], d)
        mult = _streams(norm_ffn[l] * (1.0 + m_ctx[4]), norm_ffn[l] * (1.0 + m_lat[4]))
        shift = _streams(m_ctx[3], m_lat[3])
        rw = _pad_cols(router_w[l].astype(F32), LANES)
        rb = jnp.pad(router_b[l].astype(F32), (0, LANES - n_exp))[None, :]
        xs, h2, route = modulate(xs, mult, shift, res=(y, _streams(m_ctx[2], m_lat[2])), router=(rw, rb, n_exp))
        y_prev = moe_sparse(h2, route, moe_w_gu[l].astype(BF16), moe_b_gu[l].astype(F32),
                            moe_w_down[l].astype(BF16), moe_b_down[l].astype(F32))
        gate_prev = _streams(m_ctx[5], m_lat[5])
    out = final_residual(xs, y_prev, gate_prev)
    return out[None]
```

```python
import functools
import math

import jax
import jax.numpy as jnp
from jax import lax
from jax.experimental import pallas as pl
from jax.experimental.pallas import tpu as pltpu

F32 = jnp.float32
BF16 = jnp.bfloat16

GRID_W = 64
HEAD_DIM = 128
ROPE_THETA = 10000.0
EPS = 1e-6
N_MOD = 6
D_NOPE = 128
D_ROPE = 64
D_QK = D_NOPE + D_ROPE
D_V = 128
D_PAD = 256
TOP_K = 4
SWIGLU_ALPHA = 1.702
SWIGLU_LIMIT = 7.0
LOG2E = 1.4426950408889634
NEG_BIG = -1e30
SCORE_BOUND = 60.0
BF16_NORM_SLACK = 1.02

LANES = 128
ROW_TILE = 256
VMEM_LIMIT = 56 * 1024 * 1024


def _cparams(*sem):
    return pltpu.CompilerParams(dimension_semantics=sem, vmem_limit_bytes=VMEM_LIMIT)


def _adaln_kernel(c_ref, down_ref, up_ref, b_ref, o_ref, t_ref):
    @pl.when(pl.program_id(1) == 0)
    def _():
        c = c_ref[...]
        a = c * (1.0 / (1.0 + jnp.exp(-c)))
        t_ref[...] = jnp.dot(a, down_ref[0], preferred_element_type=F32, precision=lax.Precision.HIGHEST)

    o_ref[0] = jnp.dot(t_ref[...], up_ref[0], preferred_element_type=F32,
                       precision=lax.Precision.HIGHEST) + b_ref[0]


def adaln_all(cvecs, down, up, bias):
    depth, d, rank = down.shape
    n = up.shape[2]
    tn = 2048
    assert n % tn == 0
    return pl.pallas_call(
        _adaln_kernel,
        grid=(depth, n // tn),
        in_specs=[
            pl.BlockSpec((8, d), lambda l, j: (0, 0)),
            pl.BlockSpec((1, d, rank), lambda l, j: (l, 0, 0)),
            pl.BlockSpec((1, rank, tn), lambda l, j: (l, 0, j)),
            pl.BlockSpec((1, 1, tn), lambda l, j: (l, 0, j)),
        ],
        out_specs=pl.BlockSpec((1, 8, tn), lambda l, j: (l, 0, j)),
        out_shape=jax.ShapeDtypeStruct((depth, 8, n), F32),
        scratch_shapes=[pltpu.VMEM((8, rank), F32)],
        compiler_params=_cparams("arbitrary", "arbitrary"),
        name="adaln",
    )(cvecs, down, up, bias.reshape(depth, 1, n))


def _topk_route(logits, n_exp):
    lane = lax.broadcasted_iota(jnp.int32, logits.shape, 1).astype(F32)
    work = jnp.where(lane < n_exp, logits, -jnp.inf)
    ids = jnp.zeros_like(logits)
    wts = jnp.zeros_like(logits)
    denom = None
    v0 = None
    for k in range(TOP_K):
        m = jnp.max(work, axis=-1, keepdims=True)
        idx = jnp.min(jnp.where(work == m, lane, float(LANES)), axis=-1, keepdims=True)
        if k == 0:
            v0 = m
            e = jnp.ones_like(m)
            denom = e
        else:
            e = jnp.exp(m - v0)
            denom = denom + e
        ids = jnp.where(lane == k, idx, ids)
        wts = jnp.where(lane == TOP_K + k, e, wts)
        work = jnp.where(lane == idx, -jnp.inf, work)
    return ids + wts / denom


def _modulate_kernel(*refs, has_res, n_exp):
    it = iter(refs)
    x_ref = next(it)
    y_ref = next(it) if has_res else None
    gate_ref = next(it) if has_res else None
    mult_ref = next(it)
    shift_ref = next(it)
    rw_ref = next(it) if n_exp else None
    rb_ref = next(it) if n_exp else None
    xo_ref = next(it) if has_res else None
    h_ref = next(it)
    g_ref = next(it) if n_exp else None

    x = x_ref[...]
    if has_res:
        x = x + gate_ref[0] * y_ref[...].astype(F32)
        xo_ref[...] = x
    ms = jnp.mean(x * x, axis=-1, keepdims=True)
    h = x * lax.rsqrt(ms + EPS) * mult_ref[0] + shift_ref[0]
    h_ref[...] = h.astype(h_ref.dtype)
    if n_exp:
        logits = jnp.dot(h, rw_ref[...], preferred_element_type=F32,
                         precision=lax.Precision.HIGHEST) + rb_ref[...]
        g_ref[...] = _topk_route(logits, n_exp)


def modulate(x, mult, shift, res=None, router=None):
    st, d = x.shape
    tm = ROW_TILE
    stream = lambda i: (jnp.minimum(i, 1), 0, 0)
    row = lambda i: (i, 0)
    vec_spec = pl.BlockSpec((1, 1, d), stream)
    args, in_specs = [x], [pl.BlockSpec((tm, d), row)]
    if res is not None:
        args += [res[0], res[1]]
        in_specs += [pl.BlockSpec((tm, d), row), vec_spec]
    args += [mult, shift]
    in_specs += [vec_spec, vec_spec]
    n_exp = 0
    if router is not None:
        args += [router[0], router[1]]
        n_exp = router[2]
        in_specs += [pl.BlockSpec((d, LANES), lambda i: (0, 0)), pl.BlockSpec((1, LANES), lambda i: (0, 0))]
    out_shape, out_specs = [], []
    if res is not None:
        out_shape.append(jax.ShapeDtypeStruct((st, d), F32))
        out_specs.append(pl.BlockSpec((tm, d), row))
    out_shape.append(jax.ShapeDtypeStruct((st, d), BF16 if router is None else F32))
    out_specs.append(pl.BlockSpec((tm, d), row))
    if router is not None:
        out_shape.append(jax.ShapeDtypeStruct((st, LANES), F32))
        out_specs.append(pl.BlockSpec((tm, LANES), row))
    return pl.pallas_call(
        functools.partial(_modulate_kernel, has_res=res is not None, n_exp=n_exp),
        grid=(st // tm,),
        in_specs=in_specs,
        out_specs=out_specs,
        out_shape=out_shape,
        compiler_params=_cparams("arbitrary"),
        name="modulate",
    )(*args)


def _final_residual_kernel(x_ref, y_ref, gate_ref, o_ref):
    o_ref[...] = x_ref[...] + gate_ref[0] * y_ref[...].astype(F32)


def final_residual(x, y, gate):
    st, d = x.shape
    tm = ROW_TILE
    n = st // tm - 1
    return pl.pallas_call(
        _final_residual_kernel,
        grid=(n,),
        in_specs=[
            pl.BlockSpec((tm, d), lambda i: (i + 1, 0)),
            pl.BlockSpec((tm, d), lambda i: (i + 1, 0)),
            pl.BlockSpec((1, 1, d), lambda i: (1, 0, 0)),
        ],
        out_specs=pl.BlockSpec((tm, d), lambda i: (i, 0)),
        out_shape=jax.ShapeDtypeStruct((n * tm, d), F32),
        compiler_params=_cparams("arbitrary"),
        name="final_residual",
    )(x, y, gate)


def _mm_kernel(*refs, n_pairs):
    o_ref = refs[-1]
    acc = None
    for p in range(n_pairs):
        t = jnp.dot(refs[p][...], refs[n_pairs + p][...], preferred_element_type=F32)
        acc = t if acc is None else acc + t
    o_ref[...] = acc.astype(o_ref.dtype)


def _row_block(m):
    for tm in (1280, 1024, 512, 256):
        if m % tm == 0:
            return tm
    raise ValueError(m)


def matmul(a_list, b_list, tn, out_dtype=BF16):
    m = a_list[0].shape[0]
    n = b_list[0].shape[1]
    tm = _row_block(m)
    while n % tn:
        tn -= LANES
    in_specs = [pl.BlockSpec((tm, a.shape[1]), lambda i, j: (i, 0)) for a in a_list]
    in_specs += [pl.BlockSpec((b.shape[0], tn), lambda i, j: (0, j)) for b in b_list]
    return pl.pallas_call(
        functools.partial(_mm_kernel, n_pairs=len(a_list)),
        grid=(m // tm, n // tn),
        in_specs=in_specs,
        out_specs=pl.BlockSpec((tm, tn), lambda i, j: (i, j)),
        out_shape=jax.ShapeDtypeStruct((m, n), out_dtype),
        compiler_params=_cparams("arbitrary", "arbitrary"),
        name="matmul",
    )(*a_list, *b_list)


def _rope(y, cos, sin_lo, sin_hi, quarter):
    left = pltpu.roll(y, LANES - quarter, 1)
    right = pltpu.roll(y, quarter, 1)
    return y * cos + left * sin_lo + right * sin_hi


def _head_norm(x, gain, n_valid):
    ss = jnp.sum(x * x, axis=-1, keepdims=True)
    return x * lax.rsqrt(ss * (1.0 / n_valid) + EPS) * gain


def _prep_even_kernel(p_ref, cos_ref, slo_ref, shi_ref, gains_ref, o_ref, *, segs, n_copy_from):
    cos, slo, shi = cos_ref[...], slo_ref[...], shi_ref[...]
    for (c0, c1, gi, scale) in segs:
        gain = gains_ref[gi:gi + 1, :]
        for c in range(c0, c1):
            x = p_ref[:, c * LANES:(c + 1) * LANES].astype(F32)
            y = _rope(_head_norm(x, gain, HEAD_DIM), cos, slo, shi, HEAD_DIM // 4)
            if scale != 1.0:
                y = y * scale
            o_ref[:, c * LANES:(c + 1) * LANES] = y.astype(BF16)
    for (c0, c1) in n_copy_from:
        o_ref[:, c0 * LANES:c1 * LANES] = p_ref[:, c0 * LANES:c1 * LANES]


def prep_heads(p, rope128, gains, segs, copies):
    st, n = p.shape
    tm = ROW_TILE
    tab = pl.BlockSpec((tm, LANES), lambda i: (i, 0))
    return pl.pallas_call(
        functools.partial(_prep_even_kernel, segs=segs, n_copy_from=copies),
        grid=(st // tm,),
        in_specs=[pl.BlockSpec((tm, n), lambda i: (i, 0)), tab, tab, tab,
                  pl.BlockSpec(gains.shape, lambda i: (0, 0))],
        out_specs=pl.BlockSpec((tm, n), lambda i: (i, 0)),
        out_shape=jax.ShapeDtypeStruct((st, n), BF16),
        compiler_params=_cparams("arbitrary"),
        name="prep_heads",
    )(p, *rope128, gains)


def _prep_odd1_kernel(p_ref, cos_ref, slo_ref, shi_ref, gains_ref, gqa_ref, gkva_ref,
                      qk_ref, qa_ref, kva_ref, *, n_q, n_k, c_k, c_qa, n_qa, c_kva, n_kva, q_scale):
    cos, slo, shi = cos_ref[...], slo_ref[...], shi_ref[...]
    for c in range(n_q):
        x = p_ref[:, c * LANES:(c + 1) * LANES].astype(F32)
        y = _rope(_head_norm(x, gains_ref[0:1, :], HEAD_DIM), cos, slo, shi, HEAD_DIM // 4) * q_scale
        qk_ref[:, c * LANES:(c + 1) * LANES] = y.astype(BF16)
    for c in range(n_k):
        x = p_ref[:, (c_k + c) * LANES:(c_k + c + 1) * LANES].astype(F32)
        y = _rope(_head_norm(x, gains_ref[1:2, :], HEAD_DIM), cos, slo, shi, HEAD_DIM // 4)
        qk_ref[:, (n_q + c) * LANES:(n_q + c + 1) * LANES] = y.astype(BF16)
    xa = p_ref[:, c_qa * LANES:(c_qa + n_qa) * LANES].astype(F32)
    qa_ref[...] = _head_norm(xa, gqa_ref[...], n_qa * LANES).astype(BF16)
    xk = p_ref[:, c_kva * LANES:(c_kva + n_kva) * LANES].astype(F32)
    kva_ref[...] = _head_norm(xk, gkva_ref[...], n_kva * LANES).astype(BF16)


def _prep_odd2_kernel(qd_ref, kv_ref, kr_ref, cos_ref, slo_ref, shi_ref, gq_ref, gk_ref,
                      qo_ref, ko_ref, *, n_heads, q_scale):
    cos, slo, shi = cos_ref[...], slo_ref[...], shi_ref[...]
    gq_n, gq_r = gq_ref[:, :LANES], gq_ref[:, LANES:]
    gk_n, gk_r = gk_ref[:, :LANES], gk_ref[:, LANES:]
    kr = kr_ref[...].astype(F32)
    kr_ss = jnp.sum(kr * kr, axis=-1, keepdims=True)
    for h in range(n_heads):
        b = h * D_PAD
        qn = qd_ref[:, b:b + LANES].astype(F32)
        qr = qd_ref[:, b + LANES:b + D_PAD].astype(F32)
        ss = jnp.sum(qn * qn, axis=-1, keepdims=True) + jnp.sum(qr * qr, axis=-1, keepdims=True)
        r = lax.rsqrt(ss * (1.0 / D_QK) + EPS) * q_scale
        qo_ref[:, b:b + LANES] = (qn * r * gq_n).astype(BF16)
        qo_ref[:, b + LANES:b + D_PAD] = _rope(qr * r * gq_r, cos, slo, shi, D_ROPE // 4).astype(BF16)
        kn = kv_ref[:, b:b + LANES].astype(F32)
        ss = jnp.sum(kn * kn, axis=-1, keepdims=True) + kr_ss
        r = lax.rsqrt(ss * (1.0 / D_QK) + EPS)
        ko_ref[:, b:b + LANES] = (kn * r * gk_n).astype(BF16)
        ko_ref[:, b + LANES:b + D_PAD] = _rope(kr * r * gk_r, cos, slo, shi, D_ROPE // 4).astype(BF16)


def _stack_heads(q_ref, g, d):
    return jnp.concatenate([q_ref[:, i * d:(i + 1) * d] for i in range(g)], axis=0)


def _qk(q, k):
    return lax.dot_general(q, k, (((1,), (1,)), ((), ())), preferred_element_type=F32)


def _window_attn_kernel(q_ref, kp_ref, km_ref, kn_ref, kc_ref, vp_ref, vm_ref, vn_ref, vc_ref, sink_ref,
                        o_ref, *, g, st):
    i = pl.program_id(1)
    tq = q_ref.shape[0]
    half = tq // 2
    qs = _stack_heads(q_ref, g, HEAD_DIM)
    k_loc = jnp.concatenate([kp_ref[...], km_ref[...], kn_ref[...]], axis=0)
    v_loc = jnp.concatenate([vp_ref[...], vm_ref[...], vn_ref[...]], axis=0)
    s_loc = _qk(qs, k_loc)
    s_ctx = _qk(qs, kc_ref[...])
    a = lax.broadcasted_iota(jnp.int32, s_loc.shape, 0) & (tq - 1)
    j = lax.broadcasted_iota(jnp.int32, s_loc.shape, 1)
    diff = j - half - a
    krow = i * tq - half + j
    valid = (jnp.abs(diff) <= half) & (krow >= tq) & (krow < st) & (i >= 1)
    s_loc = jnp.where(valid, s_loc, NEG_BIG)
    sink = jnp.concatenate([jnp.broadcast_to(sink_ref[0, h:h + 1, 0:1], (tq, 1)) for h in range(g)], axis=0)
    m = jnp.maximum(jnp.maximum(jnp.max(s_loc, axis=-1, keepdims=True),
                                jnp.max(s_ctx, axis=-1, keepdims=True)), sink)
    p_loc = jnp.exp2(s_loc - m)
    p_ctx = jnp.exp2(s_ctx - m)
    l = jnp.sum(p_loc, axis=-1, keepdims=True) + jnp.sum(p_ctx, axis=-1, keepdims=True) + jnp.exp2(sink - m)
    o = (jnp.dot(p_loc.astype(BF16), v_loc, preferred_element_type=F32)
         + jnp.dot(p_ctx.astype(BF16), vc_ref[...], preferred_element_type=F32)) / l
    for h in range(g):
        o_ref[:, h * HEAD_DIM:(h + 1) * HEAD_DIM] = o[h * tq:(h + 1) * tq].astype(BF16)


def window_attention(pn, sink, n_kv, g, c_q, c_k, c_v):
    st = pn.shape[0]
    tq = ROW_TILE
    half = tq // 2
    nh = st // half
    q_spec = pl.BlockSpec((tq, g * HEAD_DIM), lambda h, i: (i, c_q // g + h))

    def kv_specs(c0):
        return [
            pl.BlockSpec((half, HEAD_DIM), lambda h, i: (jnp.maximum(2 * i - 1, 0), c0 + h)),
            pl.BlockSpec((tq, HEAD_DIM), lambda h, i: (i, c0 + h)),
            pl.BlockSpec((half, HEAD_DIM), lambda h, i: (jnp.minimum(2 * i + 2, nh - 1), c0 + h)),
            pl.BlockSpec((tq, HEAD_DIM), lambda h, i: (0, c0 + h)),
        ]

    return pl.pallas_call(
        functools.partial(_window_attn_kernel, g=g, st=st),
        grid=(n_kv, st // tq),
        in_specs=[q_spec] + kv_specs(c_k) + kv_specs(c_v)
        + [pl.BlockSpec((1, g, LANES), lambda h, i: (h, 0, 0))],
        out_specs=pl.BlockSpec((tq, g * HEAD_DIM), lambda h, i: (i, h)),
        out_shape=jax.ShapeDtypeStruct((st, n_kv * g * HEAD_DIM), BF16),
        compiler_params=_cparams("arbitrary", "arbitrary"),
        name="window_attention",
    )(pn, pn, pn, pn, pn, pn, pn, pn, pn, sink)


def _flash_kernel(flag_ref, q_ref, k_ref, v_ref, aux_ref, o_ref, acc_ref, l_ref, accc_ref, lc_ref, *,
                  g, comps, dq, dv, tk, n_ctx, diff):
    qi = pl.program_id(1)
    tq = q_ref.shape[0]
    n_keys = k_ref.shape[0]
    n_lat = (n_keys - n_ctx) // tk
    ctx_only_tile = tq == n_ctx
    n_iter = jnp.where(qi == 0, 0, n_lat) if ctx_only_tile else n_lat
    n_stacked = g * comps
    qs = [jnp.concatenate([q_ref[:, (gi * comps + c) * dq:(gi * comps + c + 1) * dq] for gi in range(g)], axis=0)
          for c in range(comps)]

    def scores(start, size):
        return jnp.concatenate(
            [_qk(qs[c], k_ref[pl.ds(start, size), c * dq:(c + 1) * dq]) for c in range(comps)], axis=0)

    def chunk_start(j):
        return pl.multiple_of(n_ctx + j * tk, math.gcd(n_ctx, tk))

    def keep_context_rows(l_is_partial):
        if ctx_only_tile:
            return

        @pl.when(qi == 0)
        def _():
            for h in range(n_stacked):
                accc_ref[h * n_ctx:(h + 1) * n_ctx] = acc_ref[h * tq:h * tq + n_ctx]
                lrows = l_ref[h * tq:h * tq + n_ctx]
                if l_is_partial:
                    lrows = jnp.broadcast_to(jnp.sum(lrows, axis=-1, keepdims=True), lrows.shape)
                lc_ref[h * n_ctx:(h + 1) * n_ctx] = lrows

    @pl.when(flag_ref[0] == 1)
    def _bounded():
        def step(start, size, first):
            p = jnp.exp2(scores(start, size))
            psum = p[:, :LANES]
            for b in range(1, size // LANES):
                psum = psum + p[:, b * LANES:(b + 1) * LANES]
            pv = jnp.dot(p.astype(BF16), v_ref[pl.ds(start, size), :], preferred_element_type=F32)
            if first:
                acc_ref[...] = pv
                l_ref[...] = psum
            else:
                acc_ref[...] += pv
                l_ref[...] += psum

        step(0, n_ctx, True)
        keep_context_rows(True)

        def body(j, carry):
            step(chunk_start(j), tk, False)
            return carry

        lax.fori_loop(0, n_iter, body, 0)
        l_ref[...] = jnp.broadcast_to(jnp.sum(l_ref[...], axis=-1, keepdims=True), l_ref.shape)

    @pl.when(flag_ref[0] != 1)
    def _online():
        def step(start, size, m, l, first):
            s = scores(start, size)
            m_cur = jnp.max(s, axis=-1, keepdims=True)
            m_new = m_cur if first else jnp.maximum(m, m_cur)
            p = jnp.exp2(s - m_new)
            pv = jnp.dot(p.astype(BF16), v_ref[pl.ds(start, size), :], preferred_element_type=F32)
            if first:
                l_new = jnp.sum(p, axis=-1, keepdims=True)
                acc_ref[...] = pv
            else:
                alpha = jnp.exp2(m - m_new)
                l_new = alpha * l + jnp.sum(p, axis=-1, keepdims=True)
                acc_ref[...] = alpha * acc_ref[...] + pv
            return m_new, l_new

        m, l = step(0, n_ctx, None, None, True)
        l_ref[...] = jnp.broadcast_to(l, l_ref.shape)
        keep_context_rows(False)
        m, l = lax.fori_loop(0, n_iter, lambda j, c: step(chunk_start(j), tk, c[0], c[1], False), (m, l))
        l_ref[...] = jnp.broadcast_to(l, l_ref.shape)

    def finalize(acc, l, rows):
        o = acc / l
        if diff:
            lam = aux_ref[0:1, 0:1]
            for gi in range(g):
                y = o[gi * rows:(gi + 1) * rows] - lam * o[(g + gi) * rows:(g + gi + 1) * rows]
                ss = jnp.mean(y * y, axis=-1, keepdims=True)
                y = y * lax.rsqrt(ss + EPS) * aux_ref[1:2, :]
                o_ref[0:rows, gi * dv:(gi + 1) * dv] = y.astype(BF16)
        else:
            for gi in range(g):
                o_ref[0:rows, gi * dv:(gi + 1) * dv] = o[gi * rows:(gi + 1) * rows].astype(BF16)

    finalize(acc_ref[...], l_ref[:, 0:1], tq)
    if not ctx_only_tile:
        @pl.when(qi == 0)
        def _():
            finalize(accc_ref[...], lc_ref[:, 0:1], n_ctx)


def _score_bound_flag(q_gain, k_gain, n_norm, scale):
    bound = n_norm * jnp.max(jnp.abs(q_gain)) * jnp.max(jnp.abs(k_gain)) * scale * BF16_NORM_SLACK
    return (bound <= SCORE_BOUND).astype(jnp.int32).reshape(1)


def flash_attention(q_arr, k_arr, v_arr, aux, flag, *, n_kv, g, comps, dq, dv, c_q, c_k, c_v, tk, tq=ROW_TILE,
                    diff=False):
    st = q_arr.shape[0]
    n_ctx = ROW_TILE
    assert st % tq == 0 and tq % n_ctx == 0
    while (st - n_ctx) % tk:
        tk //= 2
    rows = g * comps * tq
    ctx_rows = g * comps * n_ctx if tq != n_ctx else 8
    return pl.pallas_call(
        functools.partial(_flash_kernel, g=g, comps=comps, dq=dq, dv=dv, tk=tk, n_ctx=n_ctx, diff=diff),
        grid=(n_kv, st // tq),
        in_specs=[
            pl.BlockSpec(memory_space=pltpu.SMEM),
            pl.BlockSpec((tq, g * comps * dq), lambda h, i: (i, c_q + h)),
            pl.BlockSpec((st, comps * dq), lambda h, i: (0, c_k + h)),
            pl.BlockSpec((st, dv), lambda h, i: (0, c_v(h))),
            pl.BlockSpec(aux.shape, lambda h, i: (0, 0)),
        ],
        out_specs=pl.BlockSpec((tq, g * dv), lambda h, i: (i, h)),
        out_shape=jax.ShapeDtypeStruct((st, n_kv * g * dv), BF16),
        scratch_shapes=[pltpu.VMEM((rows, dv), F32), pltpu.VMEM((rows, LANES), F32),
                        pltpu.VMEM((ctx_rows, dv), F32), pltpu.VMEM((ctx_rows, LANES), F32)],
        compiler_params=_cparams("arbitrary", "arbitrary"),
        name="flash_attention",
    )(flag, q_arr, k_arr, v_arr, aux)


MOE_TILE = 256


GATHER_WINDOW = 128


def _gather_kernel(idx_ref, src_ref, o_ref, sem):
    m = o_ref.shape[0]
    window = min(GATHER_WINDOW, m)

    def row_copy(r, src_row):
        return pltpu.make_async_copy(src_ref.at[pl.ds(src_row, 1)], o_ref.at[pl.ds(r, 1)], sem)

    def issue(r, carry):
        row_copy(r, idx_ref[r]).start()
        return carry

    def retire_and_issue(r, carry):
        row_copy(0, 0).wait()
        row_copy(r, idx_ref[r]).start()
        return carry

    def retire(r, carry):
        row_copy(0, 0).wait()
        return carry

    lax.fori_loop(0, window, issue, 0)
    lax.fori_loop(window, m, retire_and_issue, 0)
    lax.fori_loop(0, window, retire, 0)


def row_gather(src, idx):
    m = idx.shape[0]
    d = src.shape[1]
    return pl.pallas_call(
        _gather_kernel,
        grid_spec=pltpu.PrefetchScalarGridSpec(
            num_scalar_prefetch=1,
            grid=(1,),
            in_specs=[pl.BlockSpec(memory_space=pl.ANY)],
            out_specs=pl.BlockSpec(memory_space=pl.ANY),
            scratch_shapes=[pltpu.SemaphoreType.DMA(())],
        ),
        out_shape=jax.ShapeDtypeStruct((m, d), src.dtype),
        compiler_params=_cparams("arbitrary"),
        name="row_gather",
    )(idx, src)


def _grouped_moe_kernel(te_ref, nt_ref, x_ref, wgu_ref, bgu_ref, wd_ref, bd_ref, o_ref, *, ff):
    t = pl.program_id(0)

    @pl.when(t < nt_ref[0])
    def _():
        gu = jnp.dot(x_ref[...].astype(BF16), wgu_ref[0], preferred_element_type=F32) + bgu_ref[0]
        glu = jnp.minimum(gu[:, :ff], SWIGLU_LIMIT)
        lin = jnp.clip(gu[:, ff:], -SWIGLU_LIMIT, SWIGLU_LIMIT)
        act = glu * (1.0 / (1.0 + jnp.exp(-SWIGLU_ALPHA * glu))) * (lin + 1.0)
        o_ref[...] = jnp.dot(act.astype(BF16), wd_ref[0], preferred_element_type=F32) + bd_ref[0]

    @pl.when(t >= nt_ref[0])
    def _():
        o_ref[...] = jnp.zeros_like(o_ref)


def grouped_moe(xs, tile_expert, n_tiles_used, w_gu, b_gu, w_down, b_down):
    p, d = xs.shape
    n_exp, _, ff2 = w_gu.shape
    ff = ff2 // 2
    tg = MOE_TILE
    return pl.pallas_call(
        functools.partial(_grouped_moe_kernel, ff=ff),
        grid_spec=pltpu.PrefetchScalarGridSpec(
            num_scalar_prefetch=2,
            grid=(p // tg,),
            in_specs=[
                pl.BlockSpec((tg, d), lambda t, te, nt: (t, 0)),
                pl.BlockSpec((1, d, ff2), lambda t, te, nt: (te[t], 0, 0)),
                pl.BlockSpec((1, 1, ff2), lambda t, te, nt: (te[t], 0, 0)),
                pl.BlockSpec((1, ff, d), lambda t, te, nt: (te[t], 0, 0)),
                pl.BlockSpec((1, 1, d), lambda t, te, nt: (te[t], 0, 0)),
            ],
            out_specs=pl.BlockSpec((tg, d), lambda t, te, nt: (t, 0)),
        ),
        out_shape=jax.ShapeDtypeStruct((p, d), F32),
        compiler_params=_cparams("arbitrary"),
        name="grouped_moe",
    )(tile_expert, n_tiles_used, xs, w_gu, b_gu.reshape(n_exp, 1, ff2), w_down, b_down.reshape(n_exp, 1, d))


def _combine_kernel(route_ref, *refs):
    o_ref = refs[-1]
    route = route_ref[...]
    acc = None
    for k in range(TOP_K):
        term = route[:, TOP_K + k:TOP_K + k + 1] * refs[k][...]
        acc = term if acc is None else acc + term
    o_ref[...] = acc.astype(o_ref.dtype)


def moe_combine(route, ys4):
    st = route.shape[0]
    d = ys4.shape[1]
    tm = ROW_TILE
    nt = st // tm
    return pl.pallas_call(
        _combine_kernel,
        grid=(nt,),
        in_specs=[pl.BlockSpec((tm, LANES), lambda i: (i, 0))]
        + [pl.BlockSpec((tm, d), functools.partial(lambda i, k: (k * nt + i, 0), k=k)) for k in range(TOP_K)],
        out_specs=pl.BlockSpec((tm, d), lambda i: (i, 0)),
        out_shape=jax.ShapeDtypeStruct((st, d), BF16),
        compiler_params=_cparams("arbitrary"),
        name="moe_combine",
    )(route, *([ys4] * TOP_K))


def moe_sparse(h, route, w_gu, b_gu, w_down, b_down):
    st = h.shape[0]
    n_exp = w_gu.shape[0]
    tg = MOE_TILE
    ids = route[:, :TOP_K].astype(jnp.int32)
    onehot = (ids[:, :, None] == jnp.arange(n_exp)[None, None, :]).any(axis=1)
    counts = jnp.sum(onehot, axis=0, dtype=jnp.int32)
    tiles = (counts + tg - 1) // tg
    tile_end = jnp.cumsum(tiles)
    row_off = (tile_end - tiles) * tg
    rank = jnp.cumsum(onehot.astype(jnp.int32), axis=0) - 1
    pos = row_off[ids] + jnp.take_along_axis(rank, ids, axis=1)
    n_tiles = (TOP_K * st + n_exp * (tg - 1)) // tg
    tok = jnp.zeros((n_tiles * tg,), jnp.int32).at[pos.reshape(-1)].set(
        jnp.repeat(jnp.arange(st, dtype=jnp.int32), TOP_K), unique_indices=True)
    tile_expert = jnp.minimum(jnp.searchsorted(tile_end, jnp.arange(n_tiles), side="right"),
                              n_exp - 1).astype(jnp.int32)
    xs = row_gather(h, tok)
    ys = grouped_moe(xs, tile_expert, tile_end[-1:].astype(jnp.int32), w_gu, b_gu, w_down, b_down)
    ys4 = row_gather(ys, pos.T.reshape(-1).astype(jnp.int32))
    return moe_combine(route, ys4)


def _rope_tables(n_ctx, n_lat, dim):
    quarter = dim // 4
    inv_freq = ROPE_THETA ** (-jnp.arange(quarter, dtype=F32) / quarter)
    t = jnp.arange(n_lat)
    row = (t // GRID_W).astype(F32)
    col = (t % GRID_W).astype(F32)
    ang = jnp.stack([row[:, None] * inv_freq, col[:, None] * inv_freq], axis=1)
    ang = jnp.broadcast_to(ang[:, :, None, :], (n_lat, 2, 2, quarter)).reshape(n_lat, dim)
    cos, sin = jnp.cos(ang), jnp.sin(ang)
    first_half = (jnp.arange(dim) % (2 * quarter)) < quarter
    sin_lo = jnp.where(first_half, -sin, 0.0)
    sin_hi = jnp.where(first_half, 0.0, sin)

    def full(tab, fill):
        tab = jnp.pad(tab, ((0, 0), (0, LANES - dim)), constant_values=fill)
        return jnp.concatenate([jnp.full((n_ctx, LANES), fill, F32), tab], axis=0)

    return full(cos, 1.0), full(sin_lo, 0.0), full(sin_hi, 0.0)


def _pad_cols(w, n):
    return jnp.pad(w, ((0, 0), (0, n - w.shape[1])))


def _streams(v_ctx, v_lat):
    return jnp.stack([v_ctx, v_lat], axis=0)[:, None, :]


def _even_mixer(h, rope128, w_in, w_out, a_qn, a_kn, a_sink, b_qn, b_kn, b_lam, b_subln, lam_init, d):
    a_heads = d // (2 * HEAD_DIM)
    a_kv = a_heads // 4
    b_heads = d // (4 * HEAD_DIM)
    b_kv = b_heads // 2
    ga, gb = a_heads // a_kv, b_heads // b_kv
    scale = HEAD_DIM ** -0.5 * LOG2E
    n_qa, n_qb, n_ka, n_va, n_kb, n_vb = a_heads, 2 * b_heads, a_kv, a_kv, 2 * b_kv, 2 * b_kv
    c_qb = n_qa
    c_ka = c_qb + n_qb
    c_va = c_ka + n_ka
    c_kb = c_va + n_va
    c_vb = c_kb + n_kb
    p = matmul([h], [w_in.astype(BF16)], tn=512)
    gains = jnp.stack([a_qn, b_qn, a_kn, b_kn], axis=0).astype(F32)
    pn = prep_heads(p, rope128, gains,
                    segs=((0, c_qb, 0, scale), (c_qb, c_ka, 1, scale), (c_ka, c_va, 2, 1.0), (c_kb, c_vb, 3, 1.0)),
                    copies=((c_va, c_kb), (c_vb, c_vb + n_vb)))
    sink = jnp.broadcast_to((a_sink.astype(F32) * LOG2E).reshape(a_kv, ga, 1), (a_kv, ga, LANES))
    ya = window_attention(pn, sink, a_kv, ga, 0, c_ka, c_va)
    lf = b_lam.astype(F32)
    lam = jnp.exp(jnp.sum(lf[0] * lf[1])) - jnp.exp(jnp.sum(lf[2] * lf[3])) + lam_init
    dvb = 2 * HEAD_DIM
    aux = jnp.stack([jnp.full((dvb,), lam, F32), b_subln.astype(F32) * (1.0 - lam_init)], axis=0)
    flag = _score_bound_flag(b_qn, b_kn, HEAD_DIM, scale)
    yb = flash_attention(pn, pn, pn, aux, flag, n_kv=b_kv, g=gb, comps=2, dq=HEAD_DIM, dv=dvb,
                         c_q=c_qb * HEAD_DIM // (gb * 2 * HEAD_DIM), c_k=c_kb // 2, c_v=lambda hh: c_vb // 2 + hh,
                         tk=1024, diff=True)
    w_out = w_out.astype(BF16)
    na = a_heads * HEAD_DIM
    return matmul([ya, yb], [w_out[:na], w_out[na:]], tn=512)


def _odd_mixer(h, rope128, rope64, w_in, w_out, c_qn, c_kn, d_qa_norm, d_kva_norm, d_wq_up, d_wkv_up, d_qn, d_kn, d):
    st = h.shape[0]
    c_heads = d // (2 * HEAD_DIM)
    c_kv = c_heads // 4
    gc = c_heads // c_kv
    d_heads = d // (2 * HEAD_DIM)
    q_rank = d_wq_up.shape[0]
    kv_rank = d_wkv_up.shape[0]
    n_qc, n_qa, n_kc, n_vc, n_kva = c_heads, q_rank // LANES, c_kv, c_kv, kv_rank // LANES
    c_qa = n_qc
    c_kc = c_qa + n_qa
    c_vc = c_kc + n_kc
    c_kva = c_vc + n_vc
    c_kr = c_kva + n_kva
    n_in = (c_kr + 1) * LANES
    tn = 768
    n_pad = -(-n_in // tn) * tn
    p = matmul([h], [_pad_cols(w_in, n_pad).astype(BF16)], tn=tn)
    tm = ROW_TILE
    tab = pl.BlockSpec((tm, LANES), lambda i: (i, 0))
    gains = jnp.stack([c_qn, c_kn], axis=0).astype(F32)
    qk, qa, kva = pl.pallas_call(
        functools.partial(_prep_odd1_kernel, n_q=n_qc, n_k=n_kc, c_k=c_kc, c_qa=c_qa, n_qa=n_qa, c_kva=c_kva,
                          n_kva=n_kva, q_scale=HEAD_DIM ** -0.5 * LOG2E),
        grid=(st // tm,),
        in_specs=[pl.BlockSpec((tm, n_pad), lambda i: (i, 0)), tab, tab, tab,
                  pl.BlockSpec((2, LANES), lambda i: (0, 0)),
                  pl.BlockSpec((1, q_rank), lambda i: (0, 0)),
                  pl.BlockSpec((1, kv_rank), lambda i: (0, 0))],
        out_specs=[pl.BlockSpec((tm, (n_qc + n_kc) * LANES), lambda i: (i, 0)),
                   pl.BlockSpec((tm, q_rank), lambda i: (i, 0)),
                   pl.BlockSpec((tm, kv_rank), lambda i: (i, 0))],
        out_shape=[jax.ShapeDtypeStruct((st, (n_qc + n_kc) * LANES), BF16),
                   jax.ShapeDtypeStruct((st, q_rank), BF16),
                   jax.ShapeDtypeStruct((st, kv_rank), BF16)],
        compiler_params=_cparams("arbitrary"),
        name="prep_odd1",
    )(p, *rope128, gains, d_qa_norm.astype(F32)[None, :], d_kva_norm.astype(F32)[None, :])
    wq = jnp.pad(d_wq_up.reshape(q_rank, d_heads, D_QK), ((0, 0), (0, 0), (0, D_PAD - D_QK)))
    qd_raw = matmul([qa], [wq.reshape(q_rank, d_heads * D_PAD).astype(BF16)], tn=512)
    kv = matmul([kva], [d_wkv_up.astype(BF16)], tn=512)
    pad_gain = lambda gvec: jnp.pad(gvec.astype(F32), (0, D_PAD - D_QK))[None, :]
    qd, kd = pl.pallas_call(
        functools.partial(_prep_odd2_kernel, n_heads=d_heads, q_scale=D_QK ** -0.5 * LOG2E),
        grid=(st // tm,),
        in_specs=[pl.BlockSpec((tm, d_heads * D_PAD), lambda i: (i, 0)),
                  pl.BlockSpec((tm, d_heads * D_PAD), lambda i: (i, 0)),
                  pl.BlockSpec((tm, LANES), lambda i: (i, c_kr)),
                  tab, tab, tab,
                  pl.BlockSpec((1, D_PAD), lambda i: (0, 0)),
                  pl.BlockSpec((1, D_PAD), lambda i: (0, 0))],
        out_specs=[pl.BlockSpec((tm, d_heads * D_PAD), lambda i: (i, 0)),
                   pl.BlockSpec((tm, d_heads * D_PAD), lambda i: (i, 0))],
        out_shape=[jax.ShapeDtypeStruct((st, d_heads * D_PAD), BF16),
                   jax.ShapeDtypeStruct((st, d_heads * D_PAD), BF16)],
        compiler_params=_cparams("arbitrary"),
        name="prep_odd2",
    )(qd_raw, kv, p, *rope64, pad_gain(d_qn), pad_gain(d_kn))
    aux = jnp.zeros((8, LANES), F32)
    flag_c = _score_bound_flag(c_qn, c_kn, HEAD_DIM, HEAD_DIM ** -0.5 * LOG2E)
    flag_d = _score_bound_flag(d_qn, d_kn, D_QK, D_QK ** -0.5 * LOG2E)
    yc = flash_attention(qk, qk, p, aux, flag_c, n_kv=c_kv, g=gc, comps=1, dq=HEAD_DIM, dv=HEAD_DIM,
                         c_q=0, c_k=n_qc, c_v=lambda hh: c_vc + hh, tk=1024)
    yd = flash_attention(qd, kd, kv, aux, flag_d, n_kv=d_heads, g=1, comps=1, dq=D_PAD, dv=D_V,
                         c_q=0, c_k=0, c_v=lambda hh: 2 * hh + 1, tk=1024,
                         tq=1280 if st % 1280 == 0 else ROW_TILE)
    w_out = w_out.astype(BF16)
    nc = c_heads * HEAD_DIM
    return matmul([yc, yd], [w_out[:nc], w_out[nc:]], tn=512)


def kernel(x, c, ctx, c_ctx, adaln_down, adaln_up, adaln_b, norm_mix, norm_ffn, ev_w_in, ev_w_out, ev_a_qn, ev_a_kn,
           ev_a_sink, ev_b_qn, ev_b_kn, ev_b_lam, ev_b_subln, od_w_in, od_w_out, od_c_qn, od_c_kn, od_d_qa_norm,
           od_d_kva_norm, od_d_wq_up, od_d_wkv_up, od_d_qn, od_d_kn, router_w, router_b, moe_w_gu, moe_b_gu,
           moe_w_down, moe_b_down):
    bsz, seq, d = x.shape
    n_ctx = ctx.shape[1]
    depth = adaln_down.shape[0]
    n_exp = router_w.shape[2]
    assert bsz == 1 and n_ctx == ROW_TILE and seq % ROW_TILE == 0 and n_exp <= LANES

    rope128 = _rope_tables(n_ctx, seq, HEAD_DIM)
    rope64 = _rope_tables(n_ctx, seq, D_ROPE)
    cvecs = jnp.zeros((8, d), F32).at[0].set(c[0]).at[1].set(c_ctx)
    mods = adaln_all(cvecs, adaln_down, adaln_up, adaln_b)

    xs = jnp.concatenate([ctx[0], x[0]], axis=0)
    y_prev, gate_prev = None, None
    for l in range(depth):
        m_lat = mods[l, 0].reshape(N_MOD, d)
        m_ctx = mods[l, 1].reshape(N_MOD, d)
        mult = _streams(norm_mix[l] * (1.0 + m_ctx[1]), norm_mix[l] * (1.0 + m_lat[1]))
        shift = _streams(m_ctx[0], m_lat[0])
        if y_prev is None:
            (h,) = modulate(xs, mult, shift)
        else:
            xs, h = modulate(xs, mult, shift, res=(y_prev, gate_prev))
        j = l // 2
        if l % 2 == 0:
            lam_init = 0.8 - 0.6 * math.exp(-0.3 * l)
            y = _even_mixer(h, rope128, ev_w_in[j], ev_w_out[j], ev_a_qn[j], ev_a_kn[j], ev_a_sink[j], ev_b_qn[j],
                            ev_b_kn[j], ev_b_lam[j], ev_b_subln[j], lam_init, d)
        else:
            y = _odd_mixer(h, rope128, rope64, od_w_in[j], od_w_out[j], od_c_qn[j], od_c_kn[j], od_d_qa_norm[j],
                           od_d_kva_norm[j], od_d_wq_up[j], od_d_wkv_up[j], od_d_qn[j], od_d_kn[j], d)
        mult = _streams(norm_ffn[l] * (1.0 + m_ctx[4]), norm_ffn[l] * (1.0 + m_lat[4]))
        shift = _streams(m_ctx[3], m_lat[3])
        rw = _pad_cols(router_w[l].astype(F32), LANES)
        rb = jnp.pad(router_b[l].astype(F32), (0, LANES - n_exp))[None, :]
        xs, h2, route = modulate(xs, mult, shift, res=(y, _streams(m_ctx[2], m_lat[2])), router=(rw, rb, n_exp))
        y_prev = moe_sparse(h2, route, moe_w_gu[l].astype(BF16), moe_b_gu[l].astype(F32),
                            moe_w_down[l].astype(BF16), moe_b_down[l].astype(F32))
        gate_prev = _streams(m_ctx[5], m_lat[5])
    out = final_residual(xs, y_prev, gate_prev)
    return out[None]
```

```python
import functools
import math

import jax
import jax.numpy as jnp
from jax import lax
from jax.experimental import pallas as pl
from jax.experimental.pallas import tpu as pltpu

F32 = jnp.float32
BF16 = jnp.bfloat16

GRID_W = 64
HEAD_DIM = 128
ROPE_THETA = 10000.0
EPS = 1e-6
N_MOD = 6
D_NOPE = 128
D_ROPE = 64
D_QK = D_NOPE + D_ROPE
D_V = 128
D_PAD = 256
TOP_K = 4
SWIGLU_ALPHA = 1.702
SWIGLU_LIMIT = 7.0
LOG2E = 1.4426950408889634
NEG_BIG = -1e30
SCORE_BOUND = 60.0
BF16_NORM_SLACK = 1.02

LANES = 128
ROW_TILE = 256
VMEM_LIMIT = 56 * 1024 * 1024


def _cparams(*sem):
    return pltpu.CompilerParams(dimension_semantics=sem, vmem_limit_bytes=VMEM_LIMIT)


def _adaln_kernel(c_ref, down_ref, up_ref, b_ref, o_ref, t_ref):
    @pl.when(pl.program_id(1) == 0)
    def _():
        c = c_ref[...]
        a = c * (1.0 / (1.0 + jnp.exp(-c)))
        t_ref[...] = jnp.dot(a, down_ref[0], preferred_element_type=F32, precision=lax.Precision.HIGHEST)

    o_ref[0] = jnp.dot(t_ref[...], up_ref[0], preferred_element_type=F32,
                       precision=lax.Precision.HIGHEST) + b_ref[0]


def adaln_all(cvecs, down, up, bias):
    depth, d, rank = down.shape
    n = up.shape[2]
    tn = 2048
    assert n % tn == 0
    return pl.pallas_call(
        _adaln_kernel,
        grid=(depth, n // tn),
        in_specs=[
            pl.BlockSpec((8, d), lambda l, j: (0, 0)),
            pl.BlockSpec((1, d, rank), lambda l, j: (l, 0, 0)),
            pl.BlockSpec((1, rank, tn), lambda l, j: (l, 0, j)),
            pl.BlockSpec((1, 1, tn), lambda l, j: (l, 0, j)),
        ],
        out_specs=pl.BlockSpec((1, 8, tn), lambda l, j: (l, 0, j)),
        out_shape=jax.ShapeDtypeStruct((depth, 8, n), F32),
        scratch_shapes=[pltpu.VMEM((8, rank), F32)],
        compiler_params=_cparams("arbitrary", "arbitrary"),
        name="adaln",
    )(cvecs, down, up, bias.reshape(depth, 1, n))


def _topk_route(logits, n_exp):
    lane = lax.broadcasted_iota(jnp.int32, logits.shape, 1).astype(F32)
    work = jnp.where(lane < n_exp, logits, -jnp.inf)
    ids = jnp.zeros_like(logits)
    wts = jnp.zeros_like(logits)
    denom = None
    v0 = None
    for k in range(TOP_K):
        m = jnp.max(work, axis=-1, keepdims=True)
        idx = jnp.min(jnp.where(work == m, lane, float(LANES)), axis=-1, keepdims=True)
        if k == 0:
            v0 = m
            e = jnp.ones_like(m)
            denom = e
        else:
            e = jnp.exp(m - v0)
            denom = denom + e
        ids = jnp.where(lane == k, idx, ids)
        wts = jnp.where(lane == TOP_K + k, e, wts)
        work = jnp.where(lane == idx, -jnp.inf, work)
    return ids + wts / denom


def _modulate_kernel(*refs, has_res, n_exp):
    it = iter(refs)
    x_ref = next(it)
    y_ref = next(it) if has_res else None
    gate_ref = next(it) if has_res else None
    mult_ref = next(it)
    shift_ref = next(it)
    rw_ref = next(it) if n_exp else None
    rb_ref = next(it) if n_exp else None
    xo_ref = next(it) if has_res else None
    h_ref = next(it)
    g_ref = next(it) if n_exp else None

    x = x_ref[...]
    if has_res:
        x = x + gate_ref[0] * y_ref[...].astype(F32)
        xo_ref[...] = x
    ms = jnp.mean(x * x, axis=-1, keepdims=True)
    h = x * lax.rsqrt(ms + EPS) * mult_ref[0] + shift_ref[0]
    h_ref[...] = h.astype(h_ref.dtype)
    if n_exp:
        logits = jnp.dot(h, rw_ref[...], preferred_element_type=F32,
                         precision=lax.Precision.HIGHEST) + rb_ref[...]
        g_ref[...] = _topk_route(logits, n_exp)


def modulate(x, mult, shift, res=None, router=None):
    st, d = x.shape
    tm = ROW_TILE
    stream = lambda i: (jnp.minimum(i, 1), 0, 0)
    row = lambda i: (i, 0)
    vec_spec = pl.BlockSpec((1, 1, d), stream)
    args, in_specs = [x], [pl.BlockSpec((tm, d), row)]
    if res is not None:
        args += [res[0], res[1]]
        in_specs += [pl.BlockSpec((tm, d), row), vec_spec]
    args += [mult, shift]
    in_specs += [vec_spec, vec_spec]
    n_exp = 0
    if router is not None:
        args += [router[0], router[1]]
        n_exp = router[2]
        in_specs += [pl.BlockSpec((d, LANES), lambda i: (0, 0)), pl.BlockSpec((1, LANES), lambda i: (0, 0))]
    out_shape, out_specs = [], []
    if res is not None:
        out_shape.append(jax.ShapeDtypeStruct((st, d), F32))
        out_specs.append(pl.BlockSpec((tm, d), row))
    out_shape.append(jax.ShapeDtypeStruct((st, d), BF16 if router is None else F32))
    out_specs.append(pl.BlockSpec((tm, d), row))
    if router is not None:
        out_shape.append(jax.ShapeDtypeStruct((st, LANES), F32))
        out_specs.append(pl.BlockSpec((tm, LANES), row))
    return pl.pallas_call(
        functools.partial(_modulate_kernel, has_res=res is not None, n_exp=n_exp),
        grid=(st // tm,),
        in_specs=in_specs,
        out_specs=out_specs,
        out_shape=out_shape,
        compiler_params=_cparams("arbitrary"),
        name="modulate",
    )(*args)


def _final_residual_kernel(x_ref, y_ref, gate_ref, o_ref):
    o_ref[...] = x_ref[...] + gate_ref[0] * y_ref[...].astype(F32)


def final_residual(x, y, gate):
    st, d = x.shape
    tm = ROW_TILE
    n = st // tm - 1
    return pl.pallas_call(
        _final_residual_kernel,
        grid=(n,),
        in_specs=[
            pl.BlockSpec((tm, d), lambda i: (i + 1, 0)),
            pl.BlockSpec((tm, d), lambda i: (i + 1, 0)),
            pl.BlockSpec((1, 1, d), lambda i: (1, 0, 0)),
        ],
        out_specs=pl.BlockSpec((tm, d), lambda i: (i, 0)),
        out_shape=jax.ShapeDtypeStruct((n * tm, d), F32),
        compiler_params=_cparams("arbitrary"),
        name="final_residual",
    )(x, y, gate)


def _mm_kernel(*refs, n_pairs):
    o_ref = refs[-1]
    acc = None
    for p in range(n_pairs):
        t = jnp.dot(refs[p][...], refs[n_pairs + p][...], preferred_element_type=F32)
        acc = t if acc is None else acc + t
    o_ref[...] = acc.astype(o_ref.dtype)


def _row_block(m):
    for tm in (1280, 1024, 512, 256):
        if m % tm == 0:
            return tm
    raise ValueError(m)


def matmul(a_list, b_list, tn, out_dtype=BF16):
    m = a_list[0].shape[0]
    n = b_list[0].shape[1]
    tm = _row_block(m)
    while n % tn:
        tn -= LANES
    in_specs = [pl.BlockSpec((tm, a.shape[1]), lambda i, j: (i, 0)) for a in a_list]
    in_specs += [pl.BlockSpec((b.shape[0], tn), lambda i, j: (0, j)) for b in b_list]
    return pl.pallas_call(
        functools.partial(_mm_kernel, n_pairs=len(a_list)),
        grid=(m // tm, n // tn),
        in_specs=in_specs,
        out_specs=pl.BlockSpec((tm, tn), lambda i, j: (i, j)),
        out_shape=jax.ShapeDtypeStruct((m, n), out_dtype),
        compiler_params=_cparams("arbitrary", "arbitrary"),
        name="matmul",
    )(*a_list, *b_list)


def _rope(y, cos, sin_lo, sin_hi, quarter):
    left = pltpu.roll(y, LANES - quarter, 1)
    right = pltpu.roll(y, quarter, 1)
    return y * cos + left * sin_lo + right * sin_hi


def _head_norm(x, gain, n_valid):
    ss = jnp.sum(x * x, axis=-1, keepdims=True)
    return x * lax.rsqrt(ss * (1.0 / n_valid) + EPS) * gain


def _prep_even_kernel(p_ref, cos_ref, slo_ref, shi_ref, gains_ref, o_ref, *, segs, n_copy_from):
    cos, slo, shi = cos_ref[...], slo_ref[...], shi_ref[...]
    for (c0, c1, gi, scale) in segs:
        gain = gains_ref[gi:gi + 1, :]
        for c in range(c0, c1):
            x = p_ref[:, c * LANES:(c + 1) * LANES].astype(F32)
            y = _rope(_head_norm(x, gain, HEAD_DIM), cos, slo, shi, HEAD_DIM // 4)
            if scale != 1.0:
                y = y * scale
            o_ref[:, c * LANES:(c + 1) * LANES] = y.astype(BF16)
    for (c0, c1) in n_copy_from:
        o_ref[:, c0 * LANES:c1 * LANES] = p_ref[:, c0 * LANES:c1 * LANES]


def prep_heads(p, rope128, gains, segs, copies):
    st, n = p.shape
    tm = ROW_TILE
    tab = pl.BlockSpec((tm, LANES), lambda i: (i, 0))
    return pl.pallas_call(
        functools.partial(_prep_even_kernel, segs=segs, n_copy_from=copies),
        grid=(st // tm,),
        in_specs=[pl.BlockSpec((tm, n), lambda i: (i, 0)), tab, tab, tab,
                  pl.BlockSpec(gains.shape, lambda i: (0, 0))],
        out_specs=pl.BlockSpec((tm, n), lambda i: (i, 0)),
        out_shape=jax.ShapeDtypeStruct((st, n), BF16),
        compiler_params=_cparams("arbitrary"),
        name="prep_heads",
    )(p, *rope128, gains)


def _prep_odd1_kernel(p_ref, cos_ref, slo_ref, shi_ref, gains_ref, gqa_ref, gkva_ref,
                      qk_ref, qa_ref, kva_ref, *, n_q, n_k, c_k, c_qa, n_qa, c_kva, n_kva, q_scale):
    cos, slo, shi = cos_ref[...], slo_ref[...], shi_ref[...]
    for c in range(n_q):
        x = p_ref[:, c * LANES:(c + 1) * LANES].astype(F32)
        y = _rope(_head_norm(x, gains_ref[0:1, :], HEAD_DIM), cos, slo, shi, HEAD_DIM // 4) * q_scale
        qk_ref[:, c * LANES:(c + 1) * LANES] = y.astype(BF16)
    for c in range(n_k):
        x = p_ref[:, (c_k + c) * LANES:(c_k + c + 1) * LANES].astype(F32)
        y = _rope(_head_norm(x, gains_ref[1:2, :], HEAD_DIM), cos, slo, shi, HEAD_DIM // 4)
        qk_ref[:, (n_q + c) * LANES:(n_q + c + 1) * LANES] = y.astype(BF16)
    xa = p_ref[:, c_qa * LANES:(c_qa + n_qa) * LANES].astype(F32)
    qa_ref[...] = _head_norm(xa, gqa_ref[...], n_qa * LANES).astype(BF16)
    xk = p_ref[:, c_kva * LANES:(c_kva + n_kva) * LANES].astype(F32)
    kva_ref[...] = _head_norm(xk, gkva_ref[...], n_kva * LANES).astype(BF16)


def _prep_odd2_kernel(qd_ref, kv_ref, kr_ref, cos_ref, slo_ref, shi_ref, gq_ref, gk_ref,
                      qo_ref, ko_ref, *, n_heads, q_scale):
    cos, slo, shi = cos_ref[...], slo_ref[...], shi_ref[...]
    gq_n, gq_r = gq_ref[:, :LANES], gq_ref[:, LANES:]
    gk_n, gk_r = gk_ref[:, :LANES], gk_ref[:, LANES:]
    kr = kr_ref[...].astype(F32)
    kr_ss = jnp.sum(kr * kr, axis=-1, keepdims=True)
    for h in range(n_heads):
        b = h * D_PAD
        qn = qd_ref[:, b:b + LANES].astype(F32)
        qr = qd_ref[:, b + LANES:b + D_PAD].astype(F32)
        ss = jnp.sum(qn * qn, axis=-1, keepdims=True) + jnp.sum(qr * qr, axis=-1, keepdims=True)
        r = lax.rsqrt(ss * (1.0 / D_QK) + EPS) * q_scale
        qo_ref[:, b:b + LANES] = (qn * r * gq_n).astype(BF16)
        qo_ref[:, b + LANES:b + D_PAD] = _rope(qr * r * gq_r, cos, slo, shi, D_ROPE // 4).astype(BF16)
        kn = kv_ref[:, b:b + LANES].astype(F32)
        ss = jnp.sum(kn * kn, axis=-1, keepdims=True) + kr_ss
        r = lax.rsqrt(ss * (1.0 / D_QK) + EPS)
        ko_ref[:, b:b + LANES] = (kn * r * gk_n).astype(BF16)
        ko_ref[:, b + LANES:b + D_PAD] = _rope(kr * r * gk_r, cos, slo, shi, D_ROPE // 4).astype(BF16)


def _stack_heads(q_ref, g, d):
    return jnp.concatenate([q_ref[:, i * d:(i + 1) * d] for i in range(g)], axis=0)


def _qk(q, k):
    return lax.dot_general(q, k, (((1,), (1,)), ((), ())), preferred_element_type=F32)


def _window_attn_kernel(q_ref, kp_ref, km_ref, kn_ref, kc_ref, vp_ref, vm_ref, vn_ref, vc_ref, sink_ref,
                        o_ref, *, g, st):
    i = pl.program_id(1)
    tq = q_ref.shape[0]
    half = tq // 2
    qs = _stack_heads(q_ref, g, HEAD_DIM)
    k_loc = jnp.concatenate([kp_ref[...], km_ref[...], kn_ref[...]], axis=0)
    v_loc = jnp.concatenate([vp_ref[...], vm_ref[...], vn_ref[...]], axis=0)
    s_loc = _qk(qs, k_loc)
    s_ctx = _qk(qs, kc_ref[...])
    a = lax.broadcasted_iota(jnp.int32, s_loc.shape, 0) & (tq - 1)
    j = lax.broadcasted_iota(jnp.int32, s_loc.shape, 1)
    diff = j - half - a
    krow = i * tq - half + j
    valid = (jnp.abs(diff) <= half) & (krow >= tq) & (krow < st) & (i >= 1)
    s_loc = jnp.where(valid, s_loc, NEG_BIG)
    sink = jnp.concatenate([jnp.broadcast_to(sink_ref[0, h:h + 1, 0:1], (tq, 1)) for h in range(g)], axis=0)
    m = jnp.maximum(jnp.maximum(jnp.max(s_loc, axis=-1, keepdims=True),
                                jnp.max(s_ctx, axis=-1, keepdims=True)), sink)
    p_loc = jnp.exp2(s_loc - m)
    p_ctx = jnp.exp2(s_ctx - m)
    l = jnp.sum(p_loc, axis=-1, keepdims=True) + jnp.sum(p_ctx, axis=-1, keepdims=True) + jnp.exp2(sink - m)
    o = (jnp.dot(p_loc.astype(BF16), v_loc, preferred_element_type=F32)
         + jnp.dot(p_ctx.astype(BF16), vc_ref[...], preferred_element_type=F32)) / l
    for h in range(g):
        o_ref[:, h * HEAD_DIM:(h + 1) * HEAD_DIM] = o[h * tq:(h + 1) * tq].astype(BF16)


def window_attention(pn, sink, n_kv, g, c_q, c_k, c_v):
    st = pn.shape[0]
    tq = ROW_TILE
    half = tq // 2
    nh = st // half
    q_spec = pl.BlockSpec((tq, g * HEAD_DIM), lambda h, i: (i, c_q // g + h))

    def kv_specs(c0):
        return [
            pl.BlockSpec((half, HEAD_DIM), lambda h, i: (jnp.maximum(2 * i - 1, 0), c0 + h)),
            pl.BlockSpec((tq, HEAD_DIM), lambda h, i: (i, c0 + h)),
            pl.BlockSpec((half, HEAD_DIM), lambda h, i: (jnp.minimum(2 * i + 2, nh - 1), c0 + h)),
            pl.BlockSpec((tq, HEAD_DIM), lambda h, i: (0, c0 + h)),
        ]

    return pl.pallas_call(
        functools.partial(_window_attn_kernel, g=g, st=st),
        grid=(n_kv, st // tq),
        in_specs=[q_spec] + kv_specs(c_k) + kv_specs(c_v)
        + [pl.BlockSpec((1, g, LANES), lambda h, i: (h, 0, 0))],
        out_specs=pl.BlockSpec((tq, g * HEAD_DIM), lambda h, i: (i, h)),
        out_shape=jax.ShapeDtypeStruct((st, n_kv * g * HEAD_DIM), BF16),
        compiler_params=_cparams("arbitrary", "arbitrary"),
        name="window_attention",
    )(pn, pn, pn, pn, pn, pn, pn, pn, pn, sink)


def _flash_kernel(flag_ref, q_ref, k_ref, v_ref, aux_ref, o_ref, acc_ref, l_ref, accc_ref, lc_ref, *,
                  g, comps, dq, dv, tk, n_ctx, diff):
    qi = pl.program_id(1)
    tq = q_ref.shape[0]
    n_keys = k_ref.shape[0]
    n_lat = (n_keys - n_ctx) // tk
    ctx_only_tile = tq == n_ctx
    n_iter = jnp.where(qi == 0, 0, n_lat) if ctx_only_tile else n_lat
    n_stacked = g * comps
    qs = [jnp.concatenate([q_ref[:, (gi * comps + c) * dq:(gi * comps + c + 1) * dq] for gi in range(g)], axis=0)
          for c in range(comps)]

    def scores(start, size):
        return jnp.concatenate(
            [_qk(qs[c], k_ref[pl.ds(start, size), c * dq:(c + 1) * dq]) for c in range(comps)], axis=0)

    def chunk_start(j):
        return pl.multiple_of(n_ctx + j * tk, math.gcd(n_ctx, tk))

    def keep_context_rows(l_is_partial):
        if ctx_only_tile:
            return

        @pl.when(qi == 0)
        def _():
            for h in range(n_stacked):
                accc_ref[h * n_ctx:(h + 1) * n_ctx] = acc_ref[h * tq:h * tq + n_ctx]
                lrows = l_ref[h * tq:h * tq + n_ctx]
                if l_is_partial:
                    lrows = jnp.broadcast_to(jnp.sum(lrows, axis=-1, keepdims=True), lrows.shape)
                lc_ref[h * n_ctx:(h + 1) * n_ctx] = lrows

    @pl.when(flag_ref[0] == 1)
    def _bounded():
        def step(start, size, first):
            p = jnp.exp2(scores(start, size))
            psum = p[:, :LANES]
            for b in range(1, size // LANES):
                psum = psum + p[:, b * LANES:(b + 1) * LANES]
            pv = jnp.dot(p.astype(BF16), v_ref[pl.ds(start, size), :], preferred_element_type=F32)
            if first:
                acc_ref[...] = pv
                l_ref[...] = psum
            else:
                acc_ref[...] += pv
                l_ref[...] += psum

        step(0, n_ctx, True)
        keep_context_rows(True)

        def body(j, carry):
            step(chunk_start(j), tk, False)
            return carry

        lax.fori_loop(0, n_iter, body, 0)
        l_ref[...] = jnp.broadcast_to(jnp.sum(l_ref[...], axis=-1, keepdims=True), l_ref.shape)

    @pl.when(flag_ref[0] != 1)
    def _online():
        def step(start, size, m, l, first):
            s = scores(start, size)
            m_cur = jnp.max(s, axis=-1, keepdims=True)
            m_new = m_cur if first else jnp.maximum(m, m_cur)
            p = jnp.exp2(s - m_new)
            pv = jnp.dot(p.astype(BF16), v_ref[pl.ds(start, size), :], preferred_element_type=F32)
            if first:
                l_new = jnp.sum(p, axis=-1, keepdims=True)
                acc_ref[...] = pv
            else:
                alpha = jnp.exp2(m - m_new)
                l_new = alpha * l + jnp.sum(p, axis=-1, keepdims=True)
                acc_ref[...] = alpha * acc_ref[...] + pv
            return m_new, l_new

        m, l = step(0, n_ctx, None, None, True)
        l_ref[...] = jnp.broadcast_to(l, l_ref.shape)
        keep_context_rows(False)
        m, l = lax.fori_loop(0, n_iter, lambda j, c: step(chunk_start(j), tk, c[0], c[1], False), (m, l))
        l_ref[...] = jnp.broadcast_to(l, l_ref.shape)

    def finalize(acc, l, rows):
        o = acc / l
        if diff:
            lam = aux_ref[0:1, 0:1]
            for gi in range(g):
                y = o[gi * rows:(gi + 1) * rows] - lam * o[(g + gi) * rows:(g + gi + 1) * rows]
                ss = jnp.mean(y * y, axis=-1, keepdims=True)
                y = y * lax.rsqrt(ss + EPS) * aux_ref[1:2, :]
                o_ref[0:rows, gi * dv:(gi + 1) * dv] = y.astype(BF16)
        else:
            for gi in range(g):
                o_ref[0:rows, gi * dv:(gi + 1) * dv] = o[gi * rows:(gi + 1) * rows].astype(BF16)

    finalize(acc_ref[...], l_ref[:, 0:1], tq)
    if not ctx_only_tile:
        @pl.when(qi == 0)
        def _():
            finalize(accc_ref[...], lc_ref[:, 0:1], n_ctx)


def _score_bound_flag(q_gain, k_gain, n_norm, scale):
    bound = n_norm * jnp.max(jnp.abs(q_gain)) * jnp.max(jnp.abs(k_gain)) * scale * BF16_NORM_SLACK
    return (bound <= SCORE_BOUND).astype(jnp.int32).reshape(1)


def flash_attention(q_arr, k_arr, v_arr, aux, flag, *, n_kv, g, comps, dq, dv, c_q, c_k, c_v, tk, tq=ROW_TILE,
                    diff=False):
    st = q_arr.shape[0]
    n_ctx = ROW_TILE
    assert st % tq == 0 and tq % n_ctx == 0
    while (st - n_ctx) % tk:
        tk //= 2
    rows = g * comps * tq
    ctx_rows = g * comps * n_ctx if tq != n_ctx else 8
    return pl.pallas_call(
        functools.partial(_flash_kernel, g=g, comps=comps, dq=dq, dv=dv, tk=tk, n_ctx=n_ctx, diff=diff),
        grid=(n_kv, st // tq),
        in_specs=[
            pl.BlockSpec(memory_space=pltpu.SMEM),
            pl.BlockSpec((tq, g * comps * dq), lambda h, i: (i, c_q + h)),
            pl.BlockSpec((st, comps * dq), lambda h, i: (0, c_k + h)),
            pl.BlockSpec((st, dv), lambda h, i: (0, c_v(h))),
            pl.BlockSpec(aux.shape, lambda h, i: (0, 0)),
        ],
        out_specs=pl.BlockSpec((tq, g * dv), lambda h, i: (i, h)),
        out_shape=jax.ShapeDtypeStruct((st, n_kv * g * dv), BF16),
        scratch_shapes=[pltpu.VMEM((rows, dv), F32), pltpu.VMEM((rows, LANES), F32),
                        pltpu.VMEM((ctx_rows, dv), F32), pltpu.VMEM((ctx_rows, LANES), F32)],
        compiler_params=_cparams("arbitrary", "arbitrary"),
        name="flash_attention",
    )(flag, q_arr, k_arr, v_arr, aux)


MOE_TILE = 256


def _gather_kernel(idx_ref, src_ref, o_ref, sem):
    tr = o_ref.shape[0]
    base = pl.program_id(0) * tr

    def row_copy(r, src_row):
        return pltpu.make_async_copy(src_ref.at[pl.ds(src_row, 1)], o_ref.at[pl.ds(r, 1)], sem)

    def issue(r, carry):
        row_copy(r, idx_ref[base + r]).start()
        return carry

    def drain(r, carry):
        row_copy(r, 0).wait()
        return carry

    lax.fori_loop(0, tr, issue, 0)
    lax.fori_loop(0, tr, drain, 0)


def row_gather(src, idx):
    m = idx.shape[0]
    d = src.shape[1]
    tr = MOE_TILE
    return pl.pallas_call(
        _gather_kernel,
        grid_spec=pltpu.PrefetchScalarGridSpec(
            num_scalar_prefetch=1,
            grid=(m // tr,),
            in_specs=[pl.BlockSpec(memory_space=pl.ANY)],
            out_specs=pl.BlockSpec((tr, d), lambda i, idx_ref: (i, 0)),
            scratch_shapes=[pltpu.SemaphoreType.DMA(())],
        ),
        out_shape=jax.ShapeDtypeStruct((m, d), src.dtype),
        compiler_params=_cparams("arbitrary"),
        name="row_gather",
    )(idx, src)


def _grouped_moe_kernel(te_ref, nt_ref, x_ref, wgu_ref, bgu_ref, wd_ref, bd_ref, o_ref, *, ff):
    t = pl.program_id(0)

    @pl.when(t < nt_ref[0])
    def _():
        gu = jnp.dot(x_ref[...].astype(BF16), wgu_ref[0], preferred_element_type=F32) + bgu_ref[0]
        glu = jnp.minimum(gu[:, :ff], SWIGLU_LIMIT)
        lin = jnp.clip(gu[:, ff:], -SWIGLU_LIMIT, SWIGLU_LIMIT)
        act = glu * (1.0 / (1.0 + jnp.exp(-SWIGLU_ALPHA * glu))) * (lin + 1.0)
        o_ref[...] = jnp.dot(act.astype(BF16), wd_ref[0], preferred_element_type=F32) + bd_ref[0]

    @pl.when(t >= nt_ref[0])
    def _():
        o_ref[...] = jnp.zeros_like(o_ref)


def grouped_moe(xs, tile_expert, n_tiles_used, w_gu, b_gu, w_down, b_down):
    p, d = xs.shape
    n_exp, _, ff2 = w_gu.shape
    ff = ff2 // 2
    tg = MOE_TILE
    return pl.pallas_call(
        functools.partial(_grouped_moe_kernel, ff=ff),
        grid_spec=pltpu.PrefetchScalarGridSpec(
            num_scalar_prefetch=2,
            grid=(p // tg,),
            in_specs=[
                pl.BlockSpec((tg, d), lambda t, te, nt: (t, 0)),
                pl.BlockSpec((1, d, ff2), lambda t, te, nt: (te[t], 0, 0)),
                pl.BlockSpec((1, 1, ff2), lambda t, te, nt: (te[t], 0, 0)),
                pl.BlockSpec((1, ff, d), lambda t, te, nt: (te[t], 0, 0)),
                pl.BlockSpec((1, 1, d), lambda t, te, nt: (te[t], 0, 0)),
            ],
            out_specs=pl.BlockSpec((tg, d), lambda t, te, nt: (t, 0)),
        ),
        out_shape=jax.ShapeDtypeStruct((p, d), F32),
        compiler_params=_cparams("arbitrary"),
        name="grouped_moe",
    )(tile_expert, n_tiles_used, xs, w_gu, b_gu.reshape(n_exp, 1, ff2), w_down, b_down.reshape(n_exp, 1, d))


def _combine_kernel(route_ref, *refs):
    o_ref = refs[-1]
    route = route_ref[...]
    acc = None
    for k in range(TOP_K):
        term = route[:, TOP_K + k:TOP_K + k + 1] * refs[k][...]
        acc = term if acc is None else acc + term
    o_ref[...] = acc.astype(o_ref.dtype)


def moe_combine(route, ys4):
    st = route.shape[0]
    d = ys4.shape[1]
    tm = ROW_TILE
    nt = st // tm
    return pl.pallas_call(
        _combine_kernel,
        grid=(nt,),
        in_specs=[pl.BlockSpec((tm, LANES), lambda i: (i, 0))]
        + [pl.BlockSpec((tm, d), functools.partial(lambda i, k: (k * nt + i, 0), k=k)) for k in range(TOP_K)],
        out_specs=pl.BlockSpec((tm, d), lambda i: (i, 0)),
        out_shape=jax.ShapeDtypeStruct((st, d), BF16),
        compiler_params=_cparams("arbitrary"),
        name="moe_combine",
    )(route, *([ys4] * TOP_K))


def moe_sparse(h, route, w_gu, b_gu, w_down, b_down):
    st = h.shape[0]
    n_exp = w_gu.shape[0]
    tg = MOE_TILE
    ids = route[:, :TOP_K].astype(jnp.int32)
    onehot = (ids[:, :, None] == jnp.arange(n_exp)[None, None, :]).any(axis=1)
    counts = jnp.sum(onehot, axis=0, dtype=jnp.int32)
    tiles = (counts + tg - 1) // tg
    tile_end = jnp.cumsum(tiles)
    row_off = (tile_end - tiles) * tg
    rank = jnp.cumsum(onehot.astype(jnp.int32), axis=0) - 1
    pos = row_off[ids] + jnp.take_along_axis(rank, ids, axis=1)
    n_tiles = (TOP_K * st + n_exp * (tg - 1)) // tg
    tok = jnp.zeros((n_tiles * tg,), jnp.int32).at[pos.reshape(-1)].set(
        jnp.repeat(jnp.arange(st, dtype=jnp.int32), TOP_K), unique_indices=True)
    tile_expert = jnp.minimum(jnp.searchsorted(tile_end, jnp.arange(n_tiles), side="right"),
                              n_exp - 1).astype(jnp.int32)
    xs = row_gather(h, tok)
    ys = grouped_moe(xs, tile_expert, tile_end[-1:].astype(jnp.int32), w_gu, b_gu, w_down, b_down)
    ys4 = row_gather(ys, pos.T.reshape(-1).astype(jnp.int32))
    return moe_combine(route, ys4)


def _rope_tables(n_ctx, n_lat, dim):
    quarter = dim // 4
    inv_freq = ROPE_THETA ** (-jnp.arange(quarter, dtype=F32) / quarter)
    t = jnp.arange(n_lat)
    row = (t // GRID_W).astype(F32)
    col = (t % GRID_W).astype(F32)
    ang = jnp.stack([row[:, None] * inv_freq, col[:, None] * inv_freq], axis=1)
    ang = jnp.broadcast_to(ang[:, :, None, :], (n_lat, 2, 2, quarter)).reshape(n_lat, dim)
    cos, sin = jnp.cos(ang), jnp.sin(ang)
    first_half = (jnp.arange(dim) % (2 * quarter)) < quarter
    sin_lo = jnp.where(first_half, -sin, 0.0)
    sin_hi = jnp.where(first_half, 0.0, sin)

    def full(tab, fill):
        tab = jnp.pad(tab, ((0, 0), (0, LANES - dim)), constant_values=fill)
        return jnp.concatenate([jnp.full((n_ctx, LANES), fill, F32), tab], axis=0)

    return full(cos, 1.0), full(sin_lo, 0.0), full(sin_hi, 0.0)


def _pad_cols(w, n):
    return jnp.pad(w, ((0, 0), (0, n - w.shape[1])))


def _streams(v_ctx, v_lat):
    return jnp.stack([v_ctx, v_lat], axis=0)[:, None, :]


def _even_mixer(h, rope128, w_in, w_out, a_qn, a_kn, a_sink, b_qn, b_kn, b_lam, b_subln, lam_init, d):
    a_heads = d // (2 * HEAD_DIM)
    a_kv = a_heads // 4
    b_heads = d // (4 * HEAD_DIM)
    b_kv = b_heads // 2
    ga, gb = a_heads // a_kv, b_heads // b_kv
    scale = HEAD_DIM ** -0.5 * LOG2E
    n_qa, n_qb, n_ka, n_va, n_kb, n_vb = a_heads, 2 * b_heads, a_kv, a_kv, 2 * b_kv, 2 * b_kv
    c_qb = n_qa
    c_ka = c_qb + n_qb
    c_va = c_ka + n_ka
    c_kb = c_va + n_va
    c_vb = c_kb + n_kb
    p = matmul([h], [w_in.astype(BF16)], tn=512)
    gains = jnp.stack([a_qn, b_qn, a_kn, b_kn], axis=0).astype(F32)
    pn = prep_heads(p, rope128, gains,
                    segs=((0, c_qb, 0, scale), (c_qb, c_ka, 1, scale), (c_ka, c_va, 2, 1.0), (c_kb, c_vb, 3, 1.0)),
                    copies=((c_va, c_kb), (c_vb, c_vb + n_vb)))
    sink = jnp.broadcast_to((a_sink.astype(F32) * LOG2E).reshape(a_kv, ga, 1), (a_kv, ga, LANES))
    ya = window_attention(pn, sink, a_kv, ga, 0, c_ka, c_va)
    lf = b_lam.astype(F32)
    lam = jnp.exp(jnp.sum(lf[0] * lf[1])) - jnp.exp(jnp.sum(lf[2] * lf[3])) + lam_init
    dvb = 2 * HEAD_DIM
    aux = jnp.stack([jnp.full((dvb,), lam, F32), b_subln.astype(F32) * (1.0 - lam_init)], axis=0)
    flag = _score_bound_flag(b_qn, b_kn, HEAD_DIM, scale)
    yb = flash_attention(pn, pn, pn, aux, flag, n_kv=b_kv, g=gb, comps=2, dq=HEAD_DIM, dv=dvb,
                         c_q=c_qb * HEAD_DIM // (gb * 2 * HEAD_DIM), c_k=c_kb // 2, c_v=lambda hh: c_vb // 2 + hh,
                         tk=1024, diff=True)
    w_out = w_out.astype(BF16)
    na = a_heads * HEAD_DIM
    return matmul([ya, yb], [w_out[:na], w_out[na:]], tn=512)


def _odd_mixer(h, rope128, rope64, w_in, w_out, c_qn, c_kn, d_qa_norm, d_kva_norm, d_wq_up, d_wkv_up, d_qn, d_kn, d):
    st = h.shape[0]
    c_heads = d // (2 * HEAD_DIM)
    c_kv = c_heads // 4
    gc = c_heads // c_kv
    d_heads = d // (2 * HEAD_DIM)
    q_rank = d_wq_up.shape[0]
    kv_rank = d_wkv_up.shape[0]
    n_qc, n_qa, n_kc, n_vc, n_kva = c_heads, q_rank // LANES, c_kv, c_kv, kv_rank // LANES
    c_qa = n_qc
    c_kc = c_qa + n_qa
    c_vc = c_kc + n_kc
    c_kva = c_vc + n_vc
    c_kr = c_kva + n_kva
    n_in = (c_kr + 1) * LANES
    tn = 768
    n_pad = -(-n_in // tn) * tn
    p = matmul([h], [_pad_cols(w_in, n_pad).astype(BF16)], tn=tn)
    tm = ROW_TILE
    tab = pl.BlockSpec((tm, LANES), lambda i: (i, 0))
    gains = jnp.stack([c_qn, c_kn], axis=0).astype(F32)
    qk, qa, kva = pl.pallas_call(
        functools.partial(_prep_odd1_kernel, n_q=n_qc, n_k=n_kc, c_k=c_kc, c_qa=c_qa, n_qa=n_qa, c_kva=c_kva,
                          n_kva=n_kva, q_scale=HEAD_DIM ** -0.5 * LOG2E),
        grid=(st // tm,),
        in_specs=[pl.BlockSpec((tm, n_pad), lambda i: (i, 0)), tab, tab, tab,
                  pl.BlockSpec((2, LANES), lambda i: (0, 0)),
                  pl.BlockSpec((1, q_rank), lambda i: (0, 0)),
                  pl.BlockSpec((1, kv_rank), lambda i: (0, 0))],
        out_specs=[pl.BlockSpec((tm, (n_qc + n_kc) * LANES), lambda i: (i, 0)),
                   pl.BlockSpec((tm, q_rank), lambda i: (i, 0)),
                   pl.BlockSpec((tm, kv_rank), lambda i: (i, 0))],
        out_shape=[jax.ShapeDtypeStruct((st, (n_qc + n_kc) * LANES), BF16),
                   jax.ShapeDtypeStruct((st, q_rank), BF16),
                   jax.ShapeDtypeStruct((st, kv_rank), BF16)],
        compiler_params=_cparams("arbitrary"),
        name="prep_odd1",
    )(p, *rope128, gains, d_qa_norm.astype(F32)[None, :], d_kva_norm.astype(F32)[None, :])
    wq = jnp.pad(d_wq_up.reshape(q_rank, d_heads, D_QK), ((0, 0), (0, 0), (0, D_PAD - D_QK)))
    qd_raw = matmul([qa], [wq.reshape(q_rank, d_heads * D_PAD).astype(BF16)], tn=512)
    kv = matmul([kva], [d_wkv_up.astype(BF16)], tn=512)
    pad_gain = lambda gvec: jnp.pad(gvec.astype(F32), (0, D_PAD - D_QK))[None, :]
    qd, kd = pl.pallas_call(
        functools.partial(_prep_odd2_kernel, n_heads=d_heads, q_scale=D_QK ** -0.5 * LOG2E),
        grid=(st // tm,),
        in_specs=[pl.BlockSpec((tm, d_heads * D_PAD), lambda i: (i, 0)),
                  pl.BlockSpec((tm, d_heads * D_PAD), lambda i: (i, 0)),
                  pl.BlockSpec((tm, LANES), lambda i: (i, c_kr)),
                  tab, tab, tab,
                  pl.BlockSpec((1, D_PAD), lambda i: (0, 0)),
                  pl.BlockSpec((1, D_PAD), lambda i: (0, 0))],
        out_specs=[pl.BlockSpec((tm, d_heads * D_PAD), lambda i: (i, 0)),
                   pl.BlockSpec((tm, d_heads * D_PAD), lambda i: (i, 0))],
        out_shape=[jax.ShapeDtypeStruct((st, d_heads * D_PAD), BF16),
                   jax.ShapeDtypeStruct((st, d_heads * D_PAD), BF16)],
        compiler_params=_cparams("arbitrary"),
        name="prep_odd2",
    )(qd_raw, kv, p, *rope64, pad_gain(d_qn), pad_gain(d_kn))
    aux = jnp.zeros((8, LANES), F32)
    flag_c = _score_bound_flag(c_qn, c_kn, HEAD_DIM, HEAD_DIM ** -0.5 * LOG2E)
    flag_d = _score_bound_flag(d_qn, d_kn, D_QK, D_QK ** -0.5 * LOG2E)
    yc = flash_attention(qk, qk, p, aux, flag_c, n_kv=c_kv, g=gc, comps=1, dq=HEAD_DIM, dv=HEAD_DIM,
                         c_q=0, c_k=n_qc, c_v=lambda hh: c_vc + hh, tk=1024)
    yd = flash_attention(qd, kd, kv, aux, flag_d, n_kv=d_heads, g=1, comps=1, dq=D_PAD, dv=D_V,
                         c_q=0, c_k=0, c_v=lambda hh: 2 * hh + 1, tk=1024,
                         tq=1280 if st % 1280 == 0 else ROW_TILE)
    w_out = w_out.astype(BF16)
    nc = c_heads * HEAD_DIM
    return matmul([yc, yd], [w_out[:nc], w_out[nc:]], tn=512)


def kernel(x, c, ctx, c_ctx, adaln_down, adaln_up, adaln_b, norm_mix, norm_ffn, ev_w_in, ev_w_out, ev_a_qn, ev_a_kn,
           ev_a_sink, ev_b_qn, ev_b_kn, ev_b_lam, ev_b_subln, od_w_in, od_w_out, od_c_qn, od_c_kn, od_d_qa_norm,
           od_d_kva_norm, od_d_wq_up, od_d_wkv_up, od_d_qn, od_d_kn, router_w, router_b, moe_w_gu, moe_b_gu,
           moe_w_down, moe_b_down):
    bsz, seq, d = x.shape
    n_ctx = ctx.shape[1]
    depth = adaln_down.shape[0]
    n_exp = router_w.shape[2]
    assert bsz == 1 and n_ctx == ROW_TILE and seq % ROW_TILE == 0 and n_exp <= LANES

    rope128 = _rope_tables(n_ctx, seq, HEAD_DIM)
    rope64 = _rope_tables(n_ctx, seq, D_ROPE)
    cvecs = jnp.zeros((8, d), F32).at[0].set(c[0]).at[1].set(c_ctx)
    mods = adaln_all(cvecs, adaln_down, adaln_up, adaln_b)

    xs = jnp.concatenate([ctx[0], x[0]], axis=0)
    y_prev, gate_prev = None, None
    for l in range(depth):
        m_lat = mods[l, 0].reshape(N_MOD, d)
        m_ctx = mods[l, 1].reshape(N_MOD, d)
        mult = _streams(norm_mix[l] * (1.0 + m_ctx[1]), norm_mix[l] * (1.0 + m_lat[1]))
        shift = _streams(m_ctx[0], m_lat[0])
        if y_prev is None:
            (h,) = modulate(xs, mult, shift)
        else:
            xs, h = modulate(xs, mult, shift, res=(y_prev, gate_prev))
        j = l // 2
        if l % 2 == 0:
            lam_init = 0.8 - 0.6 * math.exp(-0.3 * l)
            y = _even_mixer(h, rope128, ev_w_in[j], ev_w_out[j], ev_a_qn[j], ev_a_kn[j], ev_a_sink[j], ev_b_qn[j],
                            ev_b_kn[j], ev_b_lam[j], ev_b_subln[j], lam_init, d)
        else:
            y = _odd_mixer(h, rope128, rope64, od_w_in[j], od_w_out[j], od_c_qn[j], od_c_kn[j], od_d_qa_norm[j],
                           od_d_kva_norm[j], od_d_wq_up[j], od_d_wkv_up[j], od_d_qn[j], od_d_kn[j], d)
        mult = _streams(norm_ffn[l] * (1.0 + m_ctx[4]), norm_ffn[l] * (1.0 + m_lat[4]))
        shift = _streams(m_ctx[3], m_lat[3])
        rw = _pad_cols(router_w[l].astype(F32), LANES)
        rb = jnp.pad(router_b[l].astype(F32), (0, LANES - n_exp))[None, :]
        xs, h2, route = modulate(xs, mult, shift, res=(y, _streams(m_ctx[2], m_lat[2])), router=(rw, rb, n_exp))
        y_prev = moe_sparse(h2, route, moe_w_gu[l].astype(BF16), moe_b_gu[l].astype(F32),
                            moe_w_down[l].astype(BF16), moe_b_down[l].astype(F32))
        gate_prev = _streams(m_ctx[5], m_lat[5])
    out = final_residual(xs, y_prev, gate_prev)
    return out[None]
```

```python
import functools
import math

import jax
import jax.numpy as jnp
from jax import lax
from jax.experimental import pallas as pl
from jax.experimental.pallas import tpu as pltpu

F32 = jnp.float32
BF16 = jnp.bfloat16

GRID_W = 64
HEAD_DIM = 128
ROPE_THETA = 10000.0
EPS = 1e-6
N_MOD = 6
D_NOPE = 128
D_ROPE = 64
D_QK = D_NOPE + D_ROPE
D_V = 128
D_PAD = 256
TOP_K = 4
SWIGLU_ALPHA = 1.702
SWIGLU_LIMIT = 7.0
LOG2E = 1.4426950408889634
NEG_BIG = -1e30
SCORE_BOUND = 60.0
BF16_NORM_SLACK = 1.02

LANES = 128
ROW_TILE = 256
VMEM_LIMIT = 56 * 1024 * 1024


def _cparams(*sem):
    return pltpu.CompilerParams(dimension_semantics=sem, vmem_limit_bytes=VMEM_LIMIT)


def _adaln_kernel(c_ref, down_ref, up_ref, b_ref, o_ref, t_ref):
    @pl.when(pl.program_id(1) == 0)
    def _():
        c = c_ref[...]
        a = c * (1.0 / (1.0 + jnp.exp(-c)))
        t_ref[...] = jnp.dot(a, down_ref[0], preferred_element_type=F32, precision=lax.Precision.HIGHEST)

    o_ref[0] = jnp.dot(t_ref[...], up_ref[0], preferred_element_type=F32,
                       precision=lax.Precision.HIGHEST) + b_ref[0]


def adaln_all(cvecs, down, up, bias):
    depth, d, rank = down.shape
    n = up.shape[2]
    tn = 2048
    assert n % tn == 0
    return pl.pallas_call(
        _adaln_kernel,
        grid=(depth, n // tn),
        in_specs=[
            pl.BlockSpec((8, d), lambda l, j: (0, 0)),
            pl.BlockSpec((1, d, rank), lambda l, j: (l, 0, 0)),
            pl.BlockSpec((1, rank, tn), lambda l, j: (l, 0, j)),
            pl.BlockSpec((1, 1, tn), lambda l, j: (l, 0, j)),
        ],
        out_specs=pl.BlockSpec((1, 8, tn), lambda l, j: (l, 0, j)),
        out_shape=jax.ShapeDtypeStruct((depth, 8, n), F32),
        scratch_shapes=[pltpu.VMEM((8, rank), F32)],
        compiler_params=_cparams("arbitrary", "arbitrary"),
        name="adaln",
    )(cvecs, down, up, bias.reshape(depth, 1, n))


def _topk_route(logits, n_exp):
    lane = lax.broadcasted_iota(jnp.int32, logits.shape, 1).astype(F32)
    work = jnp.where(lane < n_exp, logits, -jnp.inf)
    ids = jnp.zeros_like(logits)
    wts = jnp.zeros_like(logits)
    denom = None
    v0 = None
    for k in range(TOP_K):
        m = jnp.max(work, axis=-1, keepdims=True)
        idx = jnp.min(jnp.where(work == m, lane, float(LANES)), axis=-1, keepdims=True)
        if k == 0:
            v0 = m
            e = jnp.ones_like(m)
            denom = e
        else:
            e = jnp.exp(m - v0)
            denom = denom + e
        ids = jnp.where(lane == k, idx, ids)
        wts = jnp.where(lane == TOP_K + k, e, wts)
        work = jnp.where(lane == idx, -jnp.inf, work)
    return ids + wts / denom


def _modulate_kernel(*refs, has_res, n_exp):
    it = iter(refs)
    x_ref = next(it)
    y_ref = next(it) if has_res else None
    gate_ref = next(it) if has_res else None
    mult_ref = next(it)
    shift_ref = next(it)
    rw_ref = next(it) if n_exp else None
    rb_ref = next(it) if n_exp else None
    xo_ref = next(it) if has_res else None
    h_ref = next(it)
    g_ref = next(it) if n_exp else None

    x = x_ref[...]
    if has_res:
        x = x + gate_ref[0] * y_ref[...].astype(F32)
        xo_ref[...] = x
    ms = jnp.mean(x * x, axis=-1, keepdims=True)
    h = x * lax.rsqrt(ms + EPS) * mult_ref[0] + shift_ref[0]
    h_ref[...] = h.astype(h_ref.dtype)
    if n_exp:
        logits = jnp.dot(h, rw_ref[...], preferred_element_type=F32,
                         precision=lax.Precision.HIGHEST) + rb_ref[...]
        g_ref[...] = _topk_route(logits, n_exp)


def modulate(x, mult, shift, res=None, router=None):
    st, d = x.shape
    tm = ROW_TILE
    stream = lambda i: (jnp.minimum(i, 1), 0, 0)
    row = lambda i: (i, 0)
    vec_spec = pl.BlockSpec((1, 1, d), stream)
    args, in_specs = [x], [pl.BlockSpec((tm, d), row)]
    if res is not None:
        args += [res[0], res[1]]
        in_specs += [pl.BlockSpec((tm, d), row), vec_spec]
    args += [mult, shift]
    in_specs += [vec_spec, vec_spec]
    n_exp = 0
    if router is not None:
        args += [router[0], router[1]]
        n_exp = router[2]
        in_specs += [pl.BlockSpec((d, LANES), lambda i: (0, 0)), pl.BlockSpec((1, LANES), lambda i: (0, 0))]
    out_shape, out_specs = [], []
    if res is not None:
        out_shape.append(jax.ShapeDtypeStruct((st, d), F32))
        out_specs.append(pl.BlockSpec((tm, d), row))
    out_shape.append(jax.ShapeDtypeStruct((st, d), BF16 if router is None else F32))
    out_specs.append(pl.BlockSpec((tm, d), row))
    if router is not None:
        out_shape.append(jax.ShapeDtypeStruct((st, LANES), F32))
        out_specs.append(pl.BlockSpec((tm, LANES), row))
    return pl.pallas_call(
        functools.partial(_modulate_kernel, has_res=res is not None, n_exp=n_exp),
        grid=(st // tm,),
        in_specs=in_specs,
        out_specs=out_specs,
        out_shape=out_shape,
        compiler_params=_cparams("arbitrary"),
        name="modulate",
    )(*args)


def _final_residual_kernel(x_ref, y_ref, gate_ref, o_ref):
    o_ref[...] = x_ref[...] + gate_ref[0] * y_ref[...].astype(F32)


def final_residual(x, y, gate):
    st, d = x.shape
    tm = ROW_TILE
    n = st // tm - 1
    return pl.pallas_call(
        _final_residual_kernel,
        grid=(n,),
        in_specs=[
            pl.BlockSpec((tm, d), lambda i: (i + 1, 0)),
            pl.BlockSpec((tm, d), lambda i: (i + 1, 0)),
            pl.BlockSpec((1, 1, d), lambda i: (1, 0, 0)),
        ],
        out_specs=pl.BlockSpec((tm, d), lambda i: (i, 0)),
        out_shape=jax.ShapeDtypeStruct((n * tm, d), F32),
        compiler_params=_cparams("arbitrary"),
        name="final_residual",
    )(x, y, gate)


def _mm_kernel(*refs, n_pairs):
    o_ref = refs[-1]
    acc = None
    for p in range(n_pairs):
        t = jnp.dot(refs[p][...], refs[n_pairs + p][...], preferred_element_type=F32)
        acc = t if acc is None else acc + t
    o_ref[...] = acc.astype(o_ref.dtype)


def _row_block(m):
    for tm in (1280, 1024, 512, 256):
        if m % tm == 0:
            return tm
    raise ValueError(m)


def matmul(a_list, b_list, tn, out_dtype=BF16):
    m = a_list[0].shape[0]
    n = b_list[0].shape[1]
    tm = _row_block(m)
    while n % tn:
        tn -= LANES
    in_specs = [pl.BlockSpec((tm, a.shape[1]), lambda i, j: (i, 0)) for a in a_list]
    in_specs += [pl.BlockSpec((b.shape[0], tn), lambda i, j: (0, j)) for b in b_list]
    return pl.pallas_call(
        functools.partial(_mm_kernel, n_pairs=len(a_list)),
        grid=(m // tm, n // tn),
        in_specs=in_specs,
        out_specs=pl.BlockSpec((tm, tn), lambda i, j: (i, j)),
        out_shape=jax.ShapeDtypeStruct((m, n), out_dtype),
        compiler_params=_cparams("arbitrary", "arbitrary"),
        name="matmul",
    )(*a_list, *b_list)


def _rope(y, cos, sin_lo, sin_hi, quarter):
    left = pltpu.roll(y, LANES - quarter, 1)
    right = pltpu.roll(y, quarter, 1)
    return y * cos + left * sin_lo + right * sin_hi


def _head_norm(x, gain, n_valid):
    ss = jnp.sum(x * x, axis=-1, keepdims=True)
    return x * lax.rsqrt(ss * (1.0 / n_valid) + EPS) * gain


def _prep_even_kernel(p_ref, cos_ref, slo_ref, shi_ref, gains_ref, o_ref, *, segs, n_copy_from):
    cos, slo, shi = cos_ref[...], slo_ref[...], shi_ref[...]
    for (c0, c1, gi, scale) in segs:
        gain = gains_ref[gi:gi + 1, :]
        for c in range(c0, c1):
            x = p_ref[:, c * LANES:(c + 1) * LANES].astype(F32)
            y = _rope(_head_norm(x, gain, HEAD_DIM), cos, slo, shi, HEAD_DIM // 4)
            if scale != 1.0:
                y = y * scale
            o_ref[:, c * LANES:(c + 1) * LANES] = y.astype(BF16)
    for (c0, c1) in n_copy_from:
        o_ref[:, c0 * LANES:c1 * LANES] = p_ref[:, c0 * LANES:c1 * LANES]


def prep_heads(p, rope128, gains, segs, copies):
    st, n = p.shape
    tm = ROW_TILE
    tab = pl.BlockSpec((tm, LANES), lambda i: (i, 0))
    return pl.pallas_call(
        functools.partial(_prep_even_kernel, segs=segs, n_copy_from=copies),
        grid=(st // tm,),
        in_specs=[pl.BlockSpec((tm, n), lambda i: (i, 0)), tab, tab, tab,
                  pl.BlockSpec(gains.shape, lambda i: (0, 0))],
        out_specs=pl.BlockSpec((tm, n), lambda i: (i, 0)),
        out_shape=jax.ShapeDtypeStruct((st, n), BF16),
        compiler_params=_cparams("arbitrary"),
        name="prep_heads",
    )(p, *rope128, gains)


def _prep_odd1_kernel(p_ref, cos_ref, slo_ref, shi_ref, gains_ref, gqa_ref, gkva_ref,
                      qk_ref, qa_ref, kva_ref, *, n_q, n_k, c_k, c_qa, n_qa, c_kva, n_kva, q_scale):
    cos, slo, shi = cos_ref[...], slo_ref[...], shi_ref[...]
    for c in range(n_q):
        x = p_ref[:, c * LANES:(c + 1) * LANES].astype(F32)
        y = _rope(_head_norm(x, gains_ref[0:1, :], HEAD_DIM), cos, slo, shi, HEAD_DIM // 4) * q_scale
        qk_ref[:, c * LANES:(c + 1) * LANES] = y.astype(BF16)
    for c in range(n_k):
        x = p_ref[:, (c_k + c) * LANES:(c_k + c + 1) * LANES].astype(F32)
        y = _rope(_head_norm(x, gains_ref[1:2, :], HEAD_DIM), cos, slo, shi, HEAD_DIM // 4)
        qk_ref[:, (n_q + c) * LANES:(n_q + c + 1) * LANES] = y.astype(BF16)
    xa = p_ref[:, c_qa * LANES:(c_qa + n_qa) * LANES].astype(F32)
    qa_ref[...] = _head_norm(xa, gqa_ref[...], n_qa * LANES).astype(BF16)
    xk = p_ref[:, c_kva * LANES:(c_kva + n_kva) * LANES].astype(F32)
    kva_ref[...] = _head_norm(xk, gkva_ref[...], n_kva * LANES).astype(BF16)


def _prep_odd2_kernel(qd_ref, kv_ref, kr_ref, cos_ref, slo_ref, shi_ref, gq_ref, gk_ref,
                      qo_ref, ko_ref, *, n_heads, q_scale):
    cos, slo, shi = cos_ref[...], slo_ref[...], shi_ref[...]
    gq_n, gq_r = gq_ref[:, :LANES], gq_ref[:, LANES:]
    gk_n, gk_r = gk_ref[:, :LANES], gk_ref[:, LANES:]
    kr = kr_ref[...].astype(F32)
    kr_ss = jnp.sum(kr * kr, axis=-1, keepdims=True)
    for h in range(n_heads):
        b = h * D_PAD
        qn = qd_ref[:, b:b + LANES].astype(F32)
        qr = qd_ref[:, b + LANES:b + D_PAD].astype(F32)
        ss = jnp.sum(qn * qn, axis=-1, keepdims=True) + jnp.sum(qr * qr, axis=-1, keepdims=True)
        r = lax.rsqrt(ss * (1.0 / D_QK) + EPS) * q_scale
        qo_ref[:, b:b + LANES] = (qn * r * gq_n).astype(BF16)
        qo_ref[:, b + LANES:b + D_PAD] = _rope(qr * r * gq_r, cos, slo, shi, D_ROPE // 4).astype(BF16)
        kn = kv_ref[:, b:b + LANES].astype(F32)
        ss = jnp.sum(kn * kn, axis=-1, keepdims=True) + kr_ss
        r = lax.rsqrt(ss * (1.0 / D_QK) + EPS)
        ko_ref[:, b:b + LANES] = (kn * r * gk_n).astype(BF16)
        ko_ref[:, b + LANES:b + D_PAD] = _rope(kr * r * gk_r, cos, slo, shi, D_ROPE // 4).astype(BF16)


def _stack_heads(q_ref, g, d):
    return jnp.concatenate([q_ref[:, i * d:(i + 1) * d] for i in range(g)], axis=0)


def _qk(q, k):
    return lax.dot_general(q, k, (((1,), (1,)), ((), ())), preferred_element_type=F32)


def _window_attn_kernel(q_ref, kp_ref, km_ref, kn_ref, kc_ref, vp_ref, vm_ref, vn_ref, vc_ref, sink_ref,
                        o_ref, *, g, st):
    i = pl.program_id(1)
    tq = q_ref.shape[0]
    half = tq // 2
    qs = _stack_heads(q_ref, g, HEAD_DIM)
    k_loc = jnp.concatenate([kp_ref[...], km_ref[...], kn_ref[...]], axis=0)
    v_loc = jnp.concatenate([vp_ref[...], vm_ref[...], vn_ref[...]], axis=0)
    s_loc = _qk(qs, k_loc)
    s_ctx = _qk(qs, kc_ref[...])
    a = lax.broadcasted_iota(jnp.int32, s_loc.shape, 0) & (tq - 1)
    j = lax.broadcasted_iota(jnp.int32, s_loc.shape, 1)
    diff = j - half - a
    krow = i * tq - half + j
    valid = (jnp.abs(diff) <= half) & (krow >= tq) & (krow < st) & (i >= 1)
    s_loc = jnp.where(valid, s_loc, NEG_BIG)
    sink = jnp.concatenate([jnp.broadcast_to(sink_ref[0, h:h + 1, 0:1], (tq, 1)) for h in range(g)], axis=0)
    m = jnp.maximum(jnp.maximum(jnp.max(s_loc, axis=-1, keepdims=True),
                                jnp.max(s_ctx, axis=-1, keepdims=True)), sink)
    p_loc = jnp.exp2(s_loc - m)
    p_ctx = jnp.exp2(s_ctx - m)
    l = jnp.sum(p_loc, axis=-1, keepdims=True) + jnp.sum(p_ctx, axis=-1, keepdims=True) + jnp.exp2(sink - m)
    o = (jnp.dot(p_loc.astype(BF16), v_loc, preferred_element_type=F32)
         + jnp.dot(p_ctx.astype(BF16), vc_ref[...], preferred_element_type=F32)) / l
    for h in range(g):
        o_ref[:, h * HEAD_DIM:(h + 1) * HEAD_DIM] = o[h * tq:(h + 1) * tq].astype(BF16)


def window_attention(pn, sink, n_kv, g, c_q, c_k, c_v):
    st = pn.shape[0]
    tq = ROW_TILE
    half = tq // 2
    nh = st // half
    q_spec = pl.BlockSpec((tq, g * HEAD_DIM), lambda h, i: (i, c_q // g + h))

    def kv_specs(c0):
        return [
            pl.BlockSpec((half, HEAD_DIM), lambda h, i: (jnp.maximum(2 * i - 1, 0), c0 + h)),
            pl.BlockSpec((tq, HEAD_DIM), lambda h, i: (i, c0 + h)),
            pl.BlockSpec((half, HEAD_DIM), lambda h, i: (jnp.minimum(2 * i + 2, nh - 1), c0 + h)),
            pl.BlockSpec((tq, HEAD_DIM), lambda h, i: (0, c0 + h)),
        ]

    return pl.pallas_call(
        functools.partial(_window_attn_kernel, g=g, st=st),
        grid=(n_kv, st // tq),
        in_specs=[q_spec] + kv_specs(c_k) + kv_specs(c_v)
        + [pl.BlockSpec((1, g, LANES), lambda h, i: (h, 0, 0))],
        out_specs=pl.BlockSpec((tq, g * HEAD_DIM), lambda h, i: (i, h)),
        out_shape=jax.ShapeDtypeStruct((st, n_kv * g * HEAD_DIM), BF16),
        compiler_params=_cparams("arbitrary", "arbitrary"),
        name="window_attention",
    )(pn, pn, pn, pn, pn, pn, pn, pn, pn, sink)


def _flash_kernel(flag_ref, q_ref, k_ref, v_ref, aux_ref, o_ref, acc_ref, l_ref, accc_ref, lc_ref, *,
                  g, comps, dq, dv, tk, n_ctx, diff):
    qi = pl.program_id(1)
    tq = q_ref.shape[0]
    n_keys = k_ref.shape[0]
    n_lat = (n_keys - n_ctx) // tk
    ctx_only_tile = tq == n_ctx
    n_iter = jnp.where(qi == 0, 0, n_lat) if ctx_only_tile else n_lat
    n_stacked = g * comps
    qs = [jnp.concatenate([q_ref[:, (gi * comps + c) * dq:(gi * comps + c + 1) * dq] for gi in range(g)], axis=0)
          for c in range(comps)]

    def scores(start, size):
        return jnp.concatenate(
            [_qk(qs[c], k_ref[pl.ds(start, size), c * dq:(c + 1) * dq]) for c in range(comps)], axis=0)

    def chunk_start(j):
        return pl.multiple_of(n_ctx + j * tk, math.gcd(n_ctx, tk))

    def keep_context_rows(l_is_partial):
        if ctx_only_tile:
            return

        @pl.when(qi == 0)
        def _():
            for h in range(n_stacked):
                accc_ref[h * n_ctx:(h + 1) * n_ctx] = acc_ref[h * tq:h * tq + n_ctx]
                lrows = l_ref[h * tq:h * tq + n_ctx]
                if l_is_partial:
                    lrows = jnp.broadcast_to(jnp.sum(lrows, axis=-1, keepdims=True), lrows.shape)
                lc_ref[h * n_ctx:(h + 1) * n_ctx] = lrows

    @pl.when(flag_ref[0] == 1)
    def _bounded():
        def step(start, size, first):
            p = jnp.exp2(scores(start, size))
            psum = p[:, :LANES]
            for b in range(1, size // LANES):
                psum = psum + p[:, b * LANES:(b + 1) * LANES]
            pv = jnp.dot(p.astype(BF16), v_ref[pl.ds(start, size), :], preferred_element_type=F32)
            if first:
                acc_ref[...] = pv
                l_ref[...] = psum
            else:
                acc_ref[...] += pv
                l_ref[...] += psum

        step(0, n_ctx, True)
        keep_context_rows(True)

        def body(j, carry):
            step(chunk_start(j), tk, False)
            return carry

        lax.fori_loop(0, n_iter, body, 0)
        l_ref[...] = jnp.broadcast_to(jnp.sum(l_ref[...], axis=-1, keepdims=True), l_ref.shape)

    @pl.when(flag_ref[0] != 1)
    def _online():
        def step(start, size, m, l, first):
            s = scores(start, size)
            m_cur = jnp.max(s, axis=-1, keepdims=True)
            m_new = m_cur if first else jnp.maximum(m, m_cur)
            p = jnp.exp2(s - m_new)
            pv = jnp.dot(p.astype(BF16), v_ref[pl.ds(start, size), :], preferred_element_type=F32)
            if first:
                l_new = jnp.sum(p, axis=-1, keepdims=True)
                acc_ref[...] = pv
            else:
                alpha = jnp.exp2(m - m_new)
                l_new = alpha * l + jnp.sum(p, axis=-1, keepdims=True)
                acc_ref[...] = alpha * acc_ref[...] + pv
            return m_new, l_new

        m, l = step(0, n_ctx, None, None, True)
        l_ref[...] = jnp.broadcast_to(l, l_ref.shape)
        keep_context_rows(False)
        m, l = lax.fori_loop(0, n_iter, lambda j, c: step(chunk_start(j), tk, c[0], c[1], False), (m, l))
        l_ref[...] = jnp.broadcast_to(l, l_ref.shape)

    def finalize(acc, l, rows):
        o = acc / l
        if diff:
            lam = aux_ref[0:1, 0:1]
            for gi in range(g):
                y = o[gi * rows:(gi + 1) * rows] - lam * o[(g + gi) * rows:(g + gi + 1) * rows]
                ss = jnp.mean(y * y, axis=-1, keepdims=True)
                y = y * lax.rsqrt(ss + EPS) * aux_ref[1:2, :]
                o_ref[0:rows, gi * dv:(gi + 1) * dv] = y.astype(BF16)
        else:
            for gi in range(g):
                o_ref[0:rows, gi * dv:(gi + 1) * dv] = o[gi * rows:(gi + 1) * rows].astype(BF16)

    finalize(acc_ref[...], l_ref[:, 0:1], tq)
    if not ctx_only_tile:
        @pl.when(qi == 0)
        def _():
            finalize(accc_ref[...], lc_ref[:, 0:1], n_ctx)


def _score_bound_flag(q_gain, k_gain, n_norm, scale):
    bound = n_norm * jnp.max(jnp.abs(q_gain)) * jnp.max(jnp.abs(k_gain)) * scale * BF16_NORM_SLACK
    return (bound <= SCORE_BOUND).astype(jnp.int32).reshape(1)


def flash_attention(q_arr, k_arr, v_arr, aux, flag, *, n_kv, g, comps, dq, dv, c_q, c_k, c_v, tk, tq=ROW_TILE,
                    diff=False):
    st = q_arr.shape[0]
    n_ctx = ROW_TILE
    assert st % tq == 0 and tq % n_ctx == 0
    while (st - n_ctx) % tk:
        tk //= 2
    rows = g * comps * tq
    ctx_rows = g * comps * n_ctx if tq != n_ctx else 8
    return pl.pallas_call(
        functools.partial(_flash_kernel, g=g, comps=comps, dq=dq, dv=dv, tk=tk, n_ctx=n_ctx, diff=diff),
        grid=(n_kv, st // tq),
        in_specs=[
            pl.BlockSpec(memory_space=pltpu.SMEM),
            pl.BlockSpec((tq, g * comps * dq), lambda h, i: (i, c_q + h)),
            pl.BlockSpec((st, comps * dq), lambda h, i: (0, c_k + h)),
            pl.BlockSpec((st, dv), lambda h, i: (0, c_v(h))),
            pl.BlockSpec(aux.shape, lambda h, i: (0, 0)),
        ],
        out_specs=pl.BlockSpec((tq, g * dv), lambda h, i: (i, h)),
        out_shape=jax.ShapeDtypeStruct((st, n_kv * g * dv), BF16),
        scratch_shapes=[pltpu.VMEM((rows, dv), F32), pltpu.VMEM((rows, LANES), F32),
                        pltpu.VMEM((ctx_rows, dv), F32), pltpu.VMEM((ctx_rows, LANES), F32)],
        compiler_params=_cparams("arbitrary", "arbitrary"),
        name="flash_attention",
    )(flag, q_arr, k_arr, v_arr, aux)


MOE_TILE = 256


GATHER_UNROLL = 8


def _start_row_gather(idx_ref, base, n, src_ref, dst_ref, sem):
    def issue(r, carry):
        pltpu.make_async_copy(src_ref.at[pl.ds(idx_ref[base + r], 1)], dst_ref.at[pl.ds(r, 1)], sem).start()
        return carry

    lax.fori_loop(0, n, issue, 0, unroll=GATHER_UNROLL)


def _wait_row_gather(n, src_ref, dst_ref, sem):
    def drain(r, carry):
        pltpu.make_async_copy(src_ref.at[pl.ds(0, 1)], dst_ref.at[pl.ds(0, 1)], sem).wait()
        return carry

    lax.fori_loop(0, n, drain, 0, unroll=GATHER_UNROLL)


def _grouped_moe_kernel(te_ref, nt_ref, tok_ref, h_ref, wgu_ref, bgu_ref, wd_ref, bd_ref, o_ref, x_ref, sem, *, ff):
    t = pl.program_id(0)
    tg = o_ref.shape[0]

    @pl.when(t < nt_ref[0])
    def _():
        _start_row_gather(tok_ref, t * tg, tg, h_ref, x_ref, sem)
        _wait_row_gather(tg, h_ref, x_ref, sem)
        gu = jnp.dot(x_ref[...].astype(BF16), wgu_ref[0], preferred_element_type=F32) + bgu_ref[0]
        glu = jnp.minimum(gu[:, :ff], SWIGLU_LIMIT)
        lin = jnp.clip(gu[:, ff:], -SWIGLU_LIMIT, SWIGLU_LIMIT)
        act = glu * (1.0 / (1.0 + jnp.exp(-SWIGLU_ALPHA * glu))) * (lin + 1.0)
        o_ref[...] = jnp.dot(act.astype(BF16), wd_ref[0], preferred_element_type=F32) + bd_ref[0]

    @pl.when(t >= nt_ref[0])
    def _():
        o_ref[...] = jnp.zeros_like(o_ref)


def grouped_moe(h, tok, tile_expert, n_tiles_used, w_gu, b_gu, w_down, b_down):
    p = tok.shape[0]
    d = h.shape[1]
    n_exp, _, ff2 = w_gu.shape
    ff = ff2 // 2
    tg = MOE_TILE
    return pl.pallas_call(
        functools.partial(_grouped_moe_kernel, ff=ff),
        grid_spec=pltpu.PrefetchScalarGridSpec(
            num_scalar_prefetch=3,
            grid=(p // tg,),
            in_specs=[
                pl.BlockSpec(memory_space=pl.ANY),
                pl.BlockSpec((1, d, ff2), lambda t, te, nt, tok: (te[t], 0, 0)),
                pl.BlockSpec((1, 1, ff2), lambda t, te, nt, tok: (te[t], 0, 0)),
                pl.BlockSpec((1, ff, d), lambda t, te, nt, tok: (te[t], 0, 0)),
                pl.BlockSpec((1, 1, d), lambda t, te, nt, tok: (te[t], 0, 0)),
            ],
            out_specs=pl.BlockSpec((tg, d), lambda t, te, nt, tok: (t, 0)),
            scratch_shapes=[pltpu.VMEM((tg, d), F32), pltpu.SemaphoreType.DMA(())],
        ),
        out_shape=jax.ShapeDtypeStruct((p, d), F32),
        compiler_params=_cparams("arbitrary"),
        name="grouped_moe",
    )(tile_expert, n_tiles_used, tok, h, w_gu, b_gu.reshape(n_exp, 1, ff2), w_down, b_down.reshape(n_exp, 1, d))


COMBINE_TILE = 128


def _combine_kernel(pos_ref, route_ref, ys_ref, o_ref, buf_ref, sem, *, st):
    tm = o_ref.shape[0]
    base = pl.program_id(0) * tm
    for k in range(TOP_K):
        _start_row_gather(pos_ref, k * st + base, tm, ys_ref, buf_ref.at[k], sem)
    _wait_row_gather(TOP_K * tm, ys_ref, buf_ref.at[0], sem)
    route = route_ref[...]
    acc = None
    for k in range(TOP_K):
        term = route[:, TOP_K + k:TOP_K + k + 1] * buf_ref[k]
        acc = term if acc is None else acc + term
    o_ref[...] = acc.astype(o_ref.dtype)


def moe_combine(route, ys, pos):
    st = route.shape[0]
    d = ys.shape[1]
    tm = COMBINE_TILE
    return pl.pallas_call(
        functools.partial(_combine_kernel, st=st),
        grid_spec=pltpu.PrefetchScalarGridSpec(
            num_scalar_prefetch=1,
            grid=(st // tm,),
            in_specs=[pl.BlockSpec((tm, LANES), lambda i, pos: (i, 0)), pl.BlockSpec(memory_space=pl.ANY)],
            out_specs=pl.BlockSpec((tm, d), lambda i, pos: (i, 0)),
            scratch_shapes=[pltpu.VMEM((TOP_K, tm, d), F32), pltpu.SemaphoreType.DMA(())],
        ),
        out_shape=jax.ShapeDtypeStruct((st, d), BF16),
        compiler_params=_cparams("arbitrary"),
        name="moe_combine",
    )(pos, route, ys)


def moe_sparse(h, route, w_gu, b_gu, w_down, b_down):
    st = h.shape[0]
    n_exp = w_gu.shape[0]
    tg = MOE_TILE
    ids = route[:, :TOP_K].astype(jnp.int32)
    onehot = (ids[:, :, None] == jnp.arange(n_exp)[None, None, :]).any(axis=1)
    counts = jnp.sum(onehot, axis=0, dtype=jnp.int32)
    tiles = (counts + tg - 1) // tg
    tile_end = jnp.cumsum(tiles)
    row_off = (tile_end - tiles) * tg
    rank = jnp.cumsum(onehot.astype(jnp.int32), axis=0) - 1
    pos = row_off[ids] + jnp.take_along_axis(rank, ids, axis=1)
    n_tiles = (TOP_K * st + n_exp * (tg - 1)) // tg
    tok = jnp.zeros((n_tiles * tg,), jnp.int32).at[pos.reshape(-1)].set(
        jnp.repeat(jnp.arange(st, dtype=jnp.int32), TOP_K), unique_indices=True)
    tile_expert = jnp.minimum(jnp.searchsorted(tile_end, jnp.arange(n_tiles), side="right"),
                              n_exp - 1).astype(jnp.int32)
    ys = grouped_moe(h, tok, tile_expert, tile_end[-1:].astype(jnp.int32), w_gu, b_gu, w_down, b_down)
    return moe_combine(route, ys, pos.T.reshape(-1).astype(jnp.int32))


def _rope_tables(n_ctx, n_lat, dim):
    quarter = dim // 4
    inv_freq = ROPE_THETA ** (-jnp.arange(quarter, dtype=F32) / quarter)
    t = jnp.arange(n_lat)
    row = (t // GRID_W).astype(F32)
    col = (t % GRID_W).astype(F32)
    ang = jnp.stack([row[:, None] * inv_freq, col[:, None] * inv_freq], axis=1)
    ang = jnp.broadcast_to(ang[:, :, None, :], (n_lat, 2, 2, quarter)).reshape(n_lat, dim)
    cos, sin = jnp.cos(ang), jnp.sin(ang)
    first_half = (jnp.arange(dim) % (2 * quarter)) < quarter
    sin_lo = jnp.where(first_half, -sin, 0.0)
    sin_hi = jnp.where(first_half, 0.0, sin)

    def full(tab, fill):
        tab = jnp.pad(tab, ((0, 0), (0, LANES - dim)), constant_values=fill)
        return jnp.concatenate([jnp.full((n_ctx, LANES), fill, F32), tab], axis=0)

    return full(cos, 1.0), full(sin_lo, 0.0), full(sin_hi, 0.0)


def _pad_cols(w, n):
    return jnp.pad(w, ((0, 0), (0, n - w.shape[1])))


def _streams(v_ctx, v_lat):
    return jnp.stack([v_ctx, v_lat], axis=0)[:, None, :]


def _even_mixer(h, rope128, w_in, w_out, a_qn, a_kn, a_sink, b_qn, b_kn, b_lam, b_subln, lam_init, d):
    a_heads = d // (2 * HEAD_DIM)
    a_kv = a_heads // 4
    b_heads = d // (4 * HEAD_DIM)
    b_kv = b_heads // 2
    ga, gb = a_heads // a_kv, b_heads // b_kv
    scale = HEAD_DIM ** -0.5 * LOG2E
    n_qa, n_qb, n_ka, n_va, n_kb, n_vb = a_heads, 2 * b_heads, a_kv, a_kv, 2 * b_kv, 2 * b_kv
    c_qb = n_qa
    c_ka = c_qb + n_qb
    c_va = c_ka + n_ka
    c_kb = c_va + n_va
    c_vb = c_kb + n_kb
    p = matmul([h], [w_in.astype(BF16)], tn=512)
    gains = jnp.stack([a_qn, b_qn, a_kn, b_kn], axis=0).astype(F32)
    pn = prep_heads(p, rope128, gains,
                    segs=((0, c_qb, 0, scale), (c_qb, c_ka, 1, scale), (c_ka, c_va, 2, 1.0), (c_kb, c_vb, 3, 1.0)),
                    copies=((c_va, c_kb), (c_vb, c_vb + n_vb)))
    sink = jnp.broadcast_to((a_sink.astype(F32) * LOG2E).reshape(a_kv, ga, 1), (a_kv, ga, LANES))
    ya = window_attention(pn, sink, a_kv, ga, 0, c_ka, c_va)
    lf = b_lam.astype(F32)
    lam = jnp.exp(jnp.sum(lf[0] * lf[1])) - jnp.exp(jnp.sum(lf[2] * lf[3])) + lam_init
    dvb = 2 * HEAD_DIM
    aux = jnp.stack([jnp.full((dvb,), lam, F32), b_subln.astype(F32) * (1.0 - lam_init)], axis=0)
    flag = _score_bound_flag(b_qn, b_kn, HEAD_DIM, scale)
    yb = flash_attention(pn, pn, pn, aux, flag, n_kv=b_kv, g=gb, comps=2, dq=HEAD_DIM, dv=dvb,
                         c_q=c_qb * HEAD_DIM // (gb * 2 * HEAD_DIM), c_k=c_kb // 2, c_v=lambda hh: c_vb // 2 + hh,
                         tk=1024, diff=True)
    w_out = w_out.astype(BF16)
    na = a_heads * HEAD_DIM
    return matmul([ya, yb], [w_out[:na], w_out[na:]], tn=512)


def _odd_mixer(h, rope128, rope64, w_in, w_out, c_qn, c_kn, d_qa_norm, d_kva_norm, d_wq_up, d_wkv_up, d_qn, d_kn, d):
    st = h.shape[0]
    c_heads = d // (2 * HEAD_DIM)
    c_kv = c_heads // 4
    gc = c_heads // c_kv
    d_heads = d // (2 * HEAD_DIM)
    q_rank = d_wq_up.shape[0]
    kv_rank = d_wkv_up.shape[0]
    n_qc, n_qa, n_kc, n_vc, n_kva = c_heads, q_rank // LANES, c_kv, c_kv, kv_rank // LANES
    c_qa = n_qc
    c_kc = c_qa + n_qa
    c_vc = c_kc + n_kc
    c_kva = c_vc + n_vc
    c_kr = c_kva + n_kva
    n_in = (c_kr + 1) * LANES
    tn = 768
    n_pad = -(-n_in // tn) * tn
    p = matmul([h], [_pad_cols(w_in, n_pad).astype(BF16)], tn=tn)
    tm = ROW_TILE
    tab = pl.BlockSpec((tm, LANES), lambda i: (i, 0))
    gains = jnp.stack([c_qn, c_kn], axis=0).astype(F32)
    qk, qa, kva = pl.pallas_call(
        functools.partial(_prep_odd1_kernel, n_q=n_qc, n_k=n_kc, c_k=c_kc, c_qa=c_qa, n_qa=n_qa, c_kva=c_kva,
                          n_kva=n_kva, q_scale=HEAD_DIM ** -0.5 * LOG2E),
        grid=(st // tm,),
        in_specs=[pl.BlockSpec((tm, n_pad), lambda i: (i, 0)), tab, tab, tab,
                  pl.BlockSpec((2, LANES), lambda i: (0, 0)),
                  pl.BlockSpec((1, q_rank), lambda i: (0, 0)),
                  pl.BlockSpec((1, kv_rank), lambda i: (0, 0))],
        out_specs=[pl.BlockSpec((tm, (n_qc + n_kc) * LANES), lambda i: (i, 0)),
                   pl.BlockSpec((tm, q_rank), lambda i: (i, 0)),
                   pl.BlockSpec((tm, kv_rank), lambda i: (i, 0))],
        out_shape=[jax.ShapeDtypeStruct((st, (n_qc + n_kc) * LANES), BF16),
                   jax.ShapeDtypeStruct((st, q_rank), BF16),
                   jax.ShapeDtypeStruct((st, kv_rank), BF16)],
        compiler_params=_cparams("arbitrary"),
        name="prep_odd1",
    )(p, *rope128, gains, d_qa_norm.astype(F32)[None, :], d_kva_norm.astype(F32)[None, :])
    wq = jnp.pad(d_wq_up.reshape(q_rank, d_heads, D_QK), ((0, 0), (0, 0), (0, D_PAD - D_QK)))
    qd_raw = matmul([qa], [wq.reshape(q_rank, d_heads * D_PAD).astype(BF16)], tn=512)
    kv = matmul([kva], [d_wkv_up.astype(BF16)], tn=512)
    pad_gain = lambda gvec: jnp.pad(gvec.astype(F32), (0, D_PAD - D_QK))[None, :]
    qd, kd = pl.pallas_call(
        functools.partial(_prep_odd2_kernel, n_heads=d_heads, q_scale=D_QK ** -0.5 * LOG2E),
        grid=(st // tm,),
        in_specs=[pl.BlockSpec((tm, d_heads * D_PAD), lambda i: (i, 0)),
                  pl.BlockSpec((tm, d_heads * D_PAD), lambda i: (i, 0)),
                  pl.BlockSpec((tm, LANES), lambda i: (i, c_kr)),
                  tab, tab, tab,
                  pl.BlockSpec((1, D_PAD), lambda i: (0, 0)),
                  pl.BlockSpec((1, D_PAD), lambda i: (0, 0))],
        out_specs=[pl.BlockSpec((tm, d_heads * D_PAD), lambda i: (i, 0)),
                   pl.BlockSpec((tm, d_heads * D_PAD), lambda i: (i, 0))],
        out_shape=[jax.ShapeDtypeStruct((st, d_heads * D_PAD), BF16),
                   jax.ShapeDtypeStruct((st, d_heads * D_PAD), BF16)],
        compiler_params=_cparams("arbitrary"),
        name="prep_odd2",
    )(qd_raw, kv, p, *rope64, pad_gain(d_qn), pad_gain(d_kn))
    aux = jnp.zeros((8, LANES), F32)
    flag_c = _score_bound_flag(c_qn, c_kn, HEAD_DIM, HEAD_DIM ** -0.5 * LOG2E)
    flag_d = _score_bound_flag(d_qn, d_kn, D_QK, D_QK ** -0.5 * LOG2E)
    yc = flash_attention(qk, qk, p, aux, flag_c, n_kv=c_kv, g=gc, comps=1, dq=HEAD_DIM, dv=HEAD_DIM,
                         c_q=0, c_k=n_qc, c_v=lambda hh: c_vc + hh, tk=1024)
    yd = flash_attention(qd, kd, kv, aux, flag_d, n_kv=d_heads, g=1, comps=1, dq=D_PAD, dv=D_V,
                         c_q=0, c_k=0, c_v=lambda hh: 2 * hh + 1, tk=1024,
                         tq=1280 if st % 1280 == 0 else ROW_TILE)
    w_out = w_out.astype(BF16)
    nc = c_heads * HEAD_DIM
    return matmul([yc, yd], [w_out[:nc], w_out[nc:]], tn=512)


def kernel(x, c, ctx, c_ctx, adaln_down, adaln_up, adaln_b, norm_mix, norm_ffn, ev_w_in, ev_w_out, ev_a_qn, ev_a_kn,
           ev_a_sink, ev_b_qn, ev_b_kn, ev_b_lam, ev_b_subln, od_w_in, od_w_out, od_c_qn, od_c_kn, od_d_qa_norm,
           od_d_kva_norm, od_d_wq_up, od_d_wkv_up, od_d_qn, od_d_kn, router_w, router_b, moe_w_gu, moe_b_gu,
           moe_w_down, moe_b_down):
    bsz, seq, d = x.shape
    n_ctx = ctx.shape[1]
    depth = adaln_down.shape[0]
    n_exp = router_w.shape[2]
    assert bsz == 1 and n_ctx == ROW_TILE and seq % ROW_TILE == 0 and n_exp <= LANES

    rope128 = _rope_tables(n_ctx, seq, HEAD_DIM)
    rope64 = _rope_tables(n_ctx, seq, D_ROPE)
    cvecs = jnp.zeros((8, d), F32).at[0].set(c[0]).at[1].set(c_ctx)
    mods = adaln_all(cvecs, adaln_down, adaln_up, adaln_b)

    xs = jnp.concatenate([ctx[0], x[0]], axis=0)
    y_prev, gate_prev = None, None
    for l in range(depth):
        m_lat = mods[l, 0].reshape(N_MOD, d)
        m_ctx = mods[l, 1].reshape(N_MOD, d)
        mult = _streams(norm_mix[l] * (1.0 + m_ctx[1]), norm_mix[l] * (1.0 + m_lat[1]))
        shift = _streams(m_ctx[0], m_lat[0])
        if y_prev is None:
            (h,) = modulate(xs, mult, shift)
        else:
            xs, h = modulate(xs, mult, shift, res=(y_prev, gate_prev))
        j = l // 2
        if l % 2 == 0:
            lam_init = 0.8 - 0.6 * math.exp(-0.3 * l)
            y = _even_mixer(h, rope128, ev_w_in[j], ev_w_out[j], ev_a_qn[j], ev_a_kn[j], ev_a_sink[j], ev_b_qn[j],
                            ev_b_kn[j], ev_b_lam[j], ev_b_subln[j], lam_init, d)
        else:
            y = _odd_mixer(h, rope128, rope64, od_w_in[j], od_w_out[j], od_c_qn[j], od_c_kn[j], od_d_qa_norm[j],
                           od_d_kva_norm[j], od_d_wq_up[j], od_d_wkv_up[j], od_d_qn[j], od_d_kn[j], d)
        mult = _streams(norm_ffn[l] * (1.0 + m_ctx[4]), norm_ffn[l] * (1.0 + m_lat[4]))
        shift = _streams(m_ctx[3], m_lat[3])
        rw = _pad_cols(router_w[l].astype(F32), LANES)
        rb = jnp.pad(router_b[l].astype(F32), (0, LANES - n_exp))[None, :]
        xs, h2, route = modulate(xs, mult, shift, res=(y, _streams(m_ctx[2], m_lat[2])), router=(rw, rb, n_exp))
        y_prev = moe_sparse(h2, route, moe_w_gu[l].astype(BF16), moe_b_gu[l].astype(F32),
                            moe_w_down[l].astype(BF16), moe_b_down[l].astype(F32))
        gate_prev = _streams(m_ctx[5], m_lat[5])
    out = final_residual(xs, y_prev, gate_prev)
    return out[None]
```

```python
import functools
import math

import jax
import jax.numpy as jnp
from jax import lax
from jax.experimental import pallas as pl
from jax.experimental.pallas import tpu as pltpu

F32 = jnp.float32
BF16 = jnp.bfloat16

GRID_W = 64
HEAD_DIM = 128
ROPE_THETA = 10000.0
EPS = 1e-6
N_MOD = 6
D_NOPE = 128
D_ROPE = 64
D_QK = D_NOPE + D_ROPE
D_V = 128
D_PAD = 256
TOP_K = 4
SWIGLU_ALPHA = 1.702
SWIGLU_LIMIT = 7.0
LOG2E = 1.4426950408889634
NEG_BIG = -1e30
SCORE_BOUND = 60.0
BF16_NORM_SLACK = 1.02

LANES = 128
ROW_TILE = 256
VMEM_LIMIT = 56 * 1024 * 1024


def _cparams(*sem):
    return pltpu.CompilerParams(dimension_semantics=sem, vmem_limit_bytes=VMEM_LIMIT)


def _adaln_kernel(c_ref, down_ref, up_ref, b_ref, o_ref, t_ref):
    @pl.when(pl.program_id(1) == 0)
    def _():
        c = c_ref[...]
        a = c * (1.0 / (1.0 + jnp.exp(-c)))
        t_ref[...] = jnp.dot(a, down_ref[0], preferred_element_type=F32, precision=lax.Precision.HIGHEST)

    o_ref[0] = jnp.dot(t_ref[...], up_ref[0], preferred_element_type=F32,
                       precision=lax.Precision.HIGHEST) + b_ref[0]


def adaln_all(cvecs, down, up, bias):
    depth, d, rank = down.shape
    n = up.shape[2]
    tn = 2048
    assert n % tn == 0
    return pl.pallas_call(
        _adaln_kernel,
        grid=(depth, n // tn),
        in_specs=[
            pl.BlockSpec((8, d), lambda l, j: (0, 0)),
            pl.BlockSpec((1, d, rank), lambda l, j: (l, 0, 0)),
            pl.BlockSpec((1, rank, tn), lambda l, j: (l, 0, j)),
            pl.BlockSpec((1, 1, tn), lambda l, j: (l, 0, j)),
        ],
        out_specs=pl.BlockSpec((1, 8, tn), lambda l, j: (l, 0, j)),
        out_shape=jax.ShapeDtypeStruct((depth, 8, n), F32),
        scratch_shapes=[pltpu.VMEM((8, rank), F32)],
        compiler_params=_cparams("arbitrary", "arbitrary"),
        name="adaln",
    )(cvecs, down, up, bias.reshape(depth, 1, n))


def _topk_route(logits, n_exp):
    lane = lax.broadcasted_iota(jnp.int32, logits.shape, 1).astype(F32)
    work = jnp.where(lane < n_exp, logits, -jnp.inf)
    ids = jnp.zeros_like(logits)
    wts = jnp.zeros_like(logits)
    denom = None
    v0 = None
    for k in range(TOP_K):
        m = jnp.max(work, axis=-1, keepdims=True)
        idx = jnp.min(jnp.where(work == m, lane, float(LANES)), axis=-1, keepdims=True)
        if k == 0:
            v0 = m
            e = jnp.ones_like(m)
            denom = e
        else:
            e = jnp.exp(m - v0)
            denom = denom + e
        ids = jnp.where(lane == k, idx, ids)
        wts = jnp.where(lane == TOP_K + k, e, wts)
        work = jnp.where(lane == idx, -jnp.inf, work)
    return ids + wts / denom


def _modulate_kernel(*refs, has_res, n_exp):
    it = iter(refs)
    x_ref = next(it)
    y_ref = next(it) if has_res else None
    gate_ref = next(it) if has_res else None
    mult_ref = next(it)
    shift_ref = next(it)
    rw_ref = next(it) if n_exp else None
    rb_ref = next(it) if n_exp else None
    xo_ref = next(it) if has_res else None
    h_ref = next(it)
    g_ref = next(it) if n_exp else None

    x = x_ref[...]
    if has_res:
        x = x + gate_ref[0] * y_ref[...].astype(F32)
        xo_ref[...] = x
    ms = jnp.mean(x * x, axis=-1, keepdims=True)
    h = x * lax.rsqrt(ms + EPS) * mult_ref[0] + shift_ref[0]
    h_ref[...] = h.astype(h_ref.dtype)
    if n_exp:
        logits = jnp.dot(h, rw_ref[...], preferred_element_type=F32,
                         precision=lax.Precision.HIGHEST) + rb_ref[...]
        g_ref[...] = _topk_route(logits, n_exp)


def modulate(x, mult, shift, res=None, router=None):
    st, d = x.shape
    tm = ROW_TILE
    stream = lambda i: (jnp.minimum(i, 1), 0, 0)
    row = lambda i: (i, 0)
    vec_spec = pl.BlockSpec((1, 1, d), stream)
    args, in_specs = [x], [pl.BlockSpec((tm, d), row)]
    if res is not None:
        args += [res[0], res[1]]
        in_specs += [pl.BlockSpec((tm, d), row), vec_spec]
    args += [mult, shift]
    in_specs += [vec_spec, vec_spec]
    n_exp = 0
    if router is not None:
        args += [router[0], router[1]]
        n_exp = router[2]
        in_specs += [pl.BlockSpec((d, LANES), lambda i: (0, 0)), pl.BlockSpec((1, LANES), lambda i: (0, 0))]
    out_shape, out_specs = [], []
    if res is not None:
        out_shape.append(jax.ShapeDtypeStruct((st, d), F32))
        out_specs.append(pl.BlockSpec((tm, d), row))
    out_shape.append(jax.ShapeDtypeStruct((st, d), BF16 if router is None else F32))
    out_specs.append(pl.BlockSpec((tm, d), row))
    if router is not None:
        out_shape.append(jax.ShapeDtypeStruct((st, LANES), F32))
        out_specs.append(pl.BlockSpec((tm, LANES), row))
    return pl.pallas_call(
        functools.partial(_modulate_kernel, has_res=res is not None, n_exp=n_exp),
        grid=(st // tm,),
        in_specs=in_specs,
        out_specs=out_specs,
        out_shape=out_shape,
        compiler_params=_cparams("arbitrary"),
        name="modulate",
    )(*args)


def _final_residual_kernel(x_ref, y_ref, gate_ref, o_ref):
    o_ref[...] = x_ref[...] + gate_ref[0] * y_ref[...].astype(F32)


def final_residual(x, y, gate):
    st, d = x.shape
    tm = ROW_TILE
    n = st // tm - 1
    return pl.pallas_call(
        _final_residual_kernel,
        grid=(n,),
        in_specs=[
            pl.BlockSpec((tm, d), lambda i: (i + 1, 0)),
            pl.BlockSpec((tm, d), lambda i: (i + 1, 0)),
            pl.BlockSpec((1, 1, d), lambda i: (1, 0, 0)),
        ],
        out_specs=pl.BlockSpec((tm, d), lambda i: (i, 0)),
        out_shape=jax.ShapeDtypeStruct((n * tm, d), F32),
        compiler_params=_cparams("arbitrary"),
        name="final_residual",
    )(x, y, gate)


def _mm_kernel(*refs, n_pairs):
    o_ref = refs[-1]
    acc = None
    for p in range(n_pairs):
        t = jnp.dot(refs[p][...], refs[n_pairs + p][...], preferred_element_type=F32)
        acc = t if acc is None else acc + t
    o_ref[...] = acc.astype(o_ref.dtype)


def _row_block(m):
    for tm in (1280, 1024, 512, 256):
        if m % tm == 0:
            return tm
    raise ValueError(m)


def matmul(a_list, b_list, tn, out_dtype=BF16):
    m = a_list[0].shape[0]
    n = b_list[0].shape[1]
    tm = _row_block(m)
    while n % tn:
        tn -= LANES
    in_specs = [pl.BlockSpec((tm, a.shape[1]), lambda i, j: (i, 0)) for a in a_list]
    in_specs += [pl.BlockSpec((b.shape[0], tn), lambda i, j: (0, j)) for b in b_list]
    return pl.pallas_call(
        functools.partial(_mm_kernel, n_pairs=len(a_list)),
        grid=(m // tm, n // tn),
        in_specs=in_specs,
        out_specs=pl.BlockSpec((tm, tn), lambda i, j: (i, j)),
        out_shape=jax.ShapeDtypeStruct((m, n), out_dtype),
        compiler_params=_cparams("arbitrary", "arbitrary"),
        name="matmul",
    )(*a_list, *b_list)


def _rope(y, cos, sin_lo, sin_hi, quarter):
    left = pltpu.roll(y, LANES - quarter, 1)
    right = pltpu.roll(y, quarter, 1)
    return y * cos + left * sin_lo + right * sin_hi


def _head_norm(x, gain, n_valid):
    ss = jnp.sum(x * x, axis=-1, keepdims=True)
    return x * lax.rsqrt(ss * (1.0 / n_valid) + EPS) * gain


def _prep_even_kernel(p_ref, cos_ref, slo_ref, shi_ref, gains_ref, o_ref, *, segs, n_copy_from):
    cos, slo, shi = cos_ref[...], slo_ref[...], shi_ref[...]
    for (c0, c1, gi, scale) in segs:
        gain = gains_ref[gi:gi + 1, :]
        for c in range(c0, c1):
            x = p_ref[:, c * LANES:(c + 1) * LANES].astype(F32)
            y = _rope(_head_norm(x, gain, HEAD_DIM), cos, slo, shi, HEAD_DIM // 4)
            if scale != 1.0:
                y = y * scale
            o_ref[:, c * LANES:(c + 1) * LANES] = y.astype(BF16)
    for (c0, c1) in n_copy_from:
        o_ref[:, c0 * LANES:c1 * LANES] = p_ref[:, c0 * LANES:c1 * LANES]


def prep_heads(p, rope128, gains, segs, copies):
    st, n = p.shape
    tm = ROW_TILE
    tab = pl.BlockSpec((tm, LANES), lambda i: (i, 0))
    return pl.pallas_call(
        functools.partial(_prep_even_kernel, segs=segs, n_copy_from=copies),
        grid=(st // tm,),
        in_specs=[pl.BlockSpec((tm, n), lambda i: (i, 0)), tab, tab, tab,
                  pl.BlockSpec(gains.shape, lambda i: (0, 0))],
        out_specs=pl.BlockSpec((tm, n), lambda i: (i, 0)),
        out_shape=jax.ShapeDtypeStruct((st, n), BF16),
        compiler_params=_cparams("arbitrary"),
        name="prep_heads",
    )(p, *rope128, gains)


def _prep_odd1_kernel(p_ref, cos_ref, slo_ref, shi_ref, gains_ref, gqa_ref, gkva_ref,
                      qk_ref, qa_ref, kva_ref, *, n_q, n_k, c_k, c_qa, n_qa, c_kva, n_kva, q_scale):
    cos, slo, shi = cos_ref[...], slo_ref[...], shi_ref[...]
    for c in range(n_q):
        x = p_ref[:, c * LANES:(c + 1) * LANES].astype(F32)
        y = _rope(_head_norm(x, gains_ref[0:1, :], HEAD_DIM), cos, slo, shi, HEAD_DIM // 4) * q_scale
        qk_ref[:, c * LANES:(c + 1) * LANES] = y.astype(BF16)
    for c in range(n_k):
        x = p_ref[:, (c_k + c) * LANES:(c_k + c + 1) * LANES].astype(F32)
        y = _rope(_head_norm(x, gains_ref[1:2, :], HEAD_DIM), cos, slo, shi, HEAD_DIM // 4)
        qk_ref[:, (n_q + c) * LANES:(n_q + c + 1) * LANES] = y.astype(BF16)
    xa = p_ref[:, c_qa * LANES:(c_qa + n_qa) * LANES].astype(F32)
    qa_ref[...] = _head_norm(xa, gqa_ref[...], n_qa * LANES).astype(BF16)
    xk = p_ref[:, c_kva * LANES:(c_kva + n_kva) * LANES].astype(F32)
    kva_ref[...] = _head_norm(xk, gkva_ref[...], n_kva * LANES).astype(BF16)


def _prep_odd2_kernel(qd_ref, kv_ref, kr_ref, cos_ref, slo_ref, shi_ref, gq_ref, gk_ref,
                      qo_ref, ko_ref, *, n_heads, q_scale):
    cos, slo, shi = cos_ref[...], slo_ref[...], shi_ref[...]
    gq_n, gq_r = gq_ref[:, :LANES], gq_ref[:, LANES:]
    gk_n, gk_r = gk_ref[:, :LANES], gk_ref[:, LANES:]
    kr = kr_ref[...].astype(F32)
    kr_ss = jnp.sum(kr * kr, axis=-1, keepdims=True)
    for h in range(n_heads):
        b = h * D_PAD
        qn = qd_ref[:, b:b + LANES].astype(F32)
        qr = qd_ref[:, b + LANES:b + D_PAD].astype(F32)
        ss = jnp.sum(qn * qn, axis=-1, keepdims=True) + jnp.sum(qr * qr, axis=-1, keepdims=True)
        r = lax.rsqrt(ss * (1.0 / D_QK) + EPS) * q_scale
        qo_ref[:, b:b + LANES] = (qn * r * gq_n).astype(BF16)
        qo_ref[:, b + LANES:b + D_PAD] = _rope(qr * r * gq_r, cos, slo, shi, D_ROPE // 4).astype(BF16)
        kn = kv_ref[:, b:b + LANES].astype(F32)
        ss = jnp.sum(kn * kn, axis=-1, keepdims=True) + kr_ss
        r = lax.rsqrt(ss * (1.0 / D_QK) + EPS)
        ko_ref[:, b:b + LANES] = (kn * r * gk_n).astype(BF16)
        ko_ref[:, b + LANES:b + D_PAD] = _rope(kr * r * gk_r, cos, slo, shi, D_ROPE // 4).astype(BF16)


def _stack_heads(q_ref, g, d):
    return jnp.concatenate([q_ref[:, i * d:(i + 1) * d] for i in range(g)], axis=0)


def _qk(q, k):
    return lax.dot_general(q, k, (((1,), (1,)), ((), ())), preferred_element_type=F32)


def _window_attn_kernel(q_ref, kp_ref, km_ref, kn_ref, kc_ref, vp_ref, vm_ref, vn_ref, vc_ref, sink_ref,
                        o_ref, *, g, st):
    i = pl.program_id(1)
    tq = q_ref.shape[0]
    half = tq // 2
    qs = _stack_heads(q_ref, g, HEAD_DIM)
    k_loc = jnp.concatenate([kp_ref[...], km_ref[...], kn_ref[...]], axis=0)
    v_loc = jnp.concatenate([vp_ref[...], vm_ref[...], vn_ref[...]], axis=0)
    s_loc = _qk(qs, k_loc)
    s_ctx = _qk(qs, kc_ref[...])
    a = lax.broadcasted_iota(jnp.int32, s_loc.shape, 0) & (tq - 1)
    j = lax.broadcasted_iota(jnp.int32, s_loc.shape, 1)
    diff = j - half - a
    krow = i * tq - half + j
    valid = (jnp.abs(diff) <= half) & (krow >= tq) & (krow < st) & (i >= 1)
    s_loc = jnp.where(valid, s_loc, NEG_BIG)
    sink = jnp.concatenate([jnp.broadcast_to(sink_ref[0, h:h + 1, 0:1], (tq, 1)) for h in range(g)], axis=0)
    m = jnp.maximum(jnp.maximum(jnp.max(s_loc, axis=-1, keepdims=True),
                                jnp.max(s_ctx, axis=-1, keepdims=True)), sink)
    p_loc = jnp.exp2(s_loc - m)
    p_ctx = jnp.exp2(s_ctx - m)
    l = jnp.sum(p_loc, axis=-1, keepdims=True) + jnp.sum(p_ctx, axis=-1, keepdims=True) + jnp.exp2(sink - m)
    o = (jnp.dot(p_loc.astype(BF16), v_loc, preferred_element_type=F32)
         + jnp.dot(p_ctx.astype(BF16), vc_ref[...], preferred_element_type=F32)) / l
    for h in range(g):
        o_ref[:, h * HEAD_DIM:(h + 1) * HEAD_DIM] = o[h * tq:(h + 1) * tq].astype(BF16)


def window_attention(pn, sink, n_kv, g, c_q, c_k, c_v):
    st = pn.shape[0]
    tq = ROW_TILE
    half = tq // 2
    nh = st // half
    q_spec = pl.BlockSpec((tq, g * HEAD_DIM), lambda h, i: (i, c_q // g + h))

    def kv_specs(c0):
        return [
            pl.BlockSpec((half, HEAD_DIM), lambda h, i: (jnp.maximum(2 * i - 1, 0), c0 + h)),
            pl.BlockSpec((tq, HEAD_DIM), lambda h, i: (i, c0 + h)),
            pl.BlockSpec((half, HEAD_DIM), lambda h, i: (jnp.minimum(2 * i + 2, nh - 1), c0 + h)),
            pl.BlockSpec((tq, HEAD_DIM), lambda h, i: (0, c0 + h)),
        ]

    return pl.pallas_call(
        functools.partial(_window_attn_kernel, g=g, st=st),
        grid=(n_kv, st // tq),
        in_specs=[q_spec] + kv_specs(c_k) + kv_specs(c_v)
        + [pl.BlockSpec((1, g, LANES), lambda h, i: (h, 0, 0))],
        out_specs=pl.BlockSpec((tq, g * HEAD_DIM), lambda h, i: (i, h)),
        out_shape=jax.ShapeDtypeStruct((st, n_kv * g * HEAD_DIM), BF16),
        compiler_params=_cparams("arbitrary", "arbitrary"),
        name="window_attention",
    )(pn, pn, pn, pn, pn, pn, pn, pn, pn, sink)


def _flash_kernel(flag_ref, q_ref, k_ref, v_ref, aux_ref, o_ref, acc_ref, l_ref, accc_ref, lc_ref, *,
                  g, comps, dq, dv, tk, n_ctx, diff):
    qi = pl.program_id(1)
    tq = q_ref.shape[0]
    n_keys = k_ref.shape[0]
    n_lat = (n_keys - n_ctx) // tk
    ctx_only_tile = tq == n_ctx
    n_iter = jnp.where(qi == 0, 0, n_lat) if ctx_only_tile else n_lat
    n_stacked = g * comps
    qs = [jnp.concatenate([q_ref[:, (gi * comps + c) * dq:(gi * comps + c + 1) * dq] for gi in range(g)], axis=0)
          for c in range(comps)]

    def scores(start, size):
        return jnp.concatenate(
            [_qk(qs[c], k_ref[pl.ds(start, size), c * dq:(c + 1) * dq]) for c in range(comps)], axis=0)

    def chunk_start(j):
        return pl.multiple_of(n_ctx + j * tk, math.gcd(n_ctx, tk))

    def keep_context_rows(l_is_partial):
        if ctx_only_tile:
            return

        @pl.when(qi == 0)
        def _():
            for h in range(n_stacked):
                accc_ref[h * n_ctx:(h + 1) * n_ctx] = acc_ref[h * tq:h * tq + n_ctx]
                lrows = l_ref[h * tq:h * tq + n_ctx]
                if l_is_partial:
                    lrows = jnp.broadcast_to(jnp.sum(lrows, axis=-1, keepdims=True), lrows.shape)
                lc_ref[h * n_ctx:(h + 1) * n_ctx] = lrows

    @pl.when(flag_ref[0] == 1)
    def _bounded():
        def step(start, size, first):
            p = jnp.exp2(scores(start, size))
            psum = p[:, :LANES]
            for b in range(1, size // LANES):
                psum = psum + p[:, b * LANES:(b + 1) * LANES]
            pv = jnp.dot(p.astype(BF16), v_ref[pl.ds(start, size), :], preferred_element_type=F32)
            if first:
                acc_ref[...] = pv
                l_ref[...] = psum
            else:
                acc_ref[...] += pv
                l_ref[...] += psum

        step(0, n_ctx, True)
        keep_context_rows(True)

        def body(j, carry):
            step(chunk_start(j), tk, False)
            return carry

        lax.fori_loop(0, n_iter, body, 0)
        l_ref[...] = jnp.broadcast_to(jnp.sum(l_ref[...], axis=-1, keepdims=True), l_ref.shape)

    @pl.when(flag_ref[0] != 1)
    def _online():
        def step(start, size, m, l, first):
            s = scores(start, size)
            m_cur = jnp.max(s, axis=-1, keepdims=True)
            m_new = m_cur if first else jnp.maximum(m, m_cur)
            p = jnp.exp2(s - m_new)
            pv = jnp.dot(p.astype(BF16), v_ref[pl.ds(start, size), :], preferred_element_type=F32)
            if first:
                l_new = jnp.sum(p, axis=-1, keepdims=True)
                acc_ref[...] = pv
            else:
                alpha = jnp.exp2(m - m_new)
                l_new = alpha * l + jnp.sum(p, axis=-1, keepdims=True)
                acc_ref[...] = alpha * acc_ref[...] + pv
            return m_new, l_new

        m, l = step(0, n_ctx, None, None, True)
        l_ref[...] = jnp.broadcast_to(l, l_ref.shape)
        keep_context_rows(False)
        m, l = lax.fori_loop(0, n_iter, lambda j, c: step(chunk_start(j), tk, c[0], c[1], False), (m, l))
        l_ref[...] = jnp.broadcast_to(l, l_ref.shape)

    def finalize(acc, l, rows):
        o = acc / l
        if diff:
            lam = aux_ref[0:1, 0:1]
            for gi in range(g):
                y = o[gi * rows:(gi + 1) * rows] - lam * o[(g + gi) * rows:(g + gi + 1) * rows]
                ss = jnp.mean(y * y, axis=-1, keepdims=True)
                y = y * lax.rsqrt(ss + EPS) * aux_ref[1:2, :]
                o_ref[0:rows, gi * dv:(gi + 1) * dv] = y.astype(BF16)
        else:
            for gi in range(g):
                o_ref[0:rows, gi * dv:(gi + 1) * dv] = o[gi * rows:(gi + 1) * rows].astype(BF16)

    finalize(acc_ref[...], l_ref[:, 0:1], tq)
    if not ctx_only_tile:
        @pl.when(qi == 0)
        def _():
            finalize(accc_ref[...], lc_ref[:, 0:1], n_ctx)


def _score_bound_flag(q_gain, k_gain, n_norm, scale):
    bound = n_norm * jnp.max(jnp.abs(q_gain)) * jnp.max(jnp.abs(k_gain)) * scale * BF16_NORM_SLACK
    return (bound <= SCORE_BOUND).astype(jnp.int32).reshape(1)


def flash_attention(q_arr, k_arr, v_arr, aux, flag, *, n_kv, g, comps, dq, dv, c_q, c_k, c_v, tk, tq=ROW_TILE,
                    diff=False):
    st = q_arr.shape[0]
    n_ctx = ROW_TILE
    assert st % tq == 0 and tq % n_ctx == 0
    while (st - n_ctx) % tk:
        tk //= 2
    rows = g * comps * tq
    ctx_rows = g * comps * n_ctx if tq != n_ctx else 8
    return pl.pallas_call(
        functools.partial(_flash_kernel, g=g, comps=comps, dq=dq, dv=dv, tk=tk, n_ctx=n_ctx, diff=diff),
        grid=(n_kv, st // tq),
        in_specs=[
            pl.BlockSpec(memory_space=pltpu.SMEM),
            pl.BlockSpec((tq, g * comps * dq), lambda h, i: (i, c_q + h)),
            pl.BlockSpec((st, comps * dq), lambda h, i: (0, c_k + h)),
            pl.BlockSpec((st, dv), lambda h, i: (0, c_v(h))),
            pl.BlockSpec(aux.shape, lambda h, i: (0, 0)),
        ],
        out_specs=pl.BlockSpec((tq, g * dv), lambda h, i: (i, h)),
        out_shape=jax.ShapeDtypeStruct((st, n_kv * g * dv), BF16),
        scratch_shapes=[pltpu.VMEM((rows, dv), F32), pltpu.VMEM((rows, LANES), F32),
                        pltpu.VMEM((ctx_rows, dv), F32), pltpu.VMEM((ctx_rows, LANES), F32)],
        compiler_params=_cparams("arbitrary", "arbitrary"),
        name="flash_attention",
    )(flag, q_arr, k_arr, v_arr, aux)


MOE_TILE = 256


GATHER_UNROLL = 8


def _start_row_gather(idx_ref, base, n, src_ref, dst_ref, sem):
    def issue(r, carry):
        pltpu.make_async_copy(src_ref.at[pl.ds(idx_ref[base + r], 1)], dst_ref.at[pl.ds(r, 1)], sem).start()
        return carry

    lax.fori_loop(0, n, issue, 0, unroll=GATHER_UNROLL)


def _wait_row_gather(n, src_ref, dst_ref, sem):
    def drain(r, carry):
        pltpu.make_async_copy(src_ref.at[pl.ds(0, 1)], dst_ref.at[pl.ds(0, 1)], sem).wait()
        return carry

    lax.fori_loop(0, n, drain, 0, unroll=GATHER_UNROLL)


def _grouped_moe_kernel(te_ref, nt_ref, tok_ref, h_ref, wgu_ref, bgu_ref, wd_ref, bd_ref, o_ref, x_ref, sem, *, ff):
    t = pl.program_id(0)
    tg = o_ref.shape[0]
    slot = lax.rem(t, 2)

    @pl.when(t < nt_ref[0])
    def _():
        @pl.when(t == 0)
        def _():
            _start_row_gather(tok_ref, 0, tg, h_ref, x_ref.at[0], sem.at[0])

        @pl.when(t + 1 < nt_ref[0])
        def _():
            _start_row_gather(tok_ref, (t + 1) * tg, tg, h_ref, x_ref.at[1 - slot], sem.at[1 - slot])

        _wait_row_gather(tg, h_ref, x_ref.at[slot], sem.at[slot])
        gu = jnp.dot(x_ref[slot].astype(BF16), wgu_ref[0], preferred_element_type=F32) + bgu_ref[0]
        glu = jnp.minimum(gu[:, :ff], SWIGLU_LIMIT)
        lin = jnp.clip(gu[:, ff:], -SWIGLU_LIMIT, SWIGLU_LIMIT)
        act = glu * (1.0 / (1.0 + jnp.exp(-SWIGLU_ALPHA * glu))) * (lin + 1.0)
        o_ref[...] = jnp.dot(act.astype(BF16), wd_ref[0], preferred_element_type=F32) + bd_ref[0]

    @pl.when(t >= nt_ref[0])
    def _():
        o_ref[...] = jnp.zeros_like(o_ref)


def grouped_moe(h, tok, tile_expert, n_tiles_used, w_gu, b_gu, w_down, b_down):
    p = tok.shape[0]
    d = h.shape[1]
    n_exp, _, ff2 = w_gu.shape
    ff = ff2 // 2
    tg = MOE_TILE
    return pl.pallas_call(
        functools.partial(_grouped_moe_kernel, ff=ff),
        grid_spec=pltpu.PrefetchScalarGridSpec(
            num_scalar_prefetch=3,
            grid=(p // tg,),
            in_specs=[
                pl.BlockSpec(memory_space=pl.ANY),
                pl.BlockSpec((1, d, ff2), lambda t, te, nt, tok: (te[t], 0, 0)),
                pl.BlockSpec((1, 1, ff2), lambda t, te, nt, tok: (te[t], 0, 0)),
                pl.BlockSpec((1, ff, d), lambda t, te, nt, tok: (te[t], 0, 0)),
                pl.BlockSpec((1, 1, d), lambda t, te, nt, tok: (te[t], 0, 0)),
            ],
            out_specs=pl.BlockSpec((tg, d), lambda t, te, nt, tok: (t, 0)),
            scratch_shapes=[pltpu.VMEM((2, tg, d), F32), pltpu.SemaphoreType.DMA((2,))],
        ),
        out_shape=jax.ShapeDtypeStruct((p, d), F32),
        compiler_params=_cparams("arbitrary"),
        name="grouped_moe",
    )(tile_expert, n_tiles_used, tok, h, w_gu, b_gu.reshape(n_exp, 1, ff2), w_down, b_down.reshape(n_exp, 1, d))


COMBINE_TILE = 128


def _combine_kernel(pos_ref, route_ref, ys_ref, o_ref, buf_ref, sem, *, st):
    tm = o_ref.shape[0]
    i = pl.program_id(0)
    slot = lax.rem(i, 2)

    def start(step, s):
        for k in range(TOP_K):
            _start_row_gather(pos_ref, k * st + step * tm, tm, ys_ref, buf_ref.at[s, k], sem.at[s])

    @pl.when(i == 0)
    def _():
        start(0, 0)

    @pl.when(i + 1 < pl.num_programs(0))
    def _():
        start(i + 1, 1 - slot)

    _wait_row_gather(TOP_K * tm, ys_ref, buf_ref.at[slot, 0], sem.at[slot])
    route = route_ref[...]
    acc = None
    for k in range(TOP_K):
        term = route[:, TOP_K + k:TOP_K + k + 1] * buf_ref[slot, k]
        acc = term if acc is None else acc + term
    o_ref[...] = acc.astype(o_ref.dtype)


def moe_combine(route, ys, pos):
    st = route.shape[0]
    d = ys.shape[1]
    tm = COMBINE_TILE
    return pl.pallas_call(
        functools.partial(_combine_kernel, st=st),
        grid_spec=pltpu.PrefetchScalarGridSpec(
            num_scalar_prefetch=1,
            grid=(st // tm,),
            in_specs=[pl.BlockSpec((tm, LANES), lambda i, pos: (i, 0)), pl.BlockSpec(memory_space=pl.ANY)],
            out_specs=pl.BlockSpec((tm, d), lambda i, pos: (i, 0)),
            scratch_shapes=[pltpu.VMEM((2, TOP_K, tm, d), F32), pltpu.SemaphoreType.DMA((2,))],
        ),
        out_shape=jax.ShapeDtypeStruct((st, d), BF16),
        compiler_params=_cparams("arbitrary"),
        name="moe_combine",
    )(pos, route, ys)


def moe_sparse(h, route, w_gu, b_gu, w_down, b_down):
    st = h.shape[0]
    n_exp = w_gu.shape[0]
    tg = MOE_TILE
    ids = route[:, :TOP_K].astype(jnp.int32)
    onehot = (ids[:, :, None] == jnp.arange(n_exp)[None, None, :]).any(axis=1)
    counts = jnp.sum(onehot, axis=0, dtype=jnp.int32)
    tiles = (counts + tg - 1) // tg
    tile_end = jnp.cumsum(tiles)
    row_off = (tile_end - tiles) * tg
    rank = jnp.cumsum(onehot.astype(jnp.int32), axis=0) - 1
    pos = row_off[ids] + jnp.take_along_axis(rank, ids, axis=1)
    n_tiles = (TOP_K * st + n_exp * (tg - 1)) // tg
    tok = jnp.zeros((n_tiles * tg,), jnp.int32).at[pos.reshape(-1)].set(
        jnp.repeat(jnp.arange(st, dtype=jnp.int32), TOP_K), unique_indices=True)
    tile_expert = jnp.minimum(jnp.searchsorted(tile_end, jnp.arange(n_tiles), side="right"),
                              n_exp - 1).astype(jnp.int32)
    ys = grouped_moe(h, tok, tile_expert, tile_end[-1:].astype(jnp.int32), w_gu, b_gu, w_down, b_down)
    return moe_combine(route, ys, pos.T.reshape(-1).astype(jnp.int32))


def _rope_tables(n_ctx, n_lat, dim):
    quarter = dim // 4
    inv_freq = ROPE_THETA ** (-jnp.arange(quarter, dtype=F32) / quarter)
    t = jnp.arange(n_lat)
    row = (t // GRID_W).astype(F32)
    col = (t % GRID_W).astype(F32)
    ang = jnp.stack([row[:, None] * inv_freq, col[:, None] * inv_freq], axis=1)
    ang = jnp.broadcast_to(ang[:, :, None, :], (n_lat, 2, 2, quarter)).reshape(n_lat, dim)
    cos, sin = jnp.cos(ang), jnp.sin(ang)
    first_half = (jnp.arange(dim) % (2 * quarter)) < quarter
    sin_lo = jnp.where(first_half, -sin, 0.0)
    sin_hi = jnp.where(first_half, 0.0, sin)

    def full(tab, fill):
        tab = jnp.pad(tab, ((0, 0), (0, LANES - dim)), constant_values=fill)
        return jnp.concatenate([jnp.full((n_ctx, LANES), fill, F32), tab], axis=0)

    return full(cos, 1.0), full(sin_lo, 0.0), full(sin_hi, 0.0)


def _pad_cols(w, n):
    return jnp.pad(w, ((0, 0), (0, n - w.shape[1])))


def _streams(v_ctx, v_lat):
    return jnp.stack([v_ctx, v_lat], axis=0)[:, None, :]


def _even_mixer(h, rope128, w_in, w_out, a_qn, a_kn, a_sink, b_qn, b_kn, b_lam, b_subln, lam_init, d):
    a_heads = d // (2 * HEAD_DIM)
    a_kv = a_heads // 4
    b_heads = d // (4 * HEAD_DIM)
    b_kv = b_heads // 2
    ga, gb = a_heads // a_kv, b_heads // b_kv
    scale = HEAD_DIM ** -0.5 * LOG2E
    n_qa, n_qb, n_ka, n_va, n_kb, n_vb = a_heads, 2 * b_heads, a_kv, a_kv, 2 * b_kv, 2 * b_kv
    c_qb = n_qa
    c_ka = c_qb + n_qb
    c_va = c_ka + n_ka
    c_kb = c_va + n_va
    c_vb = c_kb + n_kb
    p = matmul([h], [w_in.astype(BF16)], tn=512)
    gains = jnp.stack([a_qn, b_qn, a_kn, b_kn], axis=0).astype(F32)
    pn = prep_heads(p, rope128, gains,
                    segs=((0, c_qb, 0, scale), (c_qb, c_ka, 1, scale), (c_ka, c_va, 2, 1.0), (c_kb, c_vb, 3, 1.0)),
                    copies=((c_va, c_kb), (c_vb, c_vb + n_vb)))
    sink = jnp.broadcast_to((a_sink.astype(F32) * LOG2E).reshape(a_kv, ga, 1), (a_kv, ga, LANES))
    ya = window_attention(pn, sink, a_kv, ga, 0, c_ka, c_va)
    lf = b_lam.astype(F32)
    lam = jnp.exp(jnp.sum(lf[0] * lf[1])) - jnp.exp(jnp.sum(lf[2] * lf[3])) + lam_init
    dvb = 2 * HEAD_DIM
    aux = jnp.stack([jnp.full((dvb,), lam, F32), b_subln.astype(F32) * (1.0 - lam_init)], axis=0)
    flag = _score_bound_flag(b_qn, b_kn, HEAD_DIM, scale)
    yb = flash_attention(pn, pn, pn, aux, flag, n_kv=b_kv, g=gb, comps=2, dq=HEAD_DIM, dv=dvb,
                         c_q=c_qb * HEAD_DIM // (gb * 2 * HEAD_DIM), c_k=c_kb // 2, c_v=lambda hh: c_vb // 2 + hh,
                         tk=1024, diff=True)
    w_out = w_out.astype(BF16)
    na = a_heads * HEAD_DIM
    return matmul([ya, yb], [w_out[:na], w_out[na:]], tn=512)


def _odd_mixer(h, rope128, rope64, w_in, w_out, c_qn, c_kn, d_qa_norm, d_kva_norm, d_wq_up, d_wkv_up, d_qn, d_kn, d):
    st = h.shape[0]
    c_heads = d // (2 * HEAD_DIM)
    c_kv = c_heads // 4
    gc = c_heads // c_kv
    d_heads = d // (2 * HEAD_DIM)
    q_rank = d_wq_up.shape[0]
    kv_rank = d_wkv_up.shape[0]
    n_qc, n_qa, n_kc, n_vc, n_kva = c_heads, q_rank // LANES, c_kv, c_kv, kv_rank // LANES
    c_qa = n_qc
    c_kc = c_qa + n_qa
    c_vc = c_kc + n_kc
    c_kva = c_vc + n_vc
    c_kr = c_kva + n_kva
    n_in = (c_kr + 1) * LANES
    tn = 768
    n_pad = -(-n_in // tn) * tn
    p = matmul([h], [_pad_cols(w_in, n_pad).astype(BF16)], tn=tn)
    tm = ROW_TILE
    tab = pl.BlockSpec((tm, LANES), lambda i: (i, 0))
    gains = jnp.stack([c_qn, c_kn], axis=0).astype(F32)
    qk, qa, kva = pl.pallas_call(
        functools.partial(_prep_odd1_kernel, n_q=n_qc, n_k=n_kc, c_k=c_kc, c_qa=c_qa, n_qa=n_qa, c_kva=c_kva,
                          n_kva=n_kva, q_scale=HEAD_DIM ** -0.5 * LOG2E),
        grid=(st // tm,),
        in_specs=[pl.BlockSpec((tm, n_pad), lambda i: (i, 0)), tab, tab, tab,
                  pl.BlockSpec((2, LANES), lambda i: (0, 0)),
                  pl.BlockSpec((1, q_rank), lambda i: (0, 0)),
                  pl.BlockSpec((1, kv_rank), lambda i: (0, 0))],
        out_specs=[pl.BlockSpec((tm, (n_qc + n_kc) * LANES), lambda i: (i, 0)),
                   pl.BlockSpec((tm, q_rank), lambda i: (i, 0)),
                   pl.BlockSpec((tm, kv_rank), lambda i: (i, 0))],
        out_shape=[jax.ShapeDtypeStruct((st, (n_qc + n_kc) * LANES), BF16),
                   jax.ShapeDtypeStruct((st, q_rank), BF16),
                   jax.ShapeDtypeStruct((st, kv_rank), BF16)],
        compiler_params=_cparams("arbitrary"),
        name="prep_odd1",
    )(p, *rope128, gains, d_qa_norm.astype(F32)[None, :], d_kva_norm.astype(F32)[None, :])
    wq = jnp.pad(d_wq_up.reshape(q_rank, d_heads, D_QK), ((0, 0), (0, 0), (0, D_PAD - D_QK)))
    qd_raw = matmul([qa], [wq.reshape(q_rank, d_heads * D_PAD).astype(BF16)], tn=512)
    kv = matmul([kva], [d_wkv_up.astype(BF16)], tn=512)
    pad_gain = lambda gvec: jnp.pad(gvec.astype(F32), (0, D_PAD - D_QK))[None, :]
    qd, kd = pl.pallas_call(
        functools.partial(_prep_odd2_kernel, n_heads=d_heads, q_scale=D_QK ** -0.5 * LOG2E),
        grid=(st // tm,),
        in_specs=[pl.BlockSpec((tm, d_heads * D_PAD), lambda i: (i, 0)),
                  pl.BlockSpec((tm, d_heads * D_PAD), lambda i: (i, 0)),
                  pl.BlockSpec((tm, LANES), lambda i: (i, c_kr)),
                  tab, tab, tab,
                  pl.BlockSpec((1, D_PAD), lambda i: (0, 0)),
                  pl.BlockSpec((1, D_PAD), lambda i: (0, 0))],
        out_specs=[pl.BlockSpec((tm, d_heads * D_PAD), lambda i: (i, 0)),
                   pl.BlockSpec((tm, d_heads * D_PAD), lambda i: (i, 0))],
        out_shape=[jax.ShapeDtypeStruct((st, d_heads * D_PAD), BF16),
                   jax.ShapeDtypeStruct((st, d_heads * D_PAD), BF16)],
        compiler_params=_cparams("arbitrary"),
        name="prep_odd2",
    )(qd_raw, kv, p, *rope64, pad_gain(d_qn), pad_gain(d_kn))
    aux = jnp.zeros((8, LANES), F32)
    flag_c = _score_bound_flag(c_qn, c_kn, HEAD_DIM, HEAD_DIM ** -0.5 * LOG2E)
    flag_d = _score_bound_flag(d_qn, d_kn, D_QK, D_QK ** -0.5 * LOG2E)
    yc = flash_attention(qk, qk, p, aux, flag_c, n_kv=c_kv, g=gc, comps=1, dq=HEAD_DIM, dv=HEAD_DIM,
                         c_q=0, c_k=n_qc, c_v=lambda hh: c_vc + hh, tk=1024)
    yd = flash_attention(qd, kd, kv, aux, flag_d, n_kv=d_heads, g=1, comps=1, dq=D_PAD, dv=D_V,
                         c_q=0, c_k=0, c_v=lambda hh: 2 * hh + 1, tk=1024,
                         tq=1280 if st % 1280 == 0 else ROW_TILE)
    w_out = w_out.astype(BF16)
    nc = c_heads * HEAD_DIM
    return matmul([yc, yd], [w_out[:nc], w_out[nc:]], tn=512)


def kernel(x, c, ctx, c_ctx, adaln_down, adaln_up, adaln_b, norm_mix, norm_ffn, ev_w_in, ev_w_out, ev_a_qn, ev_a_kn,
           ev_a_sink, ev_b_qn, ev_b_kn, ev_b_lam, ev_b_subln, od_w_in, od_w_out, od_c_qn, od_c_kn, od_d_qa_norm,
           od_d_kva_norm, od_d_wq_up, od_d_wkv_up, od_d_qn, od_d_kn, router_w, router_b, moe_w_gu, moe_b_gu,
           moe_w_down, moe_b_down):
    bsz, seq, d = x.shape
    n_ctx = ctx.shape[1]
    depth = adaln_down.shape[0]
    n_exp = router_w.shape[2]
    assert bsz == 1 and n_ctx == ROW_TILE and seq % ROW_TILE == 0 and n_exp <= LANES

    rope128 = _rope_tables(n_ctx, seq, HEAD_DIM)
    rope64 = _rope_tables(n_ctx, seq, D_ROPE)
    cvecs = jnp.zeros((8, d), F32).at[0].set(c[0]).at[1].set(c_ctx)
    mods = adaln_all(cvecs, adaln_down, adaln_up, adaln_b)

    xs = jnp.concatenate([ctx[0], x[0]], axis=0)
    y_prev, gate_prev = None, None
    for l in range(depth):
        m_lat = mods[l, 0].reshape(N_MOD, d)
        m_ctx = mods[l, 1].reshape(N_MOD, d)
        mult = _streams(norm_mix[l] * (1.0 + m_ctx[1]), norm_mix[l] * (1.0 + m_lat[1]))
        shift = _streams(m_ctx[0], m_lat[0])
        if y_prev is None:
            (h,) = modulate(xs, mult, shift)
        else:
            xs, h = modulate(xs, mult, shift, res=(y_prev, gate_prev))
        j = l // 2
        if l % 2 == 0:
            lam_init = 0.8 - 0.6 * math.exp(-0.3 * l)
            y = _even_mixer(h, rope128, ev_w_in[j], ev_w_out[j], ev_a_qn[j], ev_a_kn[j], ev_a_sink[j], ev_b_qn[j],
                            ev_b_kn[j], ev_b_lam[j], ev_b_subln[j], lam_init, d)
        else:
            y = _odd_mixer(h, rope128, rope64, od_w_in[j], od_w_out[j], od_c_qn[j], od_c_kn[j], od_d_qa_norm[j],
                           od_d_kva_norm[j], od_d_wq_up[j], od_d_wkv_up[j], od_d_qn[j], od_d_kn[j], d)
        mult = _streams(norm_ffn[l] * (1.0 + m_ctx[4]), norm_ffn[l] * (1.0 + m_lat[4]))
        shift = _streams(m_ctx[3], m_lat[3])
        rw = _pad_cols(router_w[l].astype(F32), LANES)
        rb = jnp.pad(router_b[l].astype(F32), (0, LANES - n_exp))[None, :]
        xs, h2, route = modulate(xs, mult, shift, res=(y, _streams(m_ctx[2], m_lat[2])), router=(rw, rb, n_exp))
        y_prev = moe_sparse(h2, route, moe_w_gu[l].astype(BF16), moe_b_gu[l].astype(F32),
                            moe_w_down[l].astype(BF16), moe_b_down[l].astype(F32))
        gate_prev = _streams(m_ctx[5], m_lat[5])
    out = final_residual(xs, y_prev, gate_prev)
    return out[None]
```

```python
import functools
import math

import jax
import jax.numpy as jnp
from jax import lax
from jax.experimental import pallas as pl
from jax.experimental.pallas import tpu as pltpu

F32 = jnp.float32
BF16 = jnp.bfloat16

GRID_W = 64
HEAD_DIM = 128
ROPE_THETA = 10000.0
EPS = 1e-6
N_MOD = 6
D_NOPE = 128
D_ROPE = 64
D_QK = D_NOPE + D_ROPE
D_V = 128
D_PAD = 256
TOP_K = 4
SWIGLU_ALPHA = 1.702
SWIGLU_LIMIT = 7.0
LOG2E = 1.4426950408889634
NEG_BIG = -1e30
SCORE_BOUND = 60.0
BF16_NORM_SLACK = 1.02

LANES = 128
ROW_TILE = 256
VMEM_LIMIT = 56 * 1024 * 1024


def _cparams(*sem):
    return pltpu.CompilerParams(dimension_semantics=sem, vmem_limit_bytes=VMEM_LIMIT)


def _adaln_kernel(c_ref, down_ref, up_ref, b_ref, o_ref, t_ref):
    @pl.when(pl.program_id(1) == 0)
    def _():
        c = c_ref[...]
        a = c * (1.0 / (1.0 + jnp.exp(-c)))
        t_ref[...] = jnp.dot(a, down_ref[0], preferred_element_type=F32, precision=lax.Precision.HIGHEST)

    o_ref[0] = jnp.dot(t_ref[...], up_ref[0], preferred_element_type=F32,
                       precision=lax.Precision.HIGHEST) + b_ref[0]


def adaln_all(cvecs, down, up, bias):
    depth, d, rank = down.shape
    n = up.shape[2]
    tn = 2048
    assert n % tn == 0
    return pl.pallas_call(
        _adaln_kernel,
        grid=(depth, n // tn),
        in_specs=[
            pl.BlockSpec((8, d), lambda l, j: (0, 0)),
            pl.BlockSpec((1, d, rank), lambda l, j: (l, 0, 0)),
            pl.BlockSpec((1, rank, tn), lambda l, j: (l, 0, j)),
            pl.BlockSpec((1, 1, tn), lambda l, j: (l, 0, j)),
        ],
        out_specs=pl.BlockSpec((1, 8, tn), lambda l, j: (l, 0, j)),
        out_shape=jax.ShapeDtypeStruct((depth, 8, n), F32),
        scratch_shapes=[pltpu.VMEM((8, rank), F32)],
        compiler_params=_cparams("arbitrary", "arbitrary"),
        name="adaln",
    )(cvecs, down, up, bias.reshape(depth, 1, n))


def _topk_route(logits, n_exp):
    lane = lax.broadcasted_iota(jnp.int32, logits.shape, 1).astype(F32)
    work = jnp.where(lane < n_exp, logits, -jnp.inf)
    ids = jnp.zeros_like(logits)
    wts = jnp.zeros_like(logits)
    denom = None
    v0 = None
    for k in range(TOP_K):
        m = jnp.max(work, axis=-1, keepdims=True)
        idx = jnp.min(jnp.where(work == m, lane, float(LANES)), axis=-1, keepdims=True)
        if k == 0:
            v0 = m
            e = jnp.ones_like(m)
            denom = e
        else:
            e = jnp.exp(m - v0)
            denom = denom + e
        ids = jnp.where(lane == k, idx, ids)
        wts = jnp.where(lane == TOP_K + k, e, wts)
        work = jnp.where(lane == idx, -jnp.inf, work)
    return ids + wts / denom


def _modulate_kernel(*refs, has_res, n_exp):
    it = iter(refs)
    x_ref = next(it)
    y_ref = next(it) if has_res else None
    gate_ref = next(it) if has_res else None
    mult_ref = next(it)
    shift_ref = next(it)
    rw_ref = next(it) if n_exp else None
    rb_ref = next(it) if n_exp else None
    xo_ref = next(it) if has_res else None
    h_ref = next(it)
    g_ref = next(it) if n_exp else None

    x = x_ref[...]
    if has_res:
        x = x + gate_ref[0] * y_ref[...].astype(F32)
        xo_ref[...] = x
    ms = jnp.mean(x * x, axis=-1, keepdims=True)
    h = x * lax.rsqrt(ms + EPS) * mult_ref[0] + shift_ref[0]
    h_ref[...] = h.astype(h_ref.dtype)
    if n_exp:
        logits = jnp.dot(h, rw_ref[...], preferred_element_type=F32,
                         precision=lax.Precision.HIGHEST) + rb_ref[...]
        g_ref[...] = _topk_route(logits, n_exp)


def modulate(x, mult, shift, res=None, router=None):
    st, d = x.shape
    tm = ROW_TILE
    stream = lambda i: (jnp.minimum(i, 1), 0, 0)
    row = lambda i: (i, 0)
    vec_spec = pl.BlockSpec((1, 1, d), stream)
    args, in_specs = [x], [pl.BlockSpec((tm, d), row)]
    if res is not None:
        args += [res[0], res[1]]
        in_specs += [pl.BlockSpec((tm, d), row), vec_spec]
    args += [mult, shift]
    in_specs += [vec_spec, vec_spec]
    n_exp = 0
    if router is not None:
        args += [router[0], router[1]]
        n_exp = router[2]
        in_specs += [pl.BlockSpec((d, LANES), lambda i: (0, 0)), pl.BlockSpec((1, LANES), lambda i: (0, 0))]
    out_shape, out_specs = [], []
    if res is not None:
        out_shape.append(jax.ShapeDtypeStruct((st, d), F32))
        out_specs.append(pl.BlockSpec((tm, d), row))
    out_shape.append(jax.ShapeDtypeStruct((st, d), BF16 if router is None else F32))
    out_specs.append(pl.BlockSpec((tm, d), row))
    if router is not None:
        out_shape.append(jax.ShapeDtypeStruct((st, LANES), F32))
        out_specs.append(pl.BlockSpec((tm, LANES), row))
    return pl.pallas_call(
        functools.partial(_modulate_kernel, has_res=res is not None, n_exp=n_exp),
        grid=(st // tm,),
        in_specs=in_specs,
        out_specs=out_specs,
        out_shape=out_shape,
        compiler_params=_cparams("arbitrary"),
        name="modulate",
    )(*args)


def _final_residual_kernel(x_ref, y_ref, gate_ref, o_ref):
    o_ref[...] = x_ref[...] + gate_ref[0] * y_ref[...].astype(F32)


def final_residual(x, y, gate):
    st, d = x.shape
    tm = ROW_TILE
    n = st // tm - 1
    return pl.pallas_call(
        _final_residual_kernel,
        grid=(n,),
        in_specs=[
            pl.BlockSpec((tm, d), lambda i: (i + 1, 0)),
            pl.BlockSpec((tm, d), lambda i: (i + 1, 0)),
            pl.BlockSpec((1, 1, d), lambda i: (1, 0, 0)),
        ],
        out_specs=pl.BlockSpec((tm, d), lambda i: (i, 0)),
        out_shape=jax.ShapeDtypeStruct((n * tm, d), F32),
        compiler_params=_cparams("arbitrary"),
        name="final_residual",
    )(x, y, gate)


def _mm_kernel(*refs, n_pairs):
    o_ref = refs[-1]
    acc = None
    for p in range(n_pairs):
        t = jnp.dot(refs[p][...], refs[n_pairs + p][...], preferred_element_type=F32)
        acc = t if acc is None else acc + t
    o_ref[...] = acc.astype(o_ref.dtype)


def _row_block(m):
    for tm in (1280, 1024, 512, 256):
        if m % tm == 0:
            return tm
    raise ValueError(m)


def matmul(a_list, b_list, tn, out_dtype=BF16):
    m = a_list[0].shape[0]
    n = b_list[0].shape[1]
    tm = _row_block(m)
    while n % tn:
        tn -= LANES
    in_specs = [pl.BlockSpec((tm, a.shape[1]), lambda i, j: (i, 0)) for a in a_list]
    in_specs += [pl.BlockSpec((b.shape[0], tn), lambda i, j: (0, j)) for b in b_list]
    return pl.pallas_call(
        functools.partial(_mm_kernel, n_pairs=len(a_list)),
        grid=(m // tm, n // tn),
        in_specs=in_specs,
        out_specs=pl.BlockSpec((tm, tn), lambda i, j: (i, j)),
        out_shape=jax.ShapeDtypeStruct((m, n), out_dtype),
        compiler_params=_cparams("arbitrary", "arbitrary"),
        name="matmul",
    )(*a_list, *b_list)


def _rope(y, cos, sin_lo, sin_hi, quarter):
    left = pltpu.roll(y, LANES - quarter, 1)
    right = pltpu.roll(y, quarter, 1)
    return y * cos + left * sin_lo + right * sin_hi


def _head_norm(x, gain, n_valid):
    ss = jnp.sum(x * x, axis=-1, keepdims=True)
    return x * lax.rsqrt(ss * (1.0 / n_valid) + EPS) * gain


def _prep_even_kernel(p_ref, cos_ref, slo_ref, shi_ref, gains_ref, o_ref, *, segs, n_copy_from):
    cos, slo, shi = cos_ref[...], slo_ref[...], shi_ref[...]
    for (c0, c1, gi, scale) in segs:
        gain = gains_ref[gi:gi + 1, :]
        for c in range(c0, c1):
            x = p_ref[:, c * LANES:(c + 1) * LANES].astype(F32)
            y = _rope(_head_norm(x, gain, HEAD_DIM), cos, slo, shi, HEAD_DIM // 4)
            if scale != 1.0:
                y = y * scale
            o_ref[:, c * LANES:(c + 1) * LANES] = y.astype(BF16)
    for (c0, c1) in n_copy_from:
        o_ref[:, c0 * LANES:c1 * LANES] = p_ref[:, c0 * LANES:c1 * LANES]


def prep_heads(p, rope128, gains, segs, copies):
    st, n = p.shape
    tm = ROW_TILE
    tab = pl.BlockSpec((tm, LANES), lambda i: (i, 0))
    return pl.pallas_call(
        functools.partial(_prep_even_kernel, segs=segs, n_copy_from=copies),
        grid=(st // tm,),
        in_specs=[pl.BlockSpec((tm, n), lambda i: (i, 0)), tab, tab, tab,
                  pl.BlockSpec(gains.shape, lambda i: (0, 0))],
        out_specs=pl.BlockSpec((tm, n), lambda i: (i, 0)),
        out_shape=jax.ShapeDtypeStruct((st, n), BF16),
        compiler_params=_cparams("arbitrary"),
        name="prep_heads",
    )(p, *rope128, gains)


def _prep_odd1_kernel(p_ref, cos_ref, slo_ref, shi_ref, gains_ref, gqa_ref, gkva_ref,
                      qk_ref, qa_ref, kva_ref, *, n_q, n_k, c_k, c_qa, n_qa, c_kva, n_kva, q_scale):
    cos, slo, shi = cos_ref[...], slo_ref[...], shi_ref[...]
    for c in range(n_q):
        x = p_ref[:, c * LANES:(c + 1) * LANES].astype(F32)
        y = _rope(_head_norm(x, gains_ref[0:1, :], HEAD_DIM), cos, slo, shi, HEAD_DIM // 4) * q_scale
        qk_ref[:, c * LANES:(c + 1) * LANES] = y.astype(BF16)
    for c in range(n_k):
        x = p_ref[:, (c_k + c) * LANES:(c_k + c + 1) * LANES].astype(F32)
        y = _rope(_head_norm(x, gains_ref[1:2, :], HEAD_DIM), cos, slo, shi, HEAD_DIM // 4)
        qk_ref[:, (n_q + c) * LANES:(n_q + c + 1) * LANES] = y.astype(BF16)
    xa = p_ref[:, c_qa * LANES:(c_qa + n_qa) * LANES].astype(F32)
    qa_ref[...] = _head_norm(xa, gqa_ref[...], n_qa * LANES).astype(BF16)
    xk = p_ref[:, c_kva * LANES:(c_kva + n_kva) * LANES].astype(F32)
    kva_ref[...] = _head_norm(xk, gkva_ref[...], n_kva * LANES).astype(BF16)


def _prep_odd2_kernel(qd_ref, kv_ref, kr_ref, cos_ref, slo_ref, shi_ref, gq_ref, gk_ref,
                      qo_ref, ko_ref, *, n_heads, q_scale):
    cos, slo, shi = cos_ref[...], slo_ref[...], shi_ref[...]
    gq_n, gq_r = gq_ref[:, :LANES], gq_ref[:, LANES:]
    gk_n, gk_r = gk_ref[:, :LANES], gk_ref[:, LANES:]
    kr = kr_ref[...].astype(F32)
    kr_ss = jnp.sum(kr * kr, axis=-1, keepdims=True)
    for h in range(n_heads):
        b = h * D_PAD
        qn = qd_ref[:, b:b + LANES].astype(F32)
        qr = qd_ref[:, b + LANES:b + D_PAD].astype(F32)
        ss = jnp.sum(qn * qn, axis=-1, keepdims=True) + jnp.sum(qr * qr, axis=-1, keepdims=True)
        r = lax.rsqrt(ss * (1.0 / D_QK) + EPS) * q_scale
        qo_ref[:, b:b + LANES] = (qn * r * gq_n).astype(BF16)
        qo_ref[:, b + LANES:b + D_PAD] = _rope(qr * r * gq_r, cos, slo, shi, D_ROPE // 4).astype(BF16)
        kn = kv_ref[:, b:b + LANES].astype(F32)
        ss = jnp.sum(kn * kn, axis=-1, keepdims=True) + kr_ss
        r = lax.rsqrt(ss * (1.0 / D_QK) + EPS)
        ko_ref[:, b:b + LANES] = (kn * r * gk_n).astype(BF16)
        ko_ref[:, b + LANES:b + D_PAD] = _rope(kr * r * gk_r, cos, slo, shi, D_ROPE // 4).astype(BF16)


def _stack_heads(q_ref, g, d):
    return jnp.concatenate([q_ref[:, i * d:(i + 1) * d] for i in range(g)], axis=0)


def _qk(q, k):
    return lax.dot_general(q, k, (((1,), (1,)), ((), ())), preferred_element_type=F32)


def _window_attn_kernel(q_ref, kp_ref, km_ref, kn_ref, kc_ref, vp_ref, vm_ref, vn_ref, vc_ref, sink_ref,
                        o_ref, *, g, st):
    i = pl.program_id(1)
    tq = q_ref.shape[0]
    half = tq // 2
    qs = _stack_heads(q_ref, g, HEAD_DIM)
    k_loc = jnp.concatenate([kp_ref[...], km_ref[...], kn_ref[...]], axis=0)
    v_loc = jnp.concatenate([vp_ref[...], vm_ref[...], vn_ref[...]], axis=0)
    s_loc = _qk(qs, k_loc)
    s_ctx = _qk(qs, kc_ref[...])
    a = lax.broadcasted_iota(jnp.int32, s_loc.shape, 0) & (tq - 1)
    j = lax.broadcasted_iota(jnp.int32, s_loc.shape, 1)
    diff = j - half - a
    krow = i * tq - half + j
    valid = (jnp.abs(diff) <= half) & (krow >= tq) & (krow < st) & (i >= 1)
    s_loc = jnp.where(valid, s_loc, NEG_BIG)
    sink = jnp.concatenate([jnp.broadcast_to(sink_ref[0, h:h + 1, 0:1], (tq, 1)) for h in range(g)], axis=0)
    m = jnp.maximum(jnp.maximum(jnp.max(s_loc, axis=-1, keepdims=True),
                                jnp.max(s_ctx, axis=-1, keepdims=True)), sink)
    p_loc = jnp.exp2(s_loc - m)
    p_ctx = jnp.exp2(s_ctx - m)
    l = jnp.sum(p_loc, axis=-1, keepdims=True) + jnp.sum(p_ctx, axis=-1, keepdims=True) + jnp.exp2(sink - m)
    o = (jnp.dot(p_loc.astype(BF16), v_loc, preferred_element_type=F32)
         + jnp.dot(p_ctx.astype(BF16), vc_ref[...], preferred_element_type=F32)) / l
    for h in range(g):
        o_ref[:, h * HEAD_DIM:(h + 1) * HEAD_DIM] = o[h * tq:(h + 1) * tq].astype(BF16)


def window_attention(pn, sink, n_kv, g, c_q, c_k, c_v):
    st = pn.shape[0]
    tq = ROW_TILE
    half = tq // 2
    nh = st // half
    q_spec = pl.BlockSpec((tq, g * HEAD_DIM), lambda h, i: (i, c_q // g + h))

    def kv_specs(c0):
        return [
            pl.BlockSpec((half, HEAD_DIM), lambda h, i: (jnp.maximum(2 * i - 1, 0), c0 + h)),
            pl.BlockSpec((tq, HEAD_DIM), lambda h, i: (i, c0 + h)),
            pl.BlockSpec((half, HEAD_DIM), lambda h, i: (jnp.minimum(2 * i + 2, nh - 1), c0 + h)),
            pl.BlockSpec((tq, HEAD_DIM), lambda h, i: (0, c0 + h)),
        ]

    return pl.pallas_call(
        functools.partial(_window_attn_kernel, g=g, st=st),
        grid=(n_kv, st // tq),
        in_specs=[q_spec] + kv_specs(c_k) + kv_specs(c_v)
        + [pl.BlockSpec((1, g, LANES), lambda h, i: (h, 0, 0))],
        out_specs=pl.BlockSpec((tq, g * HEAD_DIM), lambda h, i: (i, h)),
        out_shape=jax.ShapeDtypeStruct((st, n_kv * g * HEAD_DIM), BF16),
        compiler_params=_cparams("arbitrary", "arbitrary"),
        name="window_attention",
    )(pn, pn, pn, pn, pn, pn, pn, pn, pn, sink)


def _flash_kernel(flag_ref, q_ref, k_ref, v_ref, aux_ref, o_ref, acc_ref, l_ref, accc_ref, lc_ref,
                  vt_ref, acct_ref, lt_ref, *, g, comps, dq, dv, tk, n_ctx, diff, key_major):
    qi = pl.program_id(1)
    tq = q_ref.shape[0]
    n_keys = k_ref.shape[0]
    n_lat = (n_keys - n_ctx) // tk
    ctx_only_tile = tq == n_ctx
    n_iter = jnp.where(qi == 0, 0, n_lat) if ctx_only_tile else n_lat
    n_stacked = g * comps
    qs = [jnp.concatenate([q_ref[:, (gi * comps + c) * dq:(gi * comps + c + 1) * dq] for gi in range(g)], axis=0)
          for c in range(comps)]

    def scores(start, size):
        return jnp.concatenate(
            [_qk(qs[c], k_ref[pl.ds(start, size), c * dq:(c + 1) * dq]) for c in range(comps)], axis=0)

    def chunk_start(j):
        return pl.multiple_of(n_ctx + j * tk, math.gcd(n_ctx, tk))

    def keep_context_rows(l_is_partial):
        if ctx_only_tile:
            return

        @pl.when(qi == 0)
        def _():
            for h in range(n_stacked):
                accc_ref[h * n_ctx:(h + 1) * n_ctx] = acc_ref[h * tq:h * tq + n_ctx]
                lrows = l_ref[h * tq:h * tq + n_ctx]
                if l_is_partial:
                    lrows = jnp.broadcast_to(jnp.sum(lrows, axis=-1, keepdims=True), lrows.shape)
                lc_ref[h * n_ctx:(h + 1) * n_ctx] = lrows

    def bounded_key_major():
        rows = g * tq
        key_tile = 256

        @pl.when(qi == 0)
        def _():
            def put(i, carry):
                s0 = pl.multiple_of(i * key_tile, key_tile)
                vt_ref[:, pl.ds(s0, key_tile)] = v_ref[pl.ds(s0, key_tile), :].astype(F32).T.astype(BF16)
                return carry

            lax.fori_loop(0, n_keys // key_tile, put, 0)

        def step(start, size, first):
            pt = jnp.exp2(_qk(k_ref[pl.ds(start, size), :], qs[0]))
            psum = jnp.sum(pt.reshape(size // 8, 8, rows), axis=0)
            pvt = jnp.dot(vt_ref[:, pl.ds(start, size)], pt.astype(BF16), preferred_element_type=F32)
            if first:
                acct_ref[...] = pvt
                lt_ref[...] = psum
            else:
                acct_ref[...] += pvt
                lt_ref[...] += psum

        def normalised_rows():
            return (acct_ref[...] / jnp.sum(lt_ref[...], axis=0, keepdims=True)).T

        step(0, n_ctx, True)
        if not ctx_only_tile:
            @pl.when(qi == 0)
            def _():
                o = normalised_rows()
                for h in range(g):
                    accc_ref[h * n_ctx:(h + 1) * n_ctx] = o[h * tq:h * tq + n_ctx]
                lc_ref[...] = jnp.ones_like(lc_ref)

        def body(j, carry):
            step(chunk_start(j), tk, False)
            return carry

        lax.fori_loop(0, n_iter, body, 0)
        acc_ref[...] = normalised_rows()
        l_ref[...] = jnp.ones_like(l_ref)

    @pl.when(flag_ref[0] == 1)
    def _bounded():
        if key_major:
            bounded_key_major()
            return

        def step(start, size, first):
            p = jnp.exp2(scores(start, size))
            psum = p[:, :LANES]
            for b in range(1, size // LANES):
                psum = psum + p[:, b * LANES:(b + 1) * LANES]
            pv = jnp.dot(p.astype(BF16), v_ref[pl.ds(start, size), :], preferred_element_type=F32)
            if first:
                acc_ref[...] = pv
                l_ref[...] = psum
            else:
                acc_ref[...] += pv
                l_ref[...] += psum

        step(0, n_ctx, True)
        keep_context_rows(True)

        def body(j, carry):
            step(chunk_start(j), tk, False)
            return carry

        lax.fori_loop(0, n_iter, body, 0)
        l_ref[...] = jnp.broadcast_to(jnp.sum(l_ref[...], axis=-1, keepdims=True), l_ref.shape)

    @pl.when(flag_ref[0] != 1)
    def _online():
        def step(start, size, m, l, first):
            s = scores(start, size)
            m_cur = jnp.max(s, axis=-1, keepdims=True)
            m_new = m_cur if first else jnp.maximum(m, m_cur)
            p = jnp.exp2(s - m_new)
            pv = jnp.dot(p.astype(BF16), v_ref[pl.ds(start, size), :], preferred_element_type=F32)
            if first:
                l_new = jnp.sum(p, axis=-1, keepdims=True)
                acc_ref[...] = pv
            else:
                alpha = jnp.exp2(m - m_new)
                l_new = alpha * l + jnp.sum(p, axis=-1, keepdims=True)
                acc_ref[...] = alpha * acc_ref[...] + pv
            return m_new, l_new

        m, l = step(0, n_ctx, None, None, True)
        l_ref[...] = jnp.broadcast_to(l, l_ref.shape)
        keep_context_rows(False)
        m, l = lax.fori_loop(0, n_iter, lambda j, c: step(chunk_start(j), tk, c[0], c[1], False), (m, l))
        l_ref[...] = jnp.broadcast_to(l, l_ref.shape)

    def finalize(acc, l, rows):
        o = acc / l
        if diff:
            lam = aux_ref[0:1, 0:1]
            for gi in range(g):
                y = o[gi * rows:(gi + 1) * rows] - lam * o[(g + gi) * rows:(g + gi + 1) * rows]
                ss = jnp.mean(y * y, axis=-1, keepdims=True)
                y = y * lax.rsqrt(ss + EPS) * aux_ref[1:2, :]
                o_ref[0:rows, gi * dv:(gi + 1) * dv] = y.astype(BF16)
        else:
            for gi in range(g):
                o_ref[0:rows, gi * dv:(gi + 1) * dv] = o[gi * rows:(gi + 1) * rows].astype(BF16)

    finalize(acc_ref[...], l_ref[:, 0:1], tq)
    if not ctx_only_tile:
        @pl.when(qi == 0)
        def _():
            finalize(accc_ref[...], lc_ref[:, 0:1], n_ctx)


def _score_bound_flag(q_gain, k_gain, n_norm, scale):
    bound = n_norm * jnp.max(jnp.abs(q_gain)) * jnp.max(jnp.abs(k_gain)) * scale * BF16_NORM_SLACK
    return (bound <= SCORE_BOUND).astype(jnp.int32).reshape(1)


def flash_attention(q_arr, k_arr, v_arr, aux, flag, *, n_kv, g, comps, dq, dv, c_q, c_k, c_v, tk, tq=ROW_TILE,
                    diff=False, key_major=False):
    st = q_arr.shape[0]
    n_ctx = ROW_TILE
    assert st % tq == 0 and tq % n_ctx == 0
    assert not key_major or (comps == 1 and not diff)
    while (st - n_ctx) % tk:
        tk //= 2
    rows = g * comps * tq
    ctx_rows = g * comps * n_ctx if tq != n_ctx else 8
    km_shapes = [(dv, st), (dv, rows), (8, rows)] if key_major else [(16, LANES), (8, LANES), (8, LANES)]
    return pl.pallas_call(
        functools.partial(_flash_kernel, g=g, comps=comps, dq=dq, dv=dv, tk=tk, n_ctx=n_ctx, diff=diff,
                          key_major=key_major),
        grid=(n_kv, st // tq),
        in_specs=[
            pl.BlockSpec(memory_space=pltpu.SMEM),
            pl.BlockSpec((tq, g * comps * dq), lambda h, i: (i, c_q + h)),
            pl.BlockSpec((st, comps * dq), lambda h, i: (0, c_k + h)),
            pl.BlockSpec((st, dv), lambda h, i: (0, c_v(h))),
            pl.BlockSpec(aux.shape, lambda h, i: (0, 0)),
        ],
        out_specs=pl.BlockSpec((tq, g * dv), lambda h, i: (i, h)),
        out_shape=jax.ShapeDtypeStruct((st, n_kv * g * dv), BF16),
        scratch_shapes=[pltpu.VMEM((rows, dv), F32), pltpu.VMEM((rows, LANES), F32),
                        pltpu.VMEM((ctx_rows, dv), F32), pltpu.VMEM((ctx_rows, LANES), F32),
                        pltpu.VMEM(km_shapes[0], BF16), pltpu.VMEM(km_shapes[1], F32),
                        pltpu.VMEM(km_shapes[2], F32)],
        compiler_params=_cparams("arbitrary", "arbitrary"),
        name="flash_attention",
    )(flag, q_arr, k_arr, v_arr, aux)


MOE_TILE = 256


GATHER_UNROLL = 8


def _start_row_gather(idx_ref, base, n, src_ref, dst_ref, sem):
    def issue(r, carry):
        pltpu.make_async_copy(src_ref.at[pl.ds(idx_ref[base + r], 1)], dst_ref.at[pl.ds(r, 1)], sem).start()
        return carry

    lax.fori_loop(0, n, issue, 0, unroll=GATHER_UNROLL)


def _wait_row_gather(n, src_ref, dst_ref, sem):
    def drain(r, carry):
        pltpu.make_async_copy(src_ref.at[pl.ds(0, 1)], dst_ref.at[pl.ds(0, 1)], sem).wait()
        return carry

    lax.fori_loop(0, n, drain, 0, unroll=GATHER_UNROLL)


def _grouped_moe_kernel(te_ref, nt_ref, tok_ref, h_ref, wgu_ref, bgu_ref, wd_ref, bd_ref, o_ref, x_ref, sem, *, ff):
    t = pl.program_id(0)
    tg = o_ref.shape[0]
    slot = lax.rem(t, 2)

    @pl.when(t < nt_ref[0])
    def _():
        @pl.when(t == 0)
        def _():
            _start_row_gather(tok_ref, 0, tg, h_ref, x_ref.at[0], sem.at[0])

        @pl.when(t + 1 < nt_ref[0])
        def _():
            _start_row_gather(tok_ref, (t + 1) * tg, tg, h_ref, x_ref.at[1 - slot], sem.at[1 - slot])

        _wait_row_gather(tg, h_ref, x_ref.at[slot], sem.at[slot])
        gu = jnp.dot(x_ref[slot].astype(BF16), wgu_ref[0], preferred_element_type=F32) + bgu_ref[0]
        glu = jnp.minimum(gu[:, :ff], SWIGLU_LIMIT)
        lin = jnp.clip(gu[:, ff:], -SWIGLU_LIMIT, SWIGLU_LIMIT)
        act = glu * (1.0 / (1.0 + jnp.exp(-SWIGLU_ALPHA * glu))) * (lin + 1.0)
        o_ref[...] = jnp.dot(act.astype(BF16), wd_ref[0], preferred_element_type=F32) + bd_ref[0]

    @pl.when(t >= nt_ref[0])
    def _():
        o_ref[...] = jnp.zeros_like(o_ref)


def grouped_moe(h, tok, tile_expert, n_tiles_used, w_gu, b_gu, w_down, b_down):
    p = tok.shape[0]
    d = h.shape[1]
    n_exp, _, ff2 = w_gu.shape
    ff = ff2 // 2
    tg = MOE_TILE
    return pl.pallas_call(
        functools.partial(_grouped_moe_kernel, ff=ff),
        grid_spec=pltpu.PrefetchScalarGridSpec(
            num_scalar_prefetch=3,
            grid=(p // tg,),
            in_specs=[
                pl.BlockSpec(memory_space=pl.ANY),
                pl.BlockSpec((1, d, ff2), lambda t, te, nt, tok: (te[t], 0, 0)),
                pl.BlockSpec((1, 1, ff2), lambda t, te, nt, tok: (te[t], 0, 0)),
                pl.BlockSpec((1, ff, d), lambda t, te, nt, tok: (te[t], 0, 0)),
                pl.BlockSpec((1, 1, d), lambda t, te, nt, tok: (te[t], 0, 0)),
            ],
            out_specs=pl.BlockSpec((tg, d), lambda t, te, nt, tok: (t, 0)),
            scratch_shapes=[pltpu.VMEM((2, tg, d), F32), pltpu.SemaphoreType.DMA((2,))],
        ),
        out_shape=jax.ShapeDtypeStruct((p, d), F32),
        compiler_params=_cparams("arbitrary"),
        name="grouped_moe",
    )(tile_expert, n_tiles_used, tok, h, w_gu, b_gu.reshape(n_exp, 1, ff2), w_down, b_down.reshape(n_exp, 1, d))


COMBINE_TILE = 128


def _combine_kernel(pos_ref, route_ref, ys_ref, o_ref, buf_ref, sem, *, st):
    tm = o_ref.shape[0]
    i = pl.program_id(0)
    slot = lax.rem(i, 2)

    def start(step, s):
        for k in range(TOP_K):
            _start_row_gather(pos_ref, k * st + step * tm, tm, ys_ref, buf_ref.at[s, k], sem.at[s])

    @pl.when(i == 0)
    def _():
        start(0, 0)

    @pl.when(i + 1 < pl.num_programs(0))
    def _():
        start(i + 1, 1 - slot)

    _wait_row_gather(TOP_K * tm, ys_ref, buf_ref.at[slot, 0], sem.at[slot])
    route = route_ref[...]
    acc = None
    for k in range(TOP_K):
        term = route[:, TOP_K + k:TOP_K + k + 1] * buf_ref[slot, k]
        acc = term if acc is None else acc + term
    o_ref[...] = acc.astype(o_ref.dtype)


def moe_combine(route, ys, pos):
    st = route.shape[0]
    d = ys.shape[1]
    tm = COMBINE_TILE
    return pl.pallas_call(
        functools.partial(_combine_kernel, st=st),
        grid_spec=pltpu.PrefetchScalarGridSpec(
            num_scalar_prefetch=1,
            grid=(st // tm,),
            in_specs=[pl.BlockSpec((tm, LANES), lambda i, pos: (i, 0)), pl.BlockSpec(memory_space=pl.ANY)],
            out_specs=pl.BlockSpec((tm, d), lambda i, pos: (i, 0)),
            scratch_shapes=[pltpu.VMEM((2, TOP_K, tm, d), F32), pltpu.SemaphoreType.DMA((2,))],
        ),
        out_shape=jax.ShapeDtypeStruct((st, d), BF16),
        compiler_params=_cparams("arbitrary"),
        name="moe_combine",
    )(pos, route, ys)


def moe_sparse(h, route, w_gu, b_gu, w_down, b_down):
    st = h.shape[0]
    n_exp = w_gu.shape[0]
    tg = MOE_TILE
    ids = route[:, :TOP_K].astype(jnp.int32)
    onehot = (ids[:, :, None] == jnp.arange(n_exp)[None, None, :]).any(axis=1)
    counts = jnp.sum(onehot, axis=0, dtype=jnp.int32)
    tiles = (counts + tg - 1) // tg
    tile_end = jnp.cumsum(tiles)
    row_off = (tile_end - tiles) * tg
    rank = jnp.cumsum(onehot.astype(jnp.int32), axis=0) - 1
    pos = row_off[ids] + jnp.take_along_axis(rank, ids, axis=1)
    n_tiles = (TOP_K * st + n_exp * (tg - 1)) // tg
    tok = jnp.zeros((n_tiles * tg,), jnp.int32).at[pos.reshape(-1)].set(
        jnp.repeat(jnp.arange(st, dtype=jnp.int32), TOP_K), unique_indices=True)
    tile_expert = jnp.minimum(jnp.searchsorted(tile_end, jnp.arange(n_tiles), side="right"),
                              n_exp - 1).astype(jnp.int32)
    ys = grouped_moe(h, tok, tile_expert, tile_end[-1:].astype(jnp.int32), w_gu, b_gu, w_down, b_down)
    return moe_combine(route, ys, pos.T.reshape(-1).astype(jnp.int32))


def _rope_tables(n_ctx, n_lat, dim):
    quarter = dim // 4
    inv_freq = ROPE_THETA ** (-jnp.arange(quarter, dtype=F32) / quarter)
    t = jnp.arange(n_lat)
    row = (t // GRID_W).astype(F32)
    col = (t % GRID_W).astype(F32)
    ang = jnp.stack([row[:, None] * inv_freq, col[:, None] * inv_freq], axis=1)
    ang = jnp.broadcast_to(ang[:, :, None, :], (n_lat, 2, 2, quarter)).reshape(n_lat, dim)
    cos, sin = jnp.cos(ang), jnp.sin(ang)
    first_half = (jnp.arange(dim) % (2 * quarter)) < quarter
    sin_lo = jnp.where(first_half, -sin, 0.0)
    sin_hi = jnp.where(first_half, 0.0, sin)

    def full(tab, fill):
        tab = jnp.pad(tab, ((0, 0), (0, LANES - dim)), constant_values=fill)
        return jnp.concatenate([jnp.full((n_ctx, LANES), fill, F32), tab], axis=0)

    return full(cos, 1.0), full(sin_lo, 0.0), full(sin_hi, 0.0)


def _pad_cols(w, n):
    return jnp.pad(w, ((0, 0), (0, n - w.shape[1])))


def _streams(v_ctx, v_lat):
    return jnp.stack([v_ctx, v_lat], axis=0)[:, None, :]


def _even_mixer(h, rope128, w_in, w_out, a_qn, a_kn, a_sink, b_qn, b_kn, b_lam, b_subln, lam_init, d):
    a_heads = d // (2 * HEAD_DIM)
    a_kv = a_heads // 4
    b_heads = d // (4 * HEAD_DIM)
    b_kv = b_heads // 2
    ga, gb = a_heads // a_kv, b_heads // b_kv
    scale = HEAD_DIM ** -0.5 * LOG2E
    n_qa, n_qb, n_ka, n_va, n_kb, n_vb = a_heads, 2 * b_heads, a_kv, a_kv, 2 * b_kv, 2 * b_kv
    c_qb = n_qa
    c_ka = c_qb + n_qb
    c_va = c_ka + n_ka
    c_kb = c_va + n_va
    c_vb = c_kb + n_kb
    p = matmul([h], [w_in.astype(BF16)], tn=512)
    gains = jnp.stack([a_qn, b_qn, a_kn, b_kn], axis=0).astype(F32)
    pn = prep_heads(p, rope128, gains,
                    segs=((0, c_qb, 0, scale), (c_qb, c_ka, 1, scale), (c_ka, c_va, 2, 1.0), (c_kb, c_vb, 3, 1.0)),
                    copies=((c_va, c_kb), (c_vb, c_vb + n_vb)))
    sink = jnp.broadcast_to((a_sink.astype(F32) * LOG2E).reshape(a_kv, ga, 1), (a_kv, ga, LANES))
    ya = window_attention(pn, sink, a_kv, ga, 0, c_ka, c_va)
    lf = b_lam.astype(F32)
    lam = jnp.exp(jnp.sum(lf[0] * lf[1])) - jnp.exp(jnp.sum(lf[2] * lf[3])) + lam_init
    dvb = 2 * HEAD_DIM
    aux = jnp.stack([jnp.full((dvb,), lam, F32), b_subln.astype(F32) * (1.0 - lam_init)], axis=0)
    flag = _score_bound_flag(b_qn, b_kn, HEAD_DIM, scale)
    yb = flash_attention(pn, pn, pn, aux, flag, n_kv=b_kv, g=gb, comps=2, dq=HEAD_DIM, dv=dvb,
                         c_q=c_qb * HEAD_DIM // (gb * 2 * HEAD_DIM), c_k=c_kb // 2, c_v=lambda hh: c_vb // 2 + hh,
                         tk=1024, diff=True)
    w_out = w_out.astype(BF16)
    na = a_heads * HEAD_DIM
    return matmul([ya, yb], [w_out[:na], w_out[na:]], tn=512)


def _odd_mixer(h, rope128, rope64, w_in, w_out, c_qn, c_kn, d_qa_norm, d_kva_norm, d_wq_up, d_wkv_up, d_qn, d_kn, d):
    st = h.shape[0]
    c_heads = d // (2 * HEAD_DIM)
    c_kv = c_heads // 4
    gc = c_heads // c_kv
    d_heads = d // (2 * HEAD_DIM)
    q_rank = d_wq_up.shape[0]
    kv_rank = d_wkv_up.shape[0]
    n_qc, n_qa, n_kc, n_vc, n_kva = c_heads, q_rank // LANES, c_kv, c_kv, kv_rank // LANES
    c_qa = n_qc
    c_kc = c_qa + n_qa
    c_vc = c_kc + n_kc
    c_kva = c_vc + n_vc
    c_kr = c_kva + n_kva
    n_in = (c_kr + 1) * LANES
    tn = 768
    n_pad = -(-n_in // tn) * tn
    p = matmul([h], [_pad_cols(w_in, n_pad).astype(BF16)], tn=tn)
    tm = ROW_TILE
    tab = pl.BlockSpec((tm, LANES), lambda i: (i, 0))
    gains = jnp.stack([c_qn, c_kn], axis=0).astype(F32)
    qk, qa, kva = pl.pallas_call(
        functools.partial(_prep_odd1_kernel, n_q=n_qc, n_k=n_kc, c_k=c_kc, c_qa=c_qa, n_qa=n_qa, c_kva=c_kva,
                          n_kva=n_kva, q_scale=HEAD_DIM ** -0.5 * LOG2E),
        grid=(st // tm,),
        in_specs=[pl.BlockSpec((tm, n_pad), lambda i: (i, 0)), tab, tab, tab,
                  pl.BlockSpec((2, LANES), lambda i: (0, 0)),
                  pl.BlockSpec((1, q_rank), lambda i: (0, 0)),
                  pl.BlockSpec((1, kv_rank), lambda i: (0, 0))],
        out_specs=[pl.BlockSpec((tm, (n_qc + n_kc) * LANES), lambda i: (i, 0)),
                   pl.BlockSpec((tm, q_rank), lambda i: (i, 0)),
                   pl.BlockSpec((tm, kv_rank), lambda i: (i, 0))],
        out_shape=[jax.ShapeDtypeStruct((st, (n_qc + n_kc) * LANES), BF16),
                   jax.ShapeDtypeStruct((st, q_rank), BF16),
                   jax.ShapeDtypeStruct((st, kv_rank), BF16)],
        compiler_params=_cparams("arbitrary"),
        name="prep_odd1",
    )(p, *rope128, gains, d_qa_norm.astype(F32)[None, :], d_kva_norm.astype(F32)[None, :])
    wq = jnp.pad(d_wq_up.reshape(q_rank, d_heads, D_QK), ((0, 0), (0, 0), (0, D_PAD - D_QK)))
    qd_raw = matmul([qa], [wq.reshape(q_rank, d_heads * D_PAD).astype(BF16)], tn=512)
    kv = matmul([kva], [d_wkv_up.astype(BF16)], tn=512)
    pad_gain = lambda gvec: jnp.pad(gvec.astype(F32), (0, D_PAD - D_QK))[None, :]
    qd, kd = pl.pallas_call(
        functools.partial(_prep_odd2_kernel, n_heads=d_heads, q_scale=D_QK ** -0.5 * LOG2E),
        grid=(st // tm,),
        in_specs=[pl.BlockSpec((tm, d_heads * D_PAD), lambda i: (i, 0)),
                  pl.BlockSpec((tm, d_heads * D_PAD), lambda i: (i, 0)),
                  pl.BlockSpec((tm, LANES), lambda i: (i, c_kr)),
                  tab, tab, tab,
                  pl.BlockSpec((1, D_PAD), lambda i: (0, 0)),
                  pl.BlockSpec((1, D_PAD), lambda i: (0, 0))],
        out_specs=[pl.BlockSpec((tm, d_heads * D_PAD), lambda i: (i, 0)),
                   pl.BlockSpec((tm, d_heads * D_PAD), lambda i: (i, 0))],
        out_shape=[jax.ShapeDtypeStruct((st, d_heads * D_PAD), BF16),
                   jax.ShapeDtypeStruct((st, d_heads * D_PAD), BF16)],
        compiler_params=_cparams("arbitrary"),
        name="prep_odd2",
    )(qd_raw, kv, p, *rope64, pad_gain(d_qn), pad_gain(d_kn))
    aux = jnp.zeros((8, LANES), F32)
    flag_c = _score_bound_flag(c_qn, c_kn, HEAD_DIM, HEAD_DIM ** -0.5 * LOG2E)
    flag_d = _score_bound_flag(d_qn, d_kn, D_QK, D_QK ** -0.5 * LOG2E)
    yc = flash_attention(qk, qk, p, aux, flag_c, n_kv=c_kv, g=gc, comps=1, dq=HEAD_DIM, dv=HEAD_DIM,
                         c_q=0, c_k=n_qc, c_v=lambda hh: c_vc + hh, tk=1024, key_major=True)
    yd = flash_attention(qd, kd, kv, aux, flag_d, n_kv=d_heads, g=1, comps=1, dq=D_PAD, dv=D_V,
                         c_q=0, c_k=0, c_v=lambda hh: 2 * hh + 1, tk=1024,
                         tq=1280 if st % 1280 == 0 else ROW_TILE, key_major=True)
    w_out = w_out.astype(BF16)
    nc = c_heads * HEAD_DIM
    return matmul([yc, yd], [w_out[:nc], w_out[nc:]], tn=512)


def kernel(x, c, ctx, c_ctx, adaln_down, adaln_up, adaln_b, norm_mix, norm_ffn, ev_w_in, ev_w_out, ev_a_qn, ev_a_kn,
           ev_a_sink, ev_b_qn, ev_b_kn, ev_b_lam, ev_b_subln, od_w_in, od_w_out, od_c_qn, od_c_kn, od_d_qa_norm,
           od_d_kva_norm, od_d_wq_up, od_d_wkv_up, od_d_qn, od_d_kn, router_w, router_b, moe_w_gu, moe_b_gu,
           moe_w_down, moe_b_down):
    bsz, seq, d = x.shape
    n_ctx = ctx.shape[1]
    depth = adaln_down.shape[0]
    n_exp = router_w.shape[2]
    assert bsz == 1 and n_ctx == ROW_TILE and seq % ROW_TILE == 0 and n_exp <= LANES

    rope128 = _rope_tables(n_ctx, seq, HEAD_DIM)
    rope64 = _rope_tables(n_ctx, seq, D_ROPE)
    cvecs = jnp.zeros((8, d), F32).at[0].set(c[0]).at[1].set(c_ctx)
    mods = adaln_all(cvecs, adaln_down, adaln_up, adaln_b)

    xs = jnp.concatenate([ctx[0], x[0]], axis=0)
    y_prev, gate_prev = None, None
    for l in range(depth):
        m_lat = mods[l, 0].reshape(N_MOD, d)
        m_ctx = mods[l, 1].reshape(N_MOD, d)
        mult = _streams(norm_mix[l] * (1.0 + m_ctx[1]), norm_mix[l] * (1.0 + m_lat[1]))
        shift = _streams(m_ctx[0], m_lat[0])
        if y_prev is None:
            (h,) = modulate(xs, mult, shift)
        else:
            xs, h = modulate(xs, mult, shift, res=(y_prev, gate_prev))
        j = l // 2
        if l % 2 == 0:
            lam_init = 0.8 - 0.6 * math.exp(-0.3 * l)
            y = _even_mixer(h, rope128, ev_w_in[j], ev_w_out[j], ev_a_qn[j], ev_a_kn[j], ev_a_sink[j], ev_b_qn[j],
                            ev_b_kn[j], ev_b_lam[j], ev_b_subln[j], lam_init, d)
        else:
            y = _odd_mixer(h, rope128, rope64, od_w_in[j], od_w_out[j], od_c_qn[j], od_c_kn[j], od_d_qa_norm[j],
                           od_d_kva_norm[j], od_d_wq_up[j], od_d_wkv_up[j], od_d_qn[j], od_d_kn[j], d)
        mult = _streams(norm_ffn[l] * (1.0 + m_ctx[4]), norm_ffn[l] * (1.0 + m_lat[4]))
        shift = _streams(m_ctx[3], m_lat[3])
        rw = _pad_cols(router_w[l].astype(F32), LANES)
        rb = jnp.pad(router_b[l].astype(F32), (0, LANES - n_exp))[None, :]
        xs, h2, route = modulate(xs, mult, shift, res=(y, _streams(m_ctx[2], m_lat[2])), router=(rw, rb, n_exp))
        y_prev = moe_sparse(h2, route, moe_w_gu[l].astype(BF16), moe_b_gu[l].astype(F32),
                            moe_w_down[l].astype(BF16), moe_b_down[l].astype(F32))
        gate_prev = _streams(m_ctx[5], m_lat[5])
    out = final_residual(xs, y_prev, gate_prev)
    return out[None]
```

```python
import functools
import math

import jax
import jax.numpy as jnp
from jax import lax
from jax.experimental import pallas as pl
from jax.experimental.pallas import tpu as pltpu

F32 = jnp.float32
BF16 = jnp.bfloat16

GRID_W = 64
HEAD_DIM = 128
ROPE_THETA = 10000.0
EPS = 1e-6
N_MOD = 6
D_NOPE = 128
D_ROPE = 64
D_QK = D_NOPE + D_ROPE
D_V = 128
D_PAD = 256
TOP_K = 4
SWIGLU_ALPHA = 1.702
SWIGLU_LIMIT = 7.0
LOG2E = 1.4426950408889634
NEG_BIG = -1e30
SCORE_BOUND = 60.0
BF16_NORM_SLACK = 1.02

LANES = 128
ROW_TILE = 256
VMEM_LIMIT = 56 * 1024 * 1024


def _cparams(*sem):
    return pltpu.CompilerParams(dimension_semantics=sem, vmem_limit_bytes=VMEM_LIMIT)


def _adaln_kernel(c_ref, down_ref, up_ref, b_ref, o_ref, t_ref):
    @pl.when(pl.program_id(1) == 0)
    def _():
        c = c_ref[...]
        a = c * (1.0 / (1.0 + jnp.exp(-c)))
        t_ref[...] = jnp.dot(a, down_ref[0], preferred_element_type=F32, precision=lax.Precision.HIGHEST)

    o_ref[0] = jnp.dot(t_ref[...], up_ref[0], preferred_element_type=F32,
                       precision=lax.Precision.HIGHEST) + b_ref[0]


def adaln_all(cvecs, down, up, bias):
    depth, d, rank = down.shape
    n = up.shape[2]
    tn = 2048
    assert n % tn == 0
    return pl.pallas_call(
        _adaln_kernel,
        grid=(depth, n // tn),
        in_specs=[
            pl.BlockSpec((8, d), lambda l, j: (0, 0)),
            pl.BlockSpec((1, d, rank), lambda l, j: (l, 0, 0)),
            pl.BlockSpec((1, rank, tn), lambda l, j: (l, 0, j)),
            pl.BlockSpec((1, 1, tn), lambda l, j: (l, 0, j)),
        ],
        out_specs=pl.BlockSpec((1, 8, tn), lambda l, j: (l, 0, j)),
        out_shape=jax.ShapeDtypeStruct((depth, 8, n), F32),
        scratch_shapes=[pltpu.VMEM((8, rank), F32)],
        compiler_params=_cparams("arbitrary", "arbitrary"),
        name="adaln",
    )(cvecs, down, up, bias.reshape(depth, 1, n))


def _topk_route(logits, n_exp):
    lane = lax.broadcasted_iota(jnp.int32, logits.shape, 1).astype(F32)
    work = jnp.where(lane < n_exp, logits, -jnp.inf)
    ids = jnp.zeros_like(logits)
    wts = jnp.zeros_like(logits)
    denom = None
    v0 = None
    for k in range(TOP_K):
        m = jnp.max(work, axis=-1, keepdims=True)
        idx = jnp.min(jnp.where(work == m, lane, float(LANES)), axis=-1, keepdims=True)
        if k == 0:
            v0 = m
            e = jnp.ones_like(m)
            denom = e
        else:
            e = jnp.exp(m - v0)
            denom = denom + e
        ids = jnp.where(lane == k, idx, ids)
        wts = jnp.where(lane == TOP_K + k, e, wts)
        work = jnp.where(lane == idx, -jnp.inf, work)
    return ids + wts / denom


def _modulate_kernel(*refs, has_res, n_exp):
    it = iter(refs)
    x_ref = next(it)
    y_ref = next(it) if has_res else None
    gate_ref = next(it) if has_res else None
    mult_ref = next(it)
    shift_ref = next(it)
    rw_ref = next(it) if n_exp else None
    rb_ref = next(it) if n_exp else None
    xo_ref = next(it) if has_res else None
    h_ref = next(it)
    g_ref = next(it) if n_exp else None

    x = x_ref[...]
    if has_res:
        x = x + gate_ref[0] * y_ref[...].astype(F32)
        xo_ref[...] = x
    ms = jnp.mean(x * x, axis=-1, keepdims=True)
    h = x * lax.rsqrt(ms + EPS) * mult_ref[0] + shift_ref[0]
    h_ref[...] = h.astype(h_ref.dtype)
    if n_exp:
        logits = jnp.dot(h, rw_ref[...], preferred_element_type=F32,
                         precision=lax.Precision.HIGHEST) + rb_ref[...]
        g_ref[...] = _topk_route(logits, n_exp)


def modulate(x, mult, shift, res=None, router=None):
    st, d = x.shape
    tm = ROW_TILE
    stream = lambda i: (jnp.minimum(i, 1), 0, 0)
    row = lambda i: (i, 0)
    vec_spec = pl.BlockSpec((1, 1, d), stream)
    args, in_specs = [x], [pl.BlockSpec((tm, d), row)]
    if res is not None:
        args += [res[0], res[1]]
        in_specs += [pl.BlockSpec((tm, d), row), vec_spec]
    args += [mult, shift]
    in_specs += [vec_spec, vec_spec]
    n_exp = 0
    if router is not None:
        args += [router[0], router[1]]
        n_exp = router[2]
        in_specs += [pl.BlockSpec((d, LANES), lambda i: (0, 0)), pl.BlockSpec((1, LANES), lambda i: (0, 0))]
    out_shape, out_specs = [], []
    if res is not None:
        out_shape.append(jax.ShapeDtypeStruct((st, d), F32))
        out_specs.append(pl.BlockSpec((tm, d), row))
    out_shape.append(jax.ShapeDtypeStruct((st, d), BF16 if router is None else F32))
    out_specs.append(pl.BlockSpec((tm, d), row))
    if router is not None:
        out_shape.append(jax.ShapeDtypeStruct((st, LANES), F32))
        out_specs.append(pl.BlockSpec((tm, LANES), row))
    return pl.pallas_call(
        functools.partial(_modulate_kernel, has_res=res is not None, n_exp=n_exp),
        grid=(st // tm,),
        in_specs=in_specs,
        out_specs=out_specs,
        out_shape=out_shape,
        compiler_params=_cparams("arbitrary"),
        name="modulate",
    )(*args)


def _final_residual_kernel(x_ref, y_ref, gate_ref, o_ref):
    o_ref[...] = x_ref[...] + gate_ref[0] * y_ref[...].astype(F32)


def final_residual(x, y, gate):
    st, d = x.shape
    tm = ROW_TILE
    n = st // tm - 1
    return pl.pallas_call(
        _final_residual_kernel,
        grid=(n,),
        in_specs=[
            pl.BlockSpec((tm, d), lambda i: (i + 1, 0)),
            pl.BlockSpec((tm, d), lambda i: (i + 1, 0)),
            pl.BlockSpec((1, 1, d), lambda i: (1, 0, 0)),
        ],
        out_specs=pl.BlockSpec((tm, d), lambda i: (i, 0)),
        out_shape=jax.ShapeDtypeStruct((n * tm, d), F32),
        compiler_params=_cparams("arbitrary"),
        name="final_residual",
    )(x, y, gate)


def _mm_kernel(*refs, n_pairs):
    o_ref = refs[-1]
    acc = None
    for p in range(n_pairs):
        t = jnp.dot(refs[p][...], refs[n_pairs + p][...], preferred_element_type=F32)
        acc = t if acc is None else acc + t
    o_ref[...] = acc.astype(o_ref.dtype)


def _row_block(m):
    for tm in (1280, 1024, 512, 256):
        if m % tm == 0:
            return tm
    raise ValueError(m)


def matmul(a_list, b_list, tn, out_dtype=BF16):
    m = a_list[0].shape[0]
    n = b_list[0].shape[1]
    tm = _row_block(m)
    while n % tn:
        tn -= LANES
    in_specs = [pl.BlockSpec((tm, a.shape[1]), lambda i, j: (i, 0)) for a in a_list]
    in_specs += [pl.BlockSpec((b.shape[0], tn), lambda i, j: (0, j)) for b in b_list]
    return pl.pallas_call(
        functools.partial(_mm_kernel, n_pairs=len(a_list)),
        grid=(m // tm, n // tn),
        in_specs=in_specs,
        out_specs=pl.BlockSpec((tm, tn), lambda i, j: (i, j)),
        out_shape=jax.ShapeDtypeStruct((m, n), out_dtype),
        compiler_params=_cparams("arbitrary", "arbitrary"),
        name="matmul",
    )(*a_list, *b_list)


def _rope(y, cos, sin_lo, sin_hi, quarter):
    left = pltpu.roll(y, LANES - quarter, 1)
    right = pltpu.roll(y, quarter, 1)
    return y * cos + left * sin_lo + right * sin_hi


def _head_norm(x, gain, n_valid):
    ss = jnp.sum(x * x, axis=-1, keepdims=True)
    return x * lax.rsqrt(ss * (1.0 / n_valid) + EPS) * gain


def _prep_even_kernel(p_ref, cos_ref, slo_ref, shi_ref, gains_ref, o_ref, *, segs, n_copy_from):
    cos, slo, shi = cos_ref[...], slo_ref[...], shi_ref[...]
    for (c0, c1, gi, scale) in segs:
        gain = gains_ref[gi:gi + 1, :]
        for c in range(c0, c1):
            x = p_ref[:, c * LANES:(c + 1) * LANES].astype(F32)
            y = _rope(_head_norm(x, gain, HEAD_DIM), cos, slo, shi, HEAD_DIM // 4)
            if scale != 1.0:
                y = y * scale
            o_ref[:, c * LANES:(c + 1) * LANES] = y.astype(BF16)
    for (c0, c1) in n_copy_from:
        o_ref[:, c0 * LANES:c1 * LANES] = p_ref[:, c0 * LANES:c1 * LANES]


def prep_heads(p, rope128, gains, segs, copies):
    st, n = p.shape
    tm = ROW_TILE
    tab = pl.BlockSpec((tm, LANES), lambda i: (i, 0))
    return pl.pallas_call(
        functools.partial(_prep_even_kernel, segs=segs, n_copy_from=copies),
        grid=(st // tm,),
        in_specs=[pl.BlockSpec((tm, n), lambda i: (i, 0)), tab, tab, tab,
                  pl.BlockSpec(gains.shape, lambda i: (0, 0))],
        out_specs=pl.BlockSpec((tm, n), lambda i: (i, 0)),
        out_shape=jax.ShapeDtypeStruct((st, n), BF16),
        compiler_params=_cparams("arbitrary"),
        name="prep_heads",
    )(p, *rope128, gains)


def _prep_odd1_kernel(p_ref, cos_ref, slo_ref, shi_ref, gains_ref, gqa_ref, gkva_ref,
                      qk_ref, qa_ref, kva_ref, *, n_q, n_k, c_k, c_qa, n_qa, c_kva, n_kva, q_scale):
    cos, slo, shi = cos_ref[...], slo_ref[...], shi_ref[...]
    for c in range(n_q):
        x = p_ref[:, c * LANES:(c + 1) * LANES].astype(F32)
        y = _rope(_head_norm(x, gains_ref[0:1, :], HEAD_DIM), cos, slo, shi, HEAD_DIM // 4) * q_scale
        qk_ref[:, c * LANES:(c + 1) * LANES] = y.astype(BF16)
    for c in range(n_k):
        x = p_ref[:, (c_k + c) * LANES:(c_k + c + 1) * LANES].astype(F32)
        y = _rope(_head_norm(x, gains_ref[1:2, :], HEAD_DIM), cos, slo, shi, HEAD_DIM // 4)
        qk_ref[:, (n_q + c) * LANES:(n_q + c + 1) * LANES] = y.astype(BF16)
    xa = p_ref[:, c_qa * LANES:(c_qa + n_qa) * LANES].astype(F32)
    qa_ref[...] = _head_norm(xa, gqa_ref[...], n_qa * LANES).astype(BF16)
    xk = p_ref[:, c_kva * LANES:(c_kva + n_kva) * LANES].astype(F32)
    kva_ref[...] = _head_norm(xk, gkva_ref[...], n_kva * LANES).astype(BF16)


def _prep_odd2_kernel(qd_ref, kv_ref, kr_ref, cos_ref, slo_ref, shi_ref, gq_ref, gk_ref,
                      qo_ref, ko_ref, *, n_heads, q_scale):
    cos, slo, shi = cos_ref[...], slo_ref[...], shi_ref[...]
    gq_n, gq_r = gq_ref[:, :LANES], gq_ref[:, LANES:]
    gk_n, gk_r = gk_ref[:, :LANES], gk_ref[:, LANES:]
    kr = kr_ref[...].astype(F32)
    kr_ss = jnp.sum(kr * kr, axis=-1, keepdims=True)
    for h in range(n_heads):
        b = h * D_PAD
        qn = qd_ref[:, b:b + LANES].astype(F32)
        qr = qd_ref[:, b + LANES:b + D_PAD].astype(F32)
        ss = jnp.sum(qn * qn, axis=-1, keepdims=True) + jnp.sum(qr * qr, axis=-1, keepdims=True)
        r = lax.rsqrt(ss * (1.0 / D_QK) + EPS) * q_scale
        qo_ref[:, b:b + LANES] = (qn * r * gq_n).astype(BF16)
        qo_ref[:, b + LANES:b + D_PAD] = _rope(qr * r * gq_r, cos, slo, shi, D_ROPE // 4).astype(BF16)
        kn = kv_ref[:, b:b + LANES].astype(F32)
        ss = jnp.sum(kn * kn, axis=-1, keepdims=True) + kr_ss
        r = lax.rsqrt(ss * (1.0 / D_QK) + EPS)
        ko_ref[:, b:b + LANES] = (kn * r * gk_n).astype(BF16)
        ko_ref[:, b + LANES:b + D_PAD] = _rope(kr * r * gk_r, cos, slo, shi, D_ROPE // 4).astype(BF16)


def _stack_heads(q_ref, g, d):
    return jnp.concatenate([q_ref[:, i * d:(i + 1) * d] for i in range(g)], axis=0)


def _qk(q, k):
    return lax.dot_general(q, k, (((1,), (1,)), ((), ())), preferred_element_type=F32)


def _window_attn_kernel(q_ref, kp_ref, km_ref, kn_ref, kc_ref, vp_ref, vm_ref, vn_ref, vc_ref, sink_ref,
                        o_ref, *, g, st):
    i = pl.program_id(1)
    tq = q_ref.shape[0]
    half = tq // 2
    qs = _stack_heads(q_ref, g, HEAD_DIM)
    k_loc = jnp.concatenate([kp_ref[...], km_ref[...], kn_ref[...]], axis=0)
    v_loc = jnp.concatenate([vp_ref[...], vm_ref[...], vn_ref[...]], axis=0)
    s_loc = _qk(qs, k_loc)
    s_ctx = _qk(qs, kc_ref[...])
    a = lax.broadcasted_iota(jnp.int32, s_loc.shape, 0) & (tq - 1)
    j = lax.broadcasted_iota(jnp.int32, s_loc.shape, 1)
    diff = j - half - a
    krow = i * tq - half + j
    valid = (jnp.abs(diff) <= half) & (krow >= tq) & (krow < st) & (i >= 1)
    s_loc = jnp.where(valid, s_loc, NEG_BIG)
    sink = jnp.concatenate([jnp.broadcast_to(sink_ref[0, h:h + 1, 0:1], (tq, 1)) for h in range(g)], axis=0)
    m = jnp.maximum(jnp.maximum(jnp.max(s_loc, axis=-1, keepdims=True),
                                jnp.max(s_ctx, axis=-1, keepdims=True)), sink)
    p_loc = jnp.exp2(s_loc - m)
    p_ctx = jnp.exp2(s_ctx - m)
    l = jnp.sum(p_loc, axis=-1, keepdims=True) + jnp.sum(p_ctx, axis=-1, keepdims=True) + jnp.exp2(sink - m)
    o = (jnp.dot(p_loc.astype(BF16), v_loc, preferred_element_type=F32)
         + jnp.dot(p_ctx.astype(BF16), vc_ref[...], preferred_element_type=F32)) / l
    for h in range(g):
        o_ref[:, h * HEAD_DIM:(h + 1) * HEAD_DIM] = o[h * tq:(h + 1) * tq].astype(BF16)


def window_attention(pn, sink, n_kv, g, c_q, c_k, c_v):
    st = pn.shape[0]
    tq = ROW_TILE
    half = tq // 2
    nh = st // half
    q_spec = pl.BlockSpec((tq, g * HEAD_DIM), lambda h, i: (i, c_q // g + h))

    def kv_specs(c0):
        return [
            pl.BlockSpec((half, HEAD_DIM), lambda h, i: (jnp.maximum(2 * i - 1, 0), c0 + h)),
            pl.BlockSpec((tq, HEAD_DIM), lambda h, i: (i, c0 + h)),
            pl.BlockSpec((half, HEAD_DIM), lambda h, i: (jnp.minimum(2 * i + 2, nh - 1), c0 + h)),
            pl.BlockSpec((tq, HEAD_DIM), lambda h, i: (0, c0 + h)),
        ]

    return pl.pallas_call(
        functools.partial(_window_attn_kernel, g=g, st=st),
        grid=(n_kv, st // tq),
        in_specs=[q_spec] + kv_specs(c_k) + kv_specs(c_v)
        + [pl.BlockSpec((1, g, LANES), lambda h, i: (h, 0, 0))],
        out_specs=pl.BlockSpec((tq, g * HEAD_DIM), lambda h, i: (i, h)),
        out_shape=jax.ShapeDtypeStruct((st, n_kv * g * HEAD_DIM), BF16),
        compiler_params=_cparams("arbitrary", "arbitrary"),
        name="window_attention",
    )(pn, pn, pn, pn, pn, pn, pn, pn, pn, sink)


def _flash_kernel(flag_ref, q_ref, k_ref, v_ref, aux_ref, o_ref, acc_ref, l_ref, accc_ref, lc_ref,
                  vt_ref, acct_ref, lt_ref, *, g, comps, dq, dv, tk, n_ctx, diff, key_major):
    qi = pl.program_id(1)
    tq = q_ref.shape[0]
    n_keys = k_ref.shape[0]
    n_lat = (n_keys - n_ctx) // tk
    ctx_only_tile = tq == n_ctx
    n_iter = jnp.where(qi == 0, 0, n_lat) if ctx_only_tile else n_lat
    n_stacked = g * comps
    qs = [jnp.concatenate([q_ref[:, (gi * comps + c) * dq:(gi * comps + c + 1) * dq] for gi in range(g)], axis=0)
          for c in range(comps)]

    def scores(start, size):
        return jnp.concatenate(
            [_qk(qs[c], k_ref[pl.ds(start, size), c * dq:(c + 1) * dq]) for c in range(comps)], axis=0)

    def chunk_start(j):
        return pl.multiple_of(n_ctx + j * tk, math.gcd(n_ctx, tk))

    def keep_context_rows(l_is_partial):
        if ctx_only_tile:
            return

        @pl.when(qi == 0)
        def _():
            for h in range(n_stacked):
                accc_ref[h * n_ctx:(h + 1) * n_ctx] = acc_ref[h * tq:h * tq + n_ctx]
                lrows = l_ref[h * tq:h * tq + n_ctx]
                if l_is_partial:
                    lrows = jnp.broadcast_to(jnp.sum(lrows, axis=-1, keepdims=True), lrows.shape)
                lc_ref[h * n_ctx:(h + 1) * n_ctx] = lrows

    def bounded_key_major():
        rows = g * tq
        key_tile = 256

        @pl.when(qi == 0)
        def _():
            def put(i, carry):
                s0 = pl.multiple_of(i * key_tile, key_tile)
                vt_ref[:, pl.ds(s0, key_tile)] = v_ref[pl.ds(s0, key_tile), :].astype(F32).T.astype(BF16)
                return carry

            lax.fori_loop(0, n_keys // key_tile, put, 0)

        def step(start, size, first):
            pt = jnp.exp2(_qk(k_ref[pl.ds(start, size), :], qs[0]))
            psum = jnp.sum(pt.reshape(size // 8, 8, rows), axis=0)
            pvt = jnp.dot(vt_ref[:, pl.ds(start, size)], pt.astype(BF16), preferred_element_type=F32)
            if first:
                acct_ref[...] = pvt
                lt_ref[...] = psum
            else:
                acct_ref[...] += pvt
                lt_ref[...] += psum

        def normalised_rows():
            return (acct_ref[...] / jnp.sum(lt_ref[...], axis=0, keepdims=True)).T

        step(0, n_ctx, True)
        if not ctx_only_tile:
            @pl.when(qi == 0)
            def _():
                o = normalised_rows()
                for h in range(g):
                    accc_ref[h * n_ctx:(h + 1) * n_ctx] = o[h * tq:h * tq + n_ctx]
                lc_ref[...] = jnp.ones_like(lc_ref)

        def body(j, carry):
            step(chunk_start(j), tk, False)
            return carry

        lax.fori_loop(0, n_iter, body, 0)
        acc_ref[...] = normalised_rows()
        l_ref[...] = jnp.ones_like(l_ref)

    @pl.when(flag_ref[0] == 1)
    def _bounded():
        if key_major:
            bounded_key_major()
            return

        def step(start, size, first):
            p = jnp.exp2(scores(start, size))
            psum = p[:, :LANES]
            for b in range(1, size // LANES):
                psum = psum + p[:, b * LANES:(b + 1) * LANES]
            pv = jnp.dot(p.astype(BF16), v_ref[pl.ds(start, size), :], preferred_element_type=F32)
            if first:
                acc_ref[...] = pv
                l_ref[...] = psum
            else:
                acc_ref[...] += pv
                l_ref[...] += psum

        step(0, n_ctx, True)
        keep_context_rows(True)

        def body(j, carry):
            step(chunk_start(j), tk, False)
            return carry

        lax.fori_loop(0, n_iter, body, 0)
        l_ref[...] = jnp.broadcast_to(jnp.sum(l_ref[...], axis=-1, keepdims=True), l_ref.shape)

    @pl.when(flag_ref[0] != 1)
    def _online():
        def step(start, size, m, l, first):
            s = scores(start, size)
            m_cur = jnp.max(s, axis=-1, keepdims=True)
            m_new = m_cur if first else jnp.maximum(m, m_cur)
            p = jnp.exp2(s - m_new)
            pv = jnp.dot(p.astype(BF16), v_ref[pl.ds(start, size), :], preferred_element_type=F32)
            if first:
                l_new = jnp.sum(p, axis=-1, keepdims=True)
                acc_ref[...] = pv
            else:
                alpha = jnp.exp2(m - m_new)
                l_new = alpha * l + jnp.sum(p, axis=-1, keepdims=True)
                acc_ref[...] = alpha * acc_ref[...] + pv
            return m_new, l_new

        m, l = step(0, n_ctx, None, None, True)
        l_ref[...] = jnp.broadcast_to(l, l_ref.shape)
        keep_context_rows(False)
        m, l = lax.fori_loop(0, n_iter, lambda j, c: step(chunk_start(j), tk, c[0], c[1], False), (m, l))
        l_ref[...] = jnp.broadcast_to(l, l_ref.shape)

    def finalize(acc, l, rows):
        o = acc / l
        if diff:
            lam = aux_ref[0:1, 0:1]
            for gi in range(g):
                y = o[gi * rows:(gi + 1) * rows] - lam * o[(g + gi) * rows:(g + gi + 1) * rows]
                ss = jnp.mean(y * y, axis=-1, keepdims=True)
                y = y * lax.rsqrt(ss + EPS) * aux_ref[1:2, :]
                o_ref[0:rows, gi * dv:(gi + 1) * dv] = y.astype(BF16)
        else:
            for gi in range(g):
                o_ref[0:rows, gi * dv:(gi + 1) * dv] = o[gi * rows:(gi + 1) * rows].astype(BF16)

    finalize(acc_ref[...], l_ref[:, 0:1], tq)
    if not ctx_only_tile:
        @pl.when(qi == 0)
        def _():
            finalize(accc_ref[...], lc_ref[:, 0:1], n_ctx)


def _score_bound_flag(q_gain, k_gain, n_norm, scale):
    bound = n_norm * jnp.max(jnp.abs(q_gain)) * jnp.max(jnp.abs(k_gain)) * scale * BF16_NORM_SLACK
    return (bound <= SCORE_BOUND).astype(jnp.int32).reshape(1)


def flash_attention(q_arr, k_arr, v_arr, aux, flag, *, n_kv, g, comps, dq, dv, c_q, c_k, c_v, tk, tq=ROW_TILE,
                    diff=False, key_major=False):
    st = q_arr.shape[0]
    n_ctx = ROW_TILE
    assert st % tq == 0 and tq % n_ctx == 0
    assert not key_major or (comps == 1 and not diff)
    while (st - n_ctx) % tk:
        tk //= 2
    rows = g * comps * tq
    ctx_rows = g * comps * n_ctx if tq != n_ctx else 8
    km_shapes = [(dv, st), (dv, rows), (8, rows)] if key_major else [(16, LANES), (8, LANES), (8, LANES)]
    return pl.pallas_call(
        functools.partial(_flash_kernel, g=g, comps=comps, dq=dq, dv=dv, tk=tk, n_ctx=n_ctx, diff=diff,
                          key_major=key_major),
        grid=(n_kv, st // tq),
        in_specs=[
            pl.BlockSpec(memory_space=pltpu.SMEM),
            pl.BlockSpec((tq, g * comps * dq), lambda h, i: (i, c_q + h)),
            pl.BlockSpec((st, comps * dq), lambda h, i: (0, c_k + h)),
            pl.BlockSpec((st, dv), lambda h, i: (0, c_v(h))),
            pl.BlockSpec(aux.shape, lambda h, i: (0, 0)),
        ],
        out_specs=pl.BlockSpec((tq, g * dv), lambda h, i: (i, h)),
        out_shape=jax.ShapeDtypeStruct((st, n_kv * g * dv), BF16),
        scratch_shapes=[pltpu.VMEM((rows, dv), F32), pltpu.VMEM((rows, LANES), F32),
                        pltpu.VMEM((ctx_rows, dv), F32), pltpu.VMEM((ctx_rows, LANES), F32),
                        pltpu.VMEM(km_shapes[0], BF16), pltpu.VMEM(km_shapes[1], F32),
                        pltpu.VMEM(km_shapes[2], F32)],
        compiler_params=_cparams("arbitrary", "arbitrary"),
        name="flash_attention",
    )(flag, q_arr, k_arr, v_arr, aux)


MOE_TILE = 256


GATHER_UNROLL = 8


def _start_row_gather(idx_ref, base, n, src_ref, dst_ref, sem):
    def issue(r, carry):
        pltpu.make_async_copy(src_ref.at[pl.ds(idx_ref[base + r], 1)], dst_ref.at[pl.ds(r, 1)], sem).start()
        return carry

    lax.fori_loop(0, n, issue, 0, unroll=GATHER_UNROLL)


def _wait_row_gather(n, src_ref, dst_ref, sem):
    def drain(r, carry):
        pltpu.make_async_copy(src_ref.at[pl.ds(0, 1)], dst_ref.at[pl.ds(0, 1)], sem).wait()
        return carry

    lax.fori_loop(0, n, drain, 0, unroll=GATHER_UNROLL)


def _grouped_moe_kernel(te_ref, nt_ref, tok_ref, h_ref, wgu_ref, bgu_ref, wd_ref, bd_ref, o_ref, x_ref, sem, *, ff):
    t = pl.program_id(0)
    tg = o_ref.shape[0]
    slot = lax.rem(t, 2)

    @pl.when(t < nt_ref[0])
    def _():
        @pl.when(t == 0)
        def _():
            _start_row_gather(tok_ref, 0, tg, h_ref, x_ref.at[0], sem.at[0])

        @pl.when(t + 1 < nt_ref[0])
        def _():
            _start_row_gather(tok_ref, (t + 1) * tg, tg, h_ref, x_ref.at[1 - slot], sem.at[1 - slot])

        _wait_row_gather(tg, h_ref, x_ref.at[slot], sem.at[slot])
        gu = jnp.dot(x_ref[slot].astype(BF16), wgu_ref[0], preferred_element_type=F32) + bgu_ref[0]
        glu = jnp.minimum(gu[:, :ff], SWIGLU_LIMIT)
        lin = jnp.clip(gu[:, ff:], -SWIGLU_LIMIT, SWIGLU_LIMIT)
        act = glu * (1.0 / (1.0 + jnp.exp(-SWIGLU_ALPHA * glu))) * (lin + 1.0)
        o_ref[...] = jnp.dot(act.astype(BF16), wd_ref[0], preferred_element_type=F32) + bd_ref[0]

    @pl.when(t >= nt_ref[0])
    def _():
        o_ref[...] = jnp.zeros_like(o_ref)


def grouped_moe(h, tok, tile_expert, n_tiles_used, w_gu, b_gu, w_down, b_down):
    p = tok.shape[0]
    d = h.shape[1]
    n_exp, _, ff2 = w_gu.shape
    ff = ff2 // 2
    tg = MOE_TILE
    return pl.pallas_call(
        functools.partial(_grouped_moe_kernel, ff=ff),
        grid_spec=pltpu.PrefetchScalarGridSpec(
            num_scalar_prefetch=3,
            grid=(p // tg,),
            in_specs=[
                pl.BlockSpec(memory_space=pl.ANY),
                pl.BlockSpec((1, d, ff2), lambda t, te, nt, tok: (te[t], 0, 0)),
                pl.BlockSpec((1, 1, ff2), lambda t, te, nt, tok: (te[t], 0, 0)),
                pl.BlockSpec((1, ff, d), lambda t, te, nt, tok: (te[t], 0, 0)),
                pl.BlockSpec((1, 1, d), lambda t, te, nt, tok: (te[t], 0, 0)),
            ],
            out_specs=pl.BlockSpec((tg, d), lambda t, te, nt, tok: (t, 0)),
            scratch_shapes=[pltpu.VMEM((2, tg, d), F32), pltpu.SemaphoreType.DMA((2,))],
        ),
        out_shape=jax.ShapeDtypeStruct((p, d), F32),
        compiler_params=_cparams("arbitrary"),
        name="grouped_moe",
    )(tile_expert, n_tiles_used, tok, h, w_gu, b_gu.reshape(n_exp, 1, ff2), w_down, b_down.reshape(n_exp, 1, d))


COMBINE_TILE = 128


def _combine_kernel(pos_ref, route_ref, ys_ref, o_ref, buf_ref, sem, *, st):
    tm = o_ref.shape[0]
    i = pl.program_id(0)
    slot = lax.rem(i, 2)

    def start(step, s):
        for k in range(TOP_K):
            _start_row_gather(pos_ref, k * st + step * tm, tm, ys_ref, buf_ref.at[s, k], sem.at[s])

    @pl.when(i == 0)
    def _():
        start(0, 0)

    @pl.when(i + 1 < pl.num_programs(0))
    def _():
        start(i + 1, 1 - slot)

    _wait_row_gather(TOP_K * tm, ys_ref, buf_ref.at[slot, 0], sem.at[slot])
    route = route_ref[...]
    acc = None
    for k in range(TOP_K):
        term = route[:, TOP_K + k:TOP_K + k + 1] * buf_ref[slot, k]
        acc = term if acc is None else acc + term
    o_ref[...] = acc.astype(o_ref.dtype)


def moe_combine(route, ys, pos):
    st = route.shape[0]
    d = ys.shape[1]
    tm = COMBINE_TILE
    return pl.pallas_call(
        functools.partial(_combine_kernel, st=st),
        grid_spec=pltpu.PrefetchScalarGridSpec(
            num_scalar_prefetch=1,
            grid=(st // tm,),
            in_specs=[pl.BlockSpec((tm, LANES), lambda i, pos: (i, 0)), pl.BlockSpec(memory_space=pl.ANY)],
            out_specs=pl.BlockSpec((tm, d), lambda i, pos: (i, 0)),
            scratch_shapes=[pltpu.VMEM((2, TOP_K, tm, d), F32), pltpu.SemaphoreType.DMA((2,))],
        ),
        out_shape=jax.ShapeDtypeStruct((st, d), BF16),
        compiler_params=_cparams("arbitrary"),
        name="moe_combine",
    )(pos, route, ys)


def moe_sparse(h, route, w_gu, b_gu, w_down, b_down):
    st = h.shape[0]
    n_exp = w_gu.shape[0]
    tg = MOE_TILE
    ids = route[:, :TOP_K].astype(jnp.int32)
    onehot = (ids[:, :, None] == jnp.arange(n_exp)[None, None, :]).any(axis=1)
    counts = jnp.sum(onehot, axis=0, dtype=jnp.int32)
    tiles = (counts + tg - 1) // tg
    tile_end = jnp.cumsum(tiles)
    row_off = (tile_end - tiles) * tg
    rank = jnp.cumsum(onehot.astype(jnp.int32), axis=0) - 1
    pos = row_off[ids] + jnp.take_along_axis(rank, ids, axis=1)
    n_tiles = (TOP_K * st + n_exp * (tg - 1)) // tg
    tok = jnp.zeros((n_tiles * tg,), jnp.int32).at[pos.reshape(-1)].set(
        jnp.repeat(jnp.arange(st, dtype=jnp.int32), TOP_K), unique_indices=True)
    tile_expert = jnp.minimum(jnp.searchsorted(tile_end, jnp.arange(n_tiles), side="right"),
                              n_exp - 1).astype(jnp.int32)
    ys = grouped_moe(h, tok, tile_expert, tile_end[-1:].astype(jnp.int32), w_gu, b_gu, w_down, b_down)
    return moe_combine(route, ys, pos.T.reshape(-1).astype(jnp.int32))


def _rope_tables(n_ctx, n_lat, dim):
    quarter = dim // 4
    inv_freq = ROPE_THETA ** (-jnp.arange(quarter, dtype=F32) / quarter)
    t = jnp.arange(n_lat)
    row = (t // GRID_W).astype(F32)
    col = (t % GRID_W).astype(F32)
    ang = jnp.stack([row[:, None] * inv_freq, col[:, None] * inv_freq], axis=1)
    ang = jnp.broadcast_to(ang[:, :, None, :], (n_lat, 2, 2, quarter)).reshape(n_lat, dim)
    cos, sin = jnp.cos(ang), jnp.sin(ang)
    first_half = (jnp.arange(dim) % (2 * quarter)) < quarter
    sin_lo = jnp.where(first_half, -sin, 0.0)
    sin_hi = jnp.where(first_half, 0.0, sin)

    def full(tab, fill):
        tab = jnp.pad(tab, ((0, 0), (0, LANES - dim)), constant_values=fill)
        return jnp.concatenate([jnp.full((n_ctx, LANES), fill, F32), tab], axis=0)

    return full(cos, 1.0), full(sin_lo, 0.0), full(sin_hi, 0.0)


def _pad_cols(w, n):
    return jnp.pad(w, ((0, 0), (0, n - w.shape[1])))


def _streams(v_ctx, v_lat):
    return jnp.stack([v_ctx, v_lat], axis=0)[:, None, :]


def _even_mixer(h, rope128, w_in, w_out, a_qn, a_kn, a_sink, b_qn, b_kn, b_lam, b_subln, lam_init, d):
    a_heads = d // (2 * HEAD_DIM)
    a_kv = a_heads // 4
    b_heads = d // (4 * HEAD_DIM)
    b_kv = b_heads // 2
    ga, gb = a_heads // a_kv, b_heads // b_kv
    scale = HEAD_DIM ** -0.5 * LOG2E
    n_qa, n_qb, n_ka, n_va, n_kb, n_vb = a_heads, 2 * b_heads, a_kv, a_kv, 2 * b_kv, 2 * b_kv
    c_qb = n_qa
    c_ka = c_qb + n_qb
    c_va = c_ka + n_ka
    c_kb = c_va + n_va
    c_vb = c_kb + n_kb
    p = matmul([h], [w_in.astype(BF16)], tn=512)
    gains = jnp.stack([a_qn, b_qn, a_kn, b_kn], axis=0).astype(F32)
    pn = prep_heads(p, rope128, gains,
                    segs=((0, c_qb, 0, scale), (c_qb, c_ka, 1, scale), (c_ka, c_va, 2, 1.0), (c_kb, c_vb, 3, 1.0)),
                    copies=((c_va, c_kb), (c_vb, c_vb + n_vb)))
    sink = jnp.broadcast_to((a_sink.astype(F32) * LOG2E).reshape(a_kv, ga, 1), (a_kv, ga, LANES))
    ya = window_attention(pn, sink, a_kv, ga, 0, c_ka, c_va)
    lf = b_lam.astype(F32)
    lam = jnp.exp(jnp.sum(lf[0] * lf[1])) - jnp.exp(jnp.sum(lf[2] * lf[3])) + lam_init
    dvb = 2 * HEAD_DIM
    aux = jnp.stack([jnp.full((dvb,), lam, F32), b_subln.astype(F32) * (1.0 - lam_init)], axis=0)
    flag = _score_bound_flag(b_qn, b_kn, HEAD_DIM, scale)
    yb = flash_attention(pn, pn, pn, aux, flag, n_kv=b_kv, g=gb, comps=2, dq=HEAD_DIM, dv=dvb,
                         c_q=c_qb * HEAD_DIM // (gb * 2 * HEAD_DIM), c_k=c_kb // 2, c_v=lambda hh: c_vb // 2 + hh,
                         tk=2048, diff=True)
    w_out = w_out.astype(BF16)
    na = a_heads * HEAD_DIM
    return matmul([ya, yb], [w_out[:na], w_out[na:]], tn=512)


def _odd_mixer(h, rope128, rope64, w_in, w_out, c_qn, c_kn, d_qa_norm, d_kva_norm, d_wq_up, d_wkv_up, d_qn, d_kn, d):
    st = h.shape[0]
    c_heads = d // (2 * HEAD_DIM)
    c_kv = c_heads // 4
    gc = c_heads // c_kv
    d_heads = d // (2 * HEAD_DIM)
    q_rank = d_wq_up.shape[0]
    kv_rank = d_wkv_up.shape[0]
    n_qc, n_qa, n_kc, n_vc, n_kva = c_heads, q_rank // LANES, c_kv, c_kv, kv_rank // LANES
    c_qa = n_qc
    c_kc = c_qa + n_qa
    c_vc = c_kc + n_kc
    c_kva = c_vc + n_vc
    c_kr = c_kva + n_kva
    n_in = (c_kr + 1) * LANES
    tn = 768
    n_pad = -(-n_in // tn) * tn
    p = matmul([h], [_pad_cols(w_in, n_pad).astype(BF16)], tn=tn)
    tm = ROW_TILE
    tab = pl.BlockSpec((tm, LANES), lambda i: (i, 0))
    gains = jnp.stack([c_qn, c_kn], axis=0).astype(F32)
    qk, qa, kva = pl.pallas_call(
        functools.partial(_prep_odd1_kernel, n_q=n_qc, n_k=n_kc, c_k=c_kc, c_qa=c_qa, n_qa=n_qa, c_kva=c_kva,
                          n_kva=n_kva, q_scale=HEAD_DIM ** -0.5 * LOG2E),
        grid=(st // tm,),
        in_specs=[pl.BlockSpec((tm, n_pad), lambda i: (i, 0)), tab, tab, tab,
                  pl.BlockSpec((2, LANES), lambda i: (0, 0)),
                  pl.BlockSpec((1, q_rank), lambda i: (0, 0)),
                  pl.BlockSpec((1, kv_rank), lambda i: (0, 0))],
        out_specs=[pl.BlockSpec((tm, (n_qc + n_kc) * LANES), lambda i: (i, 0)),
                   pl.BlockSpec((tm, q_rank), lambda i: (i, 0)),
                   pl.BlockSpec((tm, kv_rank), lambda i: (i, 0))],
        out_shape=[jax.ShapeDtypeStruct((st, (n_qc + n_kc) * LANES), BF16),
                   jax.ShapeDtypeStruct((st, q_rank), BF16),
                   jax.ShapeDtypeStruct((st, kv_rank), BF16)],
        compiler_params=_cparams("arbitrary"),
        name="prep_odd1",
    )(p, *rope128, gains, d_qa_norm.astype(F32)[None, :], d_kva_norm.astype(F32)[None, :])
    wq = jnp.pad(d_wq_up.reshape(q_rank, d_heads, D_QK), ((0, 0), (0, 0), (0, D_PAD - D_QK)))
    qd_raw = matmul([qa], [wq.reshape(q_rank, d_heads * D_PAD).astype(BF16)], tn=512)
    kv = matmul([kva], [d_wkv_up.astype(BF16)], tn=512)
    pad_gain = lambda gvec: jnp.pad(gvec.astype(F32), (0, D_PAD - D_QK))[None, :]
    qd, kd = pl.pallas_call(
        functools.partial(_prep_odd2_kernel, n_heads=d_heads, q_scale=D_QK ** -0.5 * LOG2E),
        grid=(st // tm,),
        in_specs=[pl.BlockSpec((tm, d_heads * D_PAD), lambda i: (i, 0)),
                  pl.BlockSpec((tm, d_heads * D_PAD), lambda i: (i, 0)),
                  pl.BlockSpec((tm, LANES), lambda i: (i, c_kr)),
                  tab, tab, tab,
                  pl.BlockSpec((1, D_PAD), lambda i: (0, 0)),
                  pl.BlockSpec((1, D_PAD), lambda i: (0, 0))],
        out_specs=[pl.BlockSpec((tm, d_heads * D_PAD), lambda i: (i, 0)),
                   pl.BlockSpec((tm, d_heads * D_PAD), lambda i: (i, 0))],
        out_shape=[jax.ShapeDtypeStruct((st, d_heads * D_PAD), BF16),
                   jax.ShapeDtypeStruct((st, d_heads * D_PAD), BF16)],
        compiler_params=_cparams("arbitrary"),
        name="prep_odd2",
    )(qd_raw, kv, p, *rope64, pad_gain(d_qn), pad_gain(d_kn))
    aux = jnp.zeros((8, LANES), F32)
    flag_c = _score_bound_flag(c_qn, c_kn, HEAD_DIM, HEAD_DIM ** -0.5 * LOG2E)
    flag_d = _score_bound_flag(d_qn, d_kn, D_QK, D_QK ** -0.5 * LOG2E)
    yc = flash_attention(qk, qk, p, aux, flag_c, n_kv=c_kv, g=gc, comps=1, dq=HEAD_DIM, dv=HEAD_DIM,
                         c_q=0, c_k=n_qc, c_v=lambda hh: c_vc + hh, tk=2048, key_major=True)
    yd = flash_attention(qd, kd, kv, aux, flag_d, n_kv=d_heads, g=1, comps=1, dq=D_PAD, dv=D_V,
                         c_q=0, c_k=0, c_v=lambda hh: 2 * hh + 1, tk=2048,
                         tq=1280 if st % 1280 == 0 else ROW_TILE, key_major=True)
    w_out = w_out.astype(BF16)
    nc = c_heads * HEAD_DIM
    return matmul([yc, yd], [w_out[:nc], w_out[nc:]], tn=512)


def kernel(x, c, ctx, c_ctx, adaln_down, adaln_up, adaln_b, norm_mix, norm_ffn, ev_w_in, ev_w_out, ev_a_qn, ev_a_kn,
           ev_a_sink, ev_b_qn, ev_b_kn, ev_b_lam, ev_b_subln, od_w_in, od_w_out, od_c_qn, od_c_kn, od_d_qa_norm,
           od_d_kva_norm, od_d_wq_up, od_d_wkv_up, od_d_qn, od_d_kn, router_w, router_b, moe_w_gu, moe_b_gu,
           moe_w_down, moe_b_down):
    bsz, seq, d = x.shape
    n_ctx = ctx.shape[1]
    depth = adaln_down.shape[0]
    n_exp = router_w.shape[2]
    assert bsz == 1 and n_ctx == ROW_TILE and seq % ROW_TILE == 0 and n_exp <= LANES

    rope128 = _rope_tables(n_ctx, seq, HEAD_DIM)
    rope64 = _rope_tables(n_ctx, seq, D_ROPE)
    cvecs = jnp.zeros((8, d), F32).at[0].set(c[0]).at[1].set(c_ctx)
    mods = adaln_all(cvecs, adaln_down, adaln_up, adaln_b)

    xs = jnp.concatenate([ctx[0], x[0]], axis=0)
    y_prev, gate_prev = None, None
    for l in range(depth):
        m_lat = mods[l, 0].reshape(N_MOD, d)
        m_ctx = mods[l, 1].reshape(N_MOD, d)
        mult = _streams(norm_mix[l] * (1.0 + m_ctx[1]), norm_mix[l] * (1.0 + m_lat[1]))
        shift = _streams(m_ctx[0], m_lat[0])
        if y_prev is None:
            (h,) = modulate(xs, mult, shift)
        else:
            xs, h = modulate(xs, mult, shift, res=(y_prev, gate_prev))
        j = l // 2
        if l % 2 == 0:
            lam_init = 0.8 - 0.6 * math.exp(-0.3 * l)
            y = _even_mixer(h, rope128, ev_w_in[j], ev_w_out[j], ev_a_qn[j], ev_a_kn[j], ev_a_sink[j], ev_b_qn[j],
                            ev_b_kn[j], ev_b_lam[j], ev_b_subln[j], lam_init, d)
        else:
            y = _odd_mixer(h, rope128, rope64, od_w_in[j], od_w_out[j], od_c_qn[j], od_c_kn[j], od_d_qa_norm[j],
                           od_d_kva_norm[j], od_d_wq_up[j], od_d_wkv_up[j], od_d_qn[j], od_d_kn[j], d)
        mult = _streams(norm_ffn[l] * (1.0 + m_ctx[4]), norm_ffn[l] * (1.0 + m_lat[4]))
        shift = _streams(m_ctx[3], m_lat[3])
        rw = _pad_cols(router_w[l].astype(F32), LANES)
        rb = jnp.pad(router_b[l].astype(F32), (0, LANES - n_exp))[None, :]
        xs, h2, route = modulate(xs, mult, shift, res=(y, _streams(m_ctx[2], m_lat[2])), router=(rw, rb, n_exp))
        y_prev = moe_sparse(h2, route, moe_w_gu[l].astype(BF16), moe_b_gu[l].astype(F32),
                            moe_w_down[l].astype(BF16), moe_b_down[l].astype(F32))
        gate_prev = _streams(m_ctx[5], m_lat[5])
    out = final_residual(xs, y_prev, gate_prev)
    return out[None]
```

```python
import functools
import math

import jax
import jax.numpy as jnp
from jax import lax
from jax.experimental import pallas as pl
from jax.experimental.pallas import tpu as pltpu

F32 = jnp.float32
BF16 = jnp.bfloat16

GRID_W = 64
HEAD_DIM = 128
ROPE_THETA = 10000.0
EPS = 1e-6
N_MOD = 6
D_NOPE = 128
D_ROPE = 64
D_QK = D_NOPE + D_ROPE
D_V = 128
D_PAD = 256
TOP_K = 4
SWIGLU_ALPHA = 1.702
SWIGLU_LIMIT = 7.0
LOG2E = 1.4426950408889634
NEG_BIG = -1e30
SCORE_BOUND = 60.0
BF16_NORM_SLACK = 1.02

LANES = 128
ROW_TILE = 256
VMEM_LIMIT = 56 * 1024 * 1024


def _cparams(*sem):
    return pltpu.CompilerParams(dimension_semantics=sem, vmem_limit_bytes=VMEM_LIMIT)


def _adaln_kernel(c_ref, down_ref, up_ref, b_ref, o_ref, t_ref):
    @pl.when(pl.program_id(1) == 0)
    def _():
        c = c_ref[...]
        a = c * (1.0 / (1.0 + jnp.exp(-c)))
        t_ref[...] = jnp.dot(a, down_ref[0], preferred_element_type=F32, precision=lax.Precision.HIGHEST)

    o_ref[0] = jnp.dot(t_ref[...], up_ref[0], preferred_element_type=F32,
                       precision=lax.Precision.HIGHEST) + b_ref[0]


def adaln_all(cvecs, down, up, bias):
    depth, d, rank = down.shape
    n = up.shape[2]
    tn = 2048
    assert n % tn == 0
    return pl.pallas_call(
        _adaln_kernel,
        grid=(depth, n // tn),
        in_specs=[
            pl.BlockSpec((8, d), lambda l, j: (0, 0)),
            pl.BlockSpec((1, d, rank), lambda l, j: (l, 0, 0)),
            pl.BlockSpec((1, rank, tn), lambda l, j: (l, 0, j)),
            pl.BlockSpec((1, 1, tn), lambda l, j: (l, 0, j)),
        ],
        out_specs=pl.BlockSpec((1, 8, tn), lambda l, j: (l, 0, j)),
        out_shape=jax.ShapeDtypeStruct((depth, 8, n), F32),
        scratch_shapes=[pltpu.VMEM((8, rank), F32)],
        compiler_params=_cparams("arbitrary", "arbitrary"),
        name="adaln",
    )(cvecs, down, up, bias.reshape(depth, 1, n))


def _topk_route(logits, n_exp):
    lane = lax.broadcasted_iota(jnp.int32, logits.shape, 1).astype(F32)
    work = jnp.where(lane < n_exp, logits, -jnp.inf)
    ids = jnp.zeros_like(logits)
    wts = jnp.zeros_like(logits)
    denom = None
    v0 = None
    for k in range(TOP_K):
        m = jnp.max(work, axis=-1, keepdims=True)
        idx = jnp.min(jnp.where(work == m, lane, float(LANES)), axis=-1, keepdims=True)
        if k == 0:
            v0 = m
            e = jnp.ones_like(m)
            denom = e
        else:
            e = jnp.exp(m - v0)
            denom = denom + e
        ids = jnp.where(lane == k, idx, ids)
        wts = jnp.where(lane == TOP_K + k, e, wts)
        work = jnp.where(lane == idx, -jnp.inf, work)
    return ids + wts / denom


def _modulate_kernel(*refs, has_res, n_exp):
    it = iter(refs)
    x_ref = next(it)
    y_ref = next(it) if has_res else None
    gate_ref = next(it) if has_res else None
    mult_ref = next(it)
    shift_ref = next(it)
    rw_ref = next(it) if n_exp else None
    rb_ref = next(it) if n_exp else None
    xo_ref = next(it) if has_res else None
    h_ref = next(it)
    g_ref = next(it) if n_exp else None

    x = x_ref[...]
    if has_res:
        x = x + gate_ref[0] * y_ref[...].astype(F32)
        xo_ref[...] = x
    ms = jnp.mean(x * x, axis=-1, keepdims=True)
    h = x * lax.rsqrt(ms + EPS) * mult_ref[0] + shift_ref[0]
    h_ref[...] = h.astype(h_ref.dtype)
    if n_exp:
        logits = jnp.dot(h, rw_ref[...], preferred_element_type=F32,
                         precision=lax.Precision.HIGHEST) + rb_ref[...]
        g_ref[...] = _topk_route(logits, n_exp)


def modulate(x, mult, shift, res=None, router=None):
    st, d = x.shape
    tm = ROW_TILE
    stream = lambda i: (jnp.minimum(i, 1), 0, 0)
    row = lambda i: (i, 0)
    vec_spec = pl.BlockSpec((1, 1, d), stream)
    args, in_specs = [x], [pl.BlockSpec((tm, d), row)]
    if res is not None:
        args += [res[0], res[1]]
        in_specs += [pl.BlockSpec((tm, d), row), vec_spec]
    args += [mult, shift]
    in_specs += [vec_spec, vec_spec]
    n_exp = 0
    if router is not None:
        args += [router[0], router[1]]
        n_exp = router[2]
        in_specs += [pl.BlockSpec((d, LANES), lambda i: (0, 0)), pl.BlockSpec((1, LANES), lambda i: (0, 0))]
    out_shape, out_specs = [], []
    if res is not None:
        out_shape.append(jax.ShapeDtypeStruct((st, d), F32))
        out_specs.append(pl.BlockSpec((tm, d), row))
    out_shape.append(jax.ShapeDtypeStruct((st, d), BF16 if router is None else F32))
    out_specs.append(pl.BlockSpec((tm, d), row))
    if router is not None:
        out_shape.append(jax.ShapeDtypeStruct((st, LANES), F32))
        out_specs.append(pl.BlockSpec((tm, LANES), row))
    return pl.pallas_call(
        functools.partial(_modulate_kernel, has_res=res is not None, n_exp=n_exp),
        grid=(st // tm,),
        in_specs=in_specs,
        out_specs=out_specs,
        out_shape=out_shape,
        compiler_params=_cparams("arbitrary"),
        name="modulate",
    )(*args)


def _final_residual_kernel(x_ref, y_ref, gate_ref, o_ref):
    o_ref[...] = x_ref[...] + gate_ref[0] * y_ref[...].astype(F32)


def final_residual(x, y, gate):
    st, d = x.shape
    tm = ROW_TILE
    n = st // tm - 1
    return pl.pallas_call(
        _final_residual_kernel,
        grid=(n,),
        in_specs=[
            pl.BlockSpec((tm, d), lambda i: (i + 1, 0)),
            pl.BlockSpec((tm, d), lambda i: (i + 1, 0)),
            pl.BlockSpec((1, 1, d), lambda i: (1, 0, 0)),
        ],
        out_specs=pl.BlockSpec((tm, d), lambda i: (i, 0)),
        out_shape=jax.ShapeDtypeStruct((n * tm, d), F32),
        compiler_params=_cparams("arbitrary"),
        name="final_residual",
    )(x, y, gate)


def _mm_kernel(*refs, n_pairs):
    o_ref = refs[-1]
    acc = None
    for p in range(n_pairs):
        t = jnp.dot(refs[p][...], refs[n_pairs + p][...], preferred_element_type=F32)
        acc = t if acc is None else acc + t
    o_ref[...] = acc.astype(o_ref.dtype)


def _row_block(m):
    for tm in (1280, 1024, 512, 256):
        if m % tm == 0:
            return tm
    raise ValueError(m)


def matmul(a_list, b_list, tn, out_dtype=BF16):
    m = a_list[0].shape[0]
    n = b_list[0].shape[1]
    tm = _row_block(m)
    while n % tn:
        tn -= LANES
    in_specs = [pl.BlockSpec((tm, a.shape[1]), lambda i, j: (i, 0)) for a in a_list]
    in_specs += [pl.BlockSpec((b.shape[0], tn), lambda i, j: (0, j)) for b in b_list]
    return pl.pallas_call(
        functools.partial(_mm_kernel, n_pairs=len(a_list)),
        grid=(m // tm, n // tn),
        in_specs=in_specs,
        out_specs=pl.BlockSpec((tm, tn), lambda i, j: (i, j)),
        out_shape=jax.ShapeDtypeStruct((m, n), out_dtype),
        compiler_params=_cparams("arbitrary", "arbitrary"),
        name="matmul",
    )(*a_list, *b_list)


def _rope(y, cos, sin_lo, sin_hi, quarter):
    left = pltpu.roll(y, LANES - quarter, 1)
    right = pltpu.roll(y, quarter, 1)
    return y * cos + left * sin_lo + right * sin_hi


def _head_norm(x, gain, n_valid):
    ss = jnp.sum(x * x, axis=-1, keepdims=True)
    return x * lax.rsqrt(ss * (1.0 / n_valid) + EPS) * gain


def _prep_even_kernel(p_ref, cos_ref, slo_ref, shi_ref, gains_ref, o_ref, *, segs, n_copy_from):
    cos, slo, shi = cos_ref[...], slo_ref[...], shi_ref[...]
    for (c0, c1, gi, scale) in segs:
        gain = gains_ref[gi:gi + 1, :]
        for c in range(c0, c1):
            x = p_ref[:, c * LANES:(c + 1) * LANES].astype(F32)
            y = _rope(_head_norm(x, gain, HEAD_DIM), cos, slo, shi, HEAD_DIM // 4)
            if scale != 1.0:
                y = y * scale
            o_ref[:, c * LANES:(c + 1) * LANES] = y.astype(BF16)
    for (c0, c1) in n_copy_from:
        o_ref[:, c0 * LANES:c1 * LANES] = p_ref[:, c0 * LANES:c1 * LANES]


def prep_heads(p, rope128, gains, segs, copies):
    st, n = p.shape
    tm = ROW_TILE
    tab = pl.BlockSpec((tm, LANES), lambda i: (i, 0))
    return pl.pallas_call(
        functools.partial(_prep_even_kernel, segs=segs, n_copy_from=copies),
        grid=(st // tm,),
        in_specs=[pl.BlockSpec((tm, n), lambda i: (i, 0)), tab, tab, tab,
                  pl.BlockSpec(gains.shape, lambda i: (0, 0))],
        out_specs=pl.BlockSpec((tm, n), lambda i: (i, 0)),
        out_shape=jax.ShapeDtypeStruct((st, n), BF16),
        compiler_params=_cparams("arbitrary"),
        name="prep_heads",
    )(p, *rope128, gains)


def _prep_odd1_kernel(p_ref, cos_ref, slo_ref, shi_ref, gains_ref, gqa_ref, gkva_ref,
                      qk_ref, qa_ref, kva_ref, *, n_q, n_k, c_k, c_qa, n_qa, c_kva, n_kva, q_scale):
    cos, slo, shi = cos_ref[...], slo_ref[...], shi_ref[...]
    for c in range(n_q):
        x = p_ref[:, c * LANES:(c + 1) * LANES].astype(F32)
        y = _rope(_head_norm(x, gains_ref[0:1, :], HEAD_DIM), cos, slo, shi, HEAD_DIM // 4) * q_scale
        qk_ref[:, c * LANES:(c + 1) * LANES] = y.astype(BF16)
    for c in range(n_k):
        x = p_ref[:, (c_k + c) * LANES:(c_k + c + 1) * LANES].astype(F32)
        y = _rope(_head_norm(x, gains_ref[1:2, :], HEAD_DIM), cos, slo, shi, HEAD_DIM // 4)
        qk_ref[:, (n_q + c) * LANES:(n_q + c + 1) * LANES] = y.astype(BF16)
    xa = p_ref[:, c_qa * LANES:(c_qa + n_qa) * LANES].astype(F32)
    qa_ref[...] = _head_norm(xa, gqa_ref[...], n_qa * LANES).astype(BF16)
    xk = p_ref[:, c_kva * LANES:(c_kva + n_kva) * LANES].astype(F32)
    kva_ref[...] = _head_norm(xk, gkva_ref[...], n_kva * LANES).astype(BF16)


def _prep_odd2_kernel(qd_ref, kv_ref, kr_ref, cos_ref, slo_ref, shi_ref, gq_ref, gk_ref,
                      qo_ref, ko_ref, *, n_heads, q_scale):
    cos, slo, shi = cos_ref[...], slo_ref[...], shi_ref[...]
    gq_n, gq_r = gq_ref[:, :LANES], gq_ref[:, LANES:]
    gk_n, gk_r = gk_ref[:, :LANES], gk_ref[:, LANES:]
    kr = kr_ref[...].astype(F32)
    kr_ss = jnp.sum(kr * kr, axis=-1, keepdims=True)
    for h in range(n_heads):
        b = h * D_PAD
        qn = qd_ref[:, b:b + LANES].astype(F32)
        qr = qd_ref[:, b + LANES:b + D_PAD].astype(F32)
        ss = jnp.sum(qn * qn, axis=-1, keepdims=True) + jnp.sum(qr * qr, axis=-1, keepdims=True)
        r = lax.rsqrt(ss * (1.0 / D_QK) + EPS) * q_scale
        qo_ref[:, b:b + LANES] = (qn * r * gq_n).astype(BF16)
        qo_ref[:, b + LANES:b + D_PAD] = _rope(qr * r * gq_r, cos, slo, shi, D_ROPE // 4).astype(BF16)
        kn = kv_ref[:, b:b + LANES].astype(F32)
        ss = jnp.sum(kn * kn, axis=-1, keepdims=True) + kr_ss
        r = lax.rsqrt(ss * (1.0 / D_QK) + EPS)
        ko_ref[:, b:b + LANES] = (kn * r * gk_n).astype(BF16)
        ko_ref[:, b + LANES:b + D_PAD] = _rope(kr * r * gk_r, cos, slo, shi, D_ROPE // 4).astype(BF16)


def _stack_heads(q_ref, g, d):
    return jnp.concatenate([q_ref[:, i * d:(i + 1) * d] for i in range(g)], axis=0)


def _qk(q, k):
    return lax.dot_general(q, k, (((1,), (1,)), ((), ())), preferred_element_type=F32)


def _window_attn_kernel(q_ref, kp_ref, km_ref, kn_ref, kc_ref, vp_ref, vm_ref, vn_ref, vc_ref, sink_ref,
                        o_ref, *, g, st):
    i = pl.program_id(1)
    tq = q_ref.shape[0]
    half = tq // 2
    qs = _stack_heads(q_ref, g, HEAD_DIM)
    k_loc = jnp.concatenate([kp_ref[...], km_ref[...], kn_ref[...]], axis=0)
    v_loc = jnp.concatenate([vp_ref[...], vm_ref[...], vn_ref[...]], axis=0)
    s_loc = _qk(qs, k_loc)
    s_ctx = _qk(qs, kc_ref[...])
    a = lax.broadcasted_iota(jnp.int32, (g * tq, 1), 0) & (tq - 1)
    j = lax.broadcasted_iota(jnp.int32, s_loc.shape, 1)
    j_lo = tq + half - i * tq
    j_hi = jnp.where(i >= 1, st + half - i * tq, j_lo) - 1
    valid = (j >= jnp.maximum(a, j_lo)) & (j <= jnp.minimum(a + tq, j_hi))
    s_loc = jnp.where(valid, s_loc, NEG_BIG)
    sink = jnp.concatenate([jnp.broadcast_to(sink_ref[0, h:h + 1, 0:1], (tq, 1)) for h in range(g)], axis=0)
    m = jnp.maximum(jnp.maximum(jnp.max(s_loc, axis=-1, keepdims=True),
                                jnp.max(s_ctx, axis=-1, keepdims=True)), sink)
    p_loc = jnp.exp2(s_loc - m)
    p_ctx = jnp.exp2(s_ctx - m)
    l = jnp.sum(p_loc, axis=-1, keepdims=True) + jnp.sum(p_ctx, axis=-1, keepdims=True) + jnp.exp2(sink - m)
    o = (jnp.dot(p_loc.astype(BF16), v_loc, preferred_element_type=F32)
         + jnp.dot(p_ctx.astype(BF16), vc_ref[...], preferred_element_type=F32)) / l
    for h in range(g):
        o_ref[:, h * HEAD_DIM:(h + 1) * HEAD_DIM] = o[h * tq:(h + 1) * tq].astype(BF16)


def window_attention(pn, sink, n_kv, g, c_q, c_k, c_v):
    st = pn.shape[0]
    tq = ROW_TILE
    half = tq // 2
    nh = st // half
    q_spec = pl.BlockSpec((tq, g * HEAD_DIM), lambda h, i: (i, c_q // g + h))

    def kv_specs(c0):
        return [
            pl.BlockSpec((half, HEAD_DIM), lambda h, i: (jnp.maximum(2 * i - 1, 0), c0 + h)),
            pl.BlockSpec((tq, HEAD_DIM), lambda h, i: (i, c0 + h)),
            pl.BlockSpec((half, HEAD_DIM), lambda h, i: (jnp.minimum(2 * i + 2, nh - 1), c0 + h)),
            pl.BlockSpec((tq, HEAD_DIM), lambda h, i: (0, c0 + h)),
        ]

    return pl.pallas_call(
        functools.partial(_window_attn_kernel, g=g, st=st),
        grid=(n_kv, st // tq),
        in_specs=[q_spec] + kv_specs(c_k) + kv_specs(c_v)
        + [pl.BlockSpec((1, g, LANES), lambda h, i: (h, 0, 0))],
        out_specs=pl.BlockSpec((tq, g * HEAD_DIM), lambda h, i: (i, h)),
        out_shape=jax.ShapeDtypeStruct((st, n_kv * g * HEAD_DIM), BF16),
        compiler_params=_cparams("arbitrary", "arbitrary"),
        name="window_attention",
    )(pn, pn, pn, pn, pn, pn, pn, pn, pn, sink)


def _flash_kernel(flag_ref, q_ref, k_ref, v_ref, aux_ref, o_ref, acc_ref, l_ref, accc_ref, lc_ref,
                  vt_ref, acct_ref, lt_ref, *, g, comps, dq, dv, tk, n_ctx, diff, key_major):
    qi = pl.program_id(1)
    tq = q_ref.shape[0]
    n_keys = k_ref.shape[0]
    n_lat = (n_keys - n_ctx) // tk
    ctx_only_tile = tq == n_ctx
    n_iter = jnp.where(qi == 0, 0, n_lat) if ctx_only_tile else n_lat
    n_stacked = g * comps
    qs = [jnp.concatenate([q_ref[:, (gi * comps + c) * dq:(gi * comps + c + 1) * dq] for gi in range(g)], axis=0)
          for c in range(comps)]

    def scores(start, size):
        return jnp.concatenate(
            [_qk(qs[c], k_ref[pl.ds(start, size), c * dq:(c + 1) * dq]) for c in range(comps)], axis=0)

    def chunk_start(j):
        return pl.multiple_of(n_ctx + j * tk, math.gcd(n_ctx, tk))

    def keep_context_rows(l_is_partial):
        if ctx_only_tile:
            return

        @pl.when(qi == 0)
        def _():
            for h in range(n_stacked):
                accc_ref[h * n_ctx:(h + 1) * n_ctx] = acc_ref[h * tq:h * tq + n_ctx]
                lrows = l_ref[h * tq:h * tq + n_ctx]
                if l_is_partial:
                    lrows = jnp.broadcast_to(jnp.sum(lrows, axis=-1, keepdims=True), lrows.shape)
                lc_ref[h * n_ctx:(h + 1) * n_ctx] = lrows

    def bounded_key_major():
        rows = g * tq
        key_tile = 256

        @pl.when(qi == 0)
        def _():
            def put(i, carry):
                s0 = pl.multiple_of(i * key_tile, key_tile)
                vt_ref[:, pl.ds(s0, key_tile)] = v_ref[pl.ds(s0, key_tile), :].astype(F32).T.astype(BF16)
                return carry

            lax.fori_loop(0, n_keys // key_tile, put, 0)

        def step(start, size, first):
            pt = jnp.exp2(_qk(k_ref[pl.ds(start, size), :], qs[0]))
            psum = jnp.sum(pt.reshape(size // 8, 8, rows), axis=0)
            pvt = jnp.dot(vt_ref[:, pl.ds(start, size)], pt.astype(BF16), preferred_element_type=F32)
            if first:
                acct_ref[...] = pvt
                lt_ref[...] = psum
            else:
                acct_ref[...] += pvt
                lt_ref[...] += psum

        def normalised_rows():
            return (acct_ref[...] / jnp.sum(lt_ref[...], axis=0, keepdims=True)).T

        step(0, n_ctx, True)
        if not ctx_only_tile:
            @pl.when(qi == 0)
            def _():
                o = normalised_rows()
                for h in range(g):
                    accc_ref[h * n_ctx:(h + 1) * n_ctx] = o[h * tq:h * tq + n_ctx]
                lc_ref[...] = jnp.ones_like(lc_ref)

        def body(j, carry):
            step(chunk_start(j), tk, False)
            return carry

        lax.fori_loop(0, n_iter, body, 0)
        acc_ref[...] = normalised_rows()
        l_ref[...] = jnp.ones_like(l_ref)

    @pl.when(flag_ref[0] == 1)
    def _bounded():
        if key_major:
            bounded_key_major()
            return

        def step(start, size, first):
            p = jnp.exp2(scores(start, size))
            psum = p[:, :LANES]
            for b in range(1, size // LANES):
                psum = psum + p[:, b * LANES:(b + 1) * LANES]
            pv = jnp.dot(p.astype(BF16), v_ref[pl.ds(start, size), :], preferred_element_type=F32)
            if first:
                acc_ref[...] = pv
                l_ref[...] = psum
            else:
                acc_ref[...] += pv
                l_ref[...] += psum

        step(0, n_ctx, True)
        keep_context_rows(True)

        def body(j, carry):
            step(chunk_start(j), tk, False)
            return carry

        lax.fori_loop(0, n_iter, body, 0)
        l_ref[...] = jnp.broadcast_to(jnp.sum(l_ref[...], axis=-1, keepdims=True), l_ref.shape)

    @pl.when(flag_ref[0] != 1)
    def _online():
        def step(start, size, m, l, first):
            s = scores(start, size)
            m_cur = jnp.max(s, axis=-1, keepdims=True)
            m_new = m_cur if first else jnp.maximum(m, m_cur)
            p = jnp.exp2(s - m_new)
            pv = jnp.dot(p.astype(BF16), v_ref[pl.ds(start, size), :], preferred_element_type=F32)
            if first:
                l_new = jnp.sum(p, axis=-1, keepdims=True)
                acc_ref[...] = pv
            else:
                alpha = jnp.exp2(m - m_new)
                l_new = alpha * l + jnp.sum(p, axis=-1, keepdims=True)
                acc_ref[...] = alpha * acc_ref[...] + pv
            return m_new, l_new

        m, l = step(0, n_ctx, None, None, True)
        l_ref[...] = jnp.broadcast_to(l, l_ref.shape)
        keep_context_rows(False)
        m, l = lax.fori_loop(0, n_iter, lambda j, c: step(chunk_start(j), tk, c[0], c[1], False), (m, l))
        l_ref[...] = jnp.broadcast_to(l, l_ref.shape)

    def finalize(acc, l, rows):
        o = acc / l
        if diff:
            lam = aux_ref[0:1, 0:1]
            for gi in range(g):
                y = o[gi * rows:(gi + 1) * rows] - lam * o[(g + gi) * rows:(g + gi + 1) * rows]
                ss = jnp.mean(y * y, axis=-1, keepdims=True)
                y = y * lax.rsqrt(ss + EPS) * aux_ref[1:2, :]
                o_ref[0:rows, gi * dv:(gi + 1) * dv] = y.astype(BF16)
        else:
            for gi in range(g):
                o_ref[0:rows, gi * dv:(gi + 1) * dv] = o[gi * rows:(gi + 1) * rows].astype(BF16)

    finalize(acc_ref[...], l_ref[:, 0:1], tq)
    if not ctx_only_tile:
        @pl.when(qi == 0)
        def _():
            finalize(accc_ref[...], lc_ref[:, 0:1], n_ctx)


def _score_bound_flag(q_gain, k_gain, n_norm, scale):
    bound = n_norm * jnp.max(jnp.abs(q_gain)) * jnp.max(jnp.abs(k_gain)) * scale * BF16_NORM_SLACK
    return (bound <= SCORE_BOUND).astype(jnp.int32).reshape(1)


def flash_attention(q_arr, k_arr, v_arr, aux, flag, *, n_kv, g, comps, dq, dv, c_q, c_k, c_v, tk, tq=ROW_TILE,
                    diff=False, key_major=False):
    st = q_arr.shape[0]
    n_ctx = ROW_TILE
    assert st % tq == 0 and tq % n_ctx == 0
    assert not key_major or (comps == 1 and not diff)
    while (st - n_ctx) % tk:
        tk //= 2
    rows = g * comps * tq
    ctx_rows = g * comps * n_ctx if tq != n_ctx else 8
    km_shapes = [(dv, st), (dv, rows), (8, rows)] if key_major else [(16, LANES), (8, LANES), (8, LANES)]
    return pl.pallas_call(
        functools.partial(_flash_kernel, g=g, comps=comps, dq=dq, dv=dv, tk=tk, n_ctx=n_ctx, diff=diff,
                          key_major=key_major),
        grid=(n_kv, st // tq),
        in_specs=[
            pl.BlockSpec(memory_space=pltpu.SMEM),
            pl.BlockSpec((tq, g * comps * dq), lambda h, i: (i, c_q + h)),
            pl.BlockSpec((st, comps * dq), lambda h, i: (0, c_k + h)),
            pl.BlockSpec((st, dv), lambda h, i: (0, c_v(h))),
            pl.BlockSpec(aux.shape, lambda h, i: (0, 0)),
        ],
        out_specs=pl.BlockSpec((tq, g * dv), lambda h, i: (i, h)),
        out_shape=jax.ShapeDtypeStruct((st, n_kv * g * dv), BF16),
        scratch_shapes=[pltpu.VMEM((rows, dv), F32), pltpu.VMEM((rows, LANES), F32),
                        pltpu.VMEM((ctx_rows, dv), F32), pltpu.VMEM((ctx_rows, LANES), F32),
                        pltpu.VMEM(km_shapes[0], BF16), pltpu.VMEM(km_shapes[1], F32),
                        pltpu.VMEM(km_shapes[2], F32)],
        compiler_params=_cparams("arbitrary", "arbitrary"),
        name="flash_attention",
    )(flag, q_arr, k_arr, v_arr, aux)


MOE_TILE = 256


GATHER_UNROLL = 8


def _start_row_gather(idx_ref, base, n, src_ref, dst_ref, sem):
    def issue(r, carry):
        pltpu.make_async_copy(src_ref.at[pl.ds(idx_ref[base + r], 1)], dst_ref.at[pl.ds(r, 1)], sem).start()
        return carry

    lax.fori_loop(0, n, issue, 0, unroll=GATHER_UNROLL)


def _wait_row_gather(n, src_ref, dst_ref, sem):
    def drain(r, carry):
        pltpu.make_async_copy(src_ref.at[pl.ds(0, 1)], dst_ref.at[pl.ds(0, 1)], sem).wait()
        return carry

    lax.fori_loop(0, n, drain, 0, unroll=GATHER_UNROLL)


def _grouped_moe_kernel(te_ref, nt_ref, tok_ref, h_ref, wgu_ref, bgu_ref, wd_ref, bd_ref, o_ref, x_ref, sem, *, ff):
    t = pl.program_id(0)
    tg = o_ref.shape[0]
    slot = lax.rem(t, 2)

    @pl.when(t < nt_ref[0])
    def _():
        @pl.when(t == 0)
        def _():
            _start_row_gather(tok_ref, 0, tg, h_ref, x_ref.at[0], sem.at[0])

        @pl.when(t + 1 < nt_ref[0])
        def _():
            _start_row_gather(tok_ref, (t + 1) * tg, tg, h_ref, x_ref.at[1 - slot], sem.at[1 - slot])

        _wait_row_gather(tg, h_ref, x_ref.at[slot], sem.at[slot])
        gu = jnp.dot(x_ref[slot].astype(BF16), wgu_ref[0], preferred_element_type=F32) + bgu_ref[0]
        glu = jnp.minimum(gu[:, :ff], SWIGLU_LIMIT)
        lin = jnp.clip(gu[:, ff:], -SWIGLU_LIMIT, SWIGLU_LIMIT)
        act = glu * (1.0 / (1.0 + jnp.exp(-SWIGLU_ALPHA * glu))) * (lin + 1.0)
        o_ref[...] = jnp.dot(act.astype(BF16), wd_ref[0], preferred_element_type=F32) + bd_ref[0]

    @pl.when(t >= nt_ref[0])
    def _():
        o_ref[...] = jnp.zeros_like(o_ref)


def grouped_moe(h, tok, tile_expert, n_tiles_used, w_gu, b_gu, w_down, b_down):
    p = tok.shape[0]
    d = h.shape[1]
    n_exp, _, ff2 = w_gu.shape
    ff = ff2 // 2
    tg = MOE_TILE
    return pl.pallas_call(
        functools.partial(_grouped_moe_kernel, ff=ff),
        grid_spec=pltpu.PrefetchScalarGridSpec(
            num_scalar_prefetch=3,
            grid=(p // tg,),
            in_specs=[
                pl.BlockSpec(memory_space=pl.ANY),
                pl.BlockSpec((1, d, ff2), lambda t, te, nt, tok: (te[t], 0, 0)),
                pl.BlockSpec((1, 1, ff2), lambda t, te, nt, tok: (te[t], 0, 0)),
                pl.BlockSpec((1, ff, d), lambda t, te, nt, tok: (te[t], 0, 0)),
                pl.BlockSpec((1, 1, d), lambda t, te, nt, tok: (te[t], 0, 0)),
            ],
            out_specs=pl.BlockSpec((tg, d), lambda t, te, nt, tok: (t, 0)),
            scratch_shapes=[pltpu.VMEM((2, tg, d), F32), pltpu.SemaphoreType.DMA((2,))],
        ),
        out_shape=jax.ShapeDtypeStruct((p, d), F32),
        compiler_params=_cparams("arbitrary"),
        name="grouped_moe",
    )(tile_expert, n_tiles_used, tok, h, w_gu, b_gu.reshape(n_exp, 1, ff2), w_down, b_down.reshape(n_exp, 1, d))


COMBINE_TILE = 128


def _combine_kernel(pos_ref, route_ref, ys_ref, o_ref, buf_ref, sem, *, st):
    tm = o_ref.shape[0]
    i = pl.program_id(0)
    slot = lax.rem(i, 2)

    def start(step, s):
        for k in range(TOP_K):
            _start_row_gather(pos_ref, k * st + step * tm, tm, ys_ref, buf_ref.at[s, k], sem.at[s])

    @pl.when(i == 0)
    def _():
        start(0, 0)

    @pl.when(i + 1 < pl.num_programs(0))
    def _():
        start(i + 1, 1 - slot)

    _wait_row_gather(TOP_K * tm, ys_ref, buf_ref.at[slot, 0], sem.at[slot])
    route = route_ref[...]
    acc = None
    for k in range(TOP_K):
        term = route[:, TOP_K + k:TOP_K + k + 1] * buf_ref[slot, k]
        acc = term if acc is None else acc + term
    o_ref[...] = acc.astype(o_ref.dtype)


def moe_combine(route, ys, pos):
    st = route.shape[0]
    d = ys.shape[1]
    tm = COMBINE_TILE
    return pl.pallas_call(
        functools.partial(_combine_kernel, st=st),
        grid_spec=pltpu.PrefetchScalarGridSpec(
            num_scalar_prefetch=1,
            grid=(st // tm,),
            in_specs=[pl.BlockSpec((tm, LANES), lambda i, pos: (i, 0)), pl.BlockSpec(memory_space=pl.ANY)],
            out_specs=pl.BlockSpec((tm, d), lambda i, pos: (i, 0)),
            scratch_shapes=[pltpu.VMEM((2, TOP_K, tm, d), F32), pltpu.SemaphoreType.DMA((2,))],
        ),
        out_shape=jax.ShapeDtypeStruct((st, d), BF16),
        compiler_params=_cparams("arbitrary"),
        name="moe_combine",
    )(pos, route, ys)


def moe_sparse(h, route, w_gu, b_gu, w_down, b_down):
    st = h.shape[0]
    n_exp = w_gu.shape[0]
    tg = MOE_TILE
    ids = route[:, :TOP_K].astype(jnp.int32)
    onehot = (ids[:, :, None] == jnp.arange(n_exp)[None, None, :]).any(axis=1)
    counts = jnp.sum(onehot, axis=0, dtype=jnp.int32)
    tiles = (counts + tg - 1) // tg
    tile_end = jnp.cumsum(tiles)
    row_off = (tile_end - tiles) * tg
    rank = jnp.cumsum(onehot.astype(jnp.int32), axis=0) - 1
    pos = row_off[ids] + jnp.take_along_axis(rank, ids, axis=1)
    n_tiles = (TOP_K * st + n_exp * (tg - 1)) // tg
    tok = jnp.zeros((n_tiles * tg,), jnp.int32).at[pos.reshape(-1)].set(
        jnp.repeat(jnp.arange(st, dtype=jnp.int32), TOP_K), unique_indices=True)
    tile_expert = jnp.minimum(jnp.searchsorted(tile_end, jnp.arange(n_tiles), side="right"),
                              n_exp - 1).astype(jnp.int32)
    ys = grouped_moe(h, tok, tile_expert, tile_end[-1:].astype(jnp.int32), w_gu, b_gu, w_down, b_down)
    return moe_combine(route, ys, pos.T.reshape(-1).astype(jnp.int32))


def _rope_tables(n_ctx, n_lat, dim):
    quarter = dim // 4
    inv_freq = ROPE_THETA ** (-jnp.arange(quarter, dtype=F32) / quarter)
    t = jnp.arange(n_lat)
    row = (t // GRID_W).astype(F32)
    col = (t % GRID_W).astype(F32)
    ang = jnp.stack([row[:, None] * inv_freq, col[:, None] * inv_freq], axis=1)
    ang = jnp.broadcast_to(ang[:, :, None, :], (n_lat, 2, 2, quarter)).reshape(n_lat, dim)
    cos, sin = jnp.cos(ang), jnp.sin(ang)
    first_half = (jnp.arange(dim) % (2 * quarter)) < quarter
    sin_lo = jnp.where(first_half, -sin, 0.0)
    sin_hi = jnp.where(first_half, 0.0, sin)

    def full(tab, fill):
        tab = jnp.pad(tab, ((0, 0), (0, LANES - dim)), constant_values=fill)
        return jnp.concatenate([jnp.full((n_ctx, LANES), fill, F32), tab], axis=0)

    return full(cos, 1.0), full(sin_lo, 0.0), full(sin_hi, 0.0)


def _pad_cols(w, n):
    return jnp.pad(w, ((0, 0), (0, n - w.shape[1])))


def _streams(v_ctx, v_lat):
    return jnp.stack([v_ctx, v_lat], axis=0)[:, None, :]


def _even_mixer(h, rope128, w_in, w_out, a_qn, a_kn, a_sink, b_qn, b_kn, b_lam, b_subln, lam_init, d):
    a_heads = d // (2 * HEAD_DIM)
    a_kv = a_heads // 4
    b_heads = d // (4 * HEAD_DIM)
    b_kv = b_heads // 2
    ga, gb = a_heads // a_kv, b_heads // b_kv
    scale = HEAD_DIM ** -0.5 * LOG2E
    n_qa, n_qb, n_ka, n_va, n_kb, n_vb = a_heads, 2 * b_heads, a_kv, a_kv, 2 * b_kv, 2 * b_kv
    c_qb = n_qa
    c_ka = c_qb + n_qb
    c_va = c_ka + n_ka
    c_kb = c_va + n_va
    c_vb = c_kb + n_kb
    p = matmul([h], [w_in.astype(BF16)], tn=512)
    gains = jnp.stack([a_qn, b_qn, a_kn, b_kn], axis=0).astype(F32)
    pn = prep_heads(p, rope128, gains,
                    segs=((0, c_qb, 0, scale), (c_qb, c_ka, 1, scale), (c_ka, c_va, 2, 1.0), (c_kb, c_vb, 3, 1.0)),
                    copies=((c_va, c_kb), (c_vb, c_vb + n_vb)))
    sink = jnp.broadcast_to((a_sink.astype(F32) * LOG2E).reshape(a_kv, ga, 1), (a_kv, ga, LANES))
    ya = window_attention(pn, sink, a_kv, ga, 0, c_ka, c_va)
    lf = b_lam.astype(F32)
    lam = jnp.exp(jnp.sum(lf[0] * lf[1])) - jnp.exp(jnp.sum(lf[2] * lf[3])) + lam_init
    dvb = 2 * HEAD_DIM
    aux = jnp.stack([jnp.full((dvb,), lam, F32), b_subln.astype(F32) * (1.0 - lam_init)], axis=0)
    flag = _score_bound_flag(b_qn, b_kn, HEAD_DIM, scale)
    yb = flash_attention(pn, pn, pn, aux, flag, n_kv=b_kv, g=gb, comps=2, dq=HEAD_DIM, dv=dvb,
                         c_q=c_qb * HEAD_DIM // (gb * 2 * HEAD_DIM), c_k=c_kb // 2, c_v=lambda hh: c_vb // 2 + hh,
                         tk=2048, diff=True)
    w_out = w_out.astype(BF16)
    na = a_heads * HEAD_DIM
    return matmul([ya, yb], [w_out[:na], w_out[na:]], tn=512)


def _odd_mixer(h, rope128, rope64, w_in, w_out, c_qn, c_kn, d_qa_norm, d_kva_norm, d_wq_up, d_wkv_up, d_qn, d_kn, d):
    st = h.shape[0]
    c_heads = d // (2 * HEAD_DIM)
    c_kv = c_heads // 4
    gc = c_heads // c_kv
    d_heads = d // (2 * HEAD_DIM)
    q_rank = d_wq_up.shape[0]
    kv_rank = d_wkv_up.shape[0]
    n_qc, n_qa, n_kc, n_vc, n_kva = c_heads, q_rank // LANES, c_kv, c_kv, kv_rank // LANES
    c_qa = n_qc
    c_kc = c_qa + n_qa
    c_vc = c_kc + n_kc
    c_kva = c_vc + n_vc
    c_kr = c_kva + n_kva
    n_in = (c_kr + 1) * LANES
    tn = 768
    n_pad = -(-n_in // tn) * tn
    p = matmul([h], [_pad_cols(w_in, n_pad).astype(BF16)], tn=tn)
    tm = ROW_TILE
    tab = pl.BlockSpec((tm, LANES), lambda i: (i, 0))
    gains = jnp.stack([c_qn, c_kn], axis=0).astype(F32)
    qk, qa, kva = pl.pallas_call(
        functools.partial(_prep_odd1_kernel, n_q=n_qc, n_k=n_kc, c_k=c_kc, c_qa=c_qa, n_qa=n_qa, c_kva=c_kva,
                          n_kva=n_kva, q_scale=HEAD_DIM ** -0.5 * LOG2E),
        grid=(st // tm,),
        in_specs=[pl.BlockSpec((tm, n_pad), lambda i: (i, 0)), tab, tab, tab,
                  pl.BlockSpec((2, LANES), lambda i: (0, 0)),
                  pl.BlockSpec((1, q_rank), lambda i: (0, 0)),
                  pl.BlockSpec((1, kv_rank), lambda i: (0, 0))],
        out_specs=[pl.BlockSpec((tm, (n_qc + n_kc) * LANES), lambda i: (i, 0)),
                   pl.BlockSpec((tm, q_rank), lambda i: (i, 0)),
                   pl.BlockSpec((tm, kv_rank), lambda i: (i, 0))],
        out_shape=[jax.ShapeDtypeStruct((st, (n_qc + n_kc) * LANES), BF16),
                   jax.ShapeDtypeStruct((st, q_rank), BF16),
                   jax.ShapeDtypeStruct((st, kv_rank), BF16)],
        compiler_params=_cparams("arbitrary"),
        name="prep_odd1",
    )(p, *rope128, gains, d_qa_norm.astype(F32)[None, :], d_kva_norm.astype(F32)[None, :])
    wq = jnp.pad(d_wq_up.reshape(q_rank, d_heads, D_QK), ((0, 0), (0, 0), (0, D_PAD - D_QK)))
    qd_raw = matmul([qa], [wq.reshape(q_rank, d_heads * D_PAD).astype(BF16)], tn=512)
    kv = matmul([kva], [d_wkv_up.astype(BF16)], tn=512)
    pad_gain = lambda gvec: jnp.pad(gvec.astype(F32), (0, D_PAD - D_QK))[None, :]
    qd, kd = pl.pallas_call(
        functools.partial(_prep_odd2_kernel, n_heads=d_heads, q_scale=D_QK ** -0.5 * LOG2E),
        grid=(st // tm,),
        in_specs=[pl.BlockSpec((tm, d_heads * D_PAD), lambda i: (i, 0)),
                  pl.BlockSpec((tm, d_heads * D_PAD), lambda i: (i, 0)),
                  pl.BlockSpec((tm, LANES), lambda i: (i, c_kr)),
                  tab, tab, tab,
                  pl.BlockSpec((1, D_PAD), lambda i: (0, 0)),
                  pl.BlockSpec((1, D_PAD), lambda i: (0, 0))],
        out_specs=[pl.BlockSpec((tm, d_heads * D_PAD), lambda i: (i, 0)),
                   pl.BlockSpec((tm, d_heads * D_PAD), lambda i: (i, 0))],
        out_shape=[jax.ShapeDtypeStruct((st, d_heads * D_PAD), BF16),
                   jax.ShapeDtypeStruct((st, d_heads * D_PAD), BF16)],
        compiler_params=_cparams("arbitrary"),
        name="prep_odd2",
    )(qd_raw, kv, p, *rope64, pad_gain(d_qn), pad_gain(d_kn))
    aux = jnp.zeros((8, LANES), F32)
    flag_c = _score_bound_flag(c_qn, c_kn, HEAD_DIM, HEAD_DIM ** -0.5 * LOG2E)
    flag_d = _score_bound_flag(d_qn, d_kn, D_QK, D_QK ** -0.5 * LOG2E)
    yc = flash_attention(qk, qk, p, aux, flag_c, n_kv=c_kv, g=gc, comps=1, dq=HEAD_DIM, dv=HEAD_DIM,
                         c_q=0, c_k=n_qc, c_v=lambda hh: c_vc + hh, tk=2048, key_major=True)
    yd = flash_attention(qd, kd, kv, aux, flag_d, n_kv=d_heads, g=1, comps=1, dq=D_PAD, dv=D_V,
                         c_q=0, c_k=0, c_v=lambda hh: 2 * hh + 1, tk=2048,
                         tq=1280 if st % 1280 == 0 else ROW_TILE, key_major=True)
    w_out = w_out.astype(BF16)
    nc = c_heads * HEAD_DIM
    return matmul([yc, yd], [w_out[:nc], w_out[nc:]], tn=512)


def kernel(x, c, ctx, c_ctx, adaln_down, adaln_up, adaln_b, norm_mix, norm_ffn, ev_w_in, ev_w_out, ev_a_qn, ev_a_kn,
           ev_a_sink, ev_b_qn, ev_b_kn, ev_b_lam, ev_b_subln, od_w_in, od_w_out, od_c_qn, od_c_kn, od_d_qa_norm,
           od_d_kva_norm, od_d_wq_up, od_d_wkv_up, od_d_qn, od_d_kn, router_w, router_b, moe_w_gu, moe_b_gu,
           moe_w_down, moe_b_down):
    bsz, seq, d = x.shape
    n_ctx = ctx.shape[1]
    depth = adaln_down.shape[0]
    n_exp = router_w.shape[2]
    assert bsz == 1 and n_ctx == ROW_TILE and seq % ROW_TILE == 0 and n_exp <= LANES

    rope128 = _rope_tables(n_ctx, seq, HEAD_DIM)
    rope64 = _rope_tables(n_ctx, seq, D_ROPE)
    cvecs = jnp.zeros((8, d), F32).at[0].set(c[0]).at[1].set(c_ctx)
    mods = adaln_all(cvecs, adaln_down, adaln_up, adaln_b)

    xs = jnp.concatenate([ctx[0], x[0]], axis=0)
    y_prev, gate_prev = None, None
    for l in range(depth):
        m_lat = mods[l, 0].reshape(N_MOD, d)
        m_ctx = mods[l, 1].reshape(N_MOD, d)
        mult = _streams(norm_mix[l] * (1.0 + m_ctx[1]), norm_mix[l] * (1.0 + m_lat[1]))
        shift = _streams(m_ctx[0], m_lat[0])
        if y_prev is None:
            (h,) = modulate(xs, mult, shift)
        else:
            xs, h = modulate(xs, mult, shift, res=(y_prev, gate_prev))
        j = l // 2
        if l % 2 == 0:
            lam_init = 0.8 - 0.6 * math.exp(-0.3 * l)
            y = _even_mixer(h, rope128, ev_w_in[j], ev_w_out[j], ev_a_qn[j], ev_a_kn[j], ev_a_sink[j], ev_b_qn[j],
                            ev_b_kn[j], ev_b_lam[j], ev_b_subln[j], lam_init, d)
        else:
            y = _odd_mixer(h, rope128, rope64, od_w_in[j], od_w_out[j], od_c_qn[j], od_c_kn[j], od_d_qa_norm[j],
                           od_d_kva_norm[j], od_d_wq_up[j], od_d_wkv_up[j], od_d_qn[j], od_d_kn[j], d)
        mult = _streams(norm_ffn[l] * (1.0 + m_ctx[4]), norm_ffn[l] * (1.0 + m_lat[4]))
        shift = _streams(m_ctx[3], m_lat[3])
        rw = _pad_cols(router_w[l].astype(F32), LANES)
        rb = jnp.pad(router_b[l].astype(F32), (0, LANES - n_exp))[None, :]
        xs, h2, route = modulate(xs, mult, shift, res=(y, _streams(m_ctx[2], m_lat[2])), router=(rw, rb, n_exp))
        y_prev = moe_sparse(h2, route, moe_w_gu[l].astype(BF16), moe_b_gu[l].astype(F32),
                            moe_w_down[l].astype(BF16), moe_b_down[l].astype(F32))
        gate_prev = _streams(m_ctx[5], m_lat[5])
    out = final_residual(xs, y_prev, gate_prev)
    return out[None]
```
